```python
import math
import jax
import jax.numpy as jnp
from jax import lax
import numpy as np

D_MODEL = 1024
BATCH = 8
SEQ = 4096
DEPTH = 2

CTX_LEN = 256
GRID_W = 64
ROPE_THETA = 10000.0
NORM_EPS = 1e-6
Q_BLOCK = 128
CONV_K = 5
N_BRANCH = 4
D_FF = 4 * D_MODEL

GDN_HEADS = 4
GDN_DK = 128
GDN_DV = 128
GDN_CHUNK = 64
MLA_HEADS = 8
MLA_NOPE = 64
MLA_ROPE = 32
MLA_V = 64
MLA_QK = MLA_NOPE + MLA_ROPE
MLA_Q_LORA = 384
MLA_KV_LORA = 256
DIFF_HEADS = 4
DIFF_HD = 64
LAMBDA_BASE = 0.8
LAMBDA_AMP = 0.6
LAMBDA_RATE = 0.3
SSM_HEADS = 8
SSM_HEAD_DIM = 64
SSM_GROUPS = 2
SSM_STATE = 128
SSM_CHUNK = 64

GDN_QK = GDN_HEADS * GDN_DK
GDN_VW = GDN_HEADS * GDN_DV
DIFF_QK = DIFF_HEADS * 2 * DIFF_HD
DIFF_VW = DIFF_HEADS * 2 * DIFF_HD
SSM_INNER = SSM_HEADS * SSM_HEAD_DIM
SSM_BC = SSM_GROUPS * SSM_STATE
BRANCH_W = 512

GDN_COLS = 2 * GDN_QK + 2 * GDN_VW + 4 * GDN_HEADS
MLA_COLS = MLA_Q_LORA + MLA_KV_LORA + MLA_ROPE
DIFF_COLS = 2 * DIFF_QK + DIFF_VW
SSM_COLS = 2 * SSM_INNER + 2 * SSM_BC + 2 * SSM_HEADS
MIX_COLS = GDN_COLS + MLA_COLS + DIFF_COLS + SSM_COLS
GATE_COLS = N_BRANCH * D_MODEL
IN_COLS = MIX_COLS + GATE_COLS

kernel_name = 'hybrid_gdn_mla_diff_ssd_prefix_block'


def _split(u, sizes):
    idx = np.cumsum(sizes)[:-1].tolist()
    return jnp.split(u, idx, axis=-1)


def rms_norm(x, g):
    xf = x.astype(jnp.float32)
    y = xf * lax.rsqrt(jnp.mean(xf * xf, axis=-1, keepdims=True) + NORM_EPS)
    return (y * g.astype(jnp.float32)).astype(x.dtype)


def l2_normalize(t):
    return t * lax.rsqrt(jnp.sum(t * t, axis=-1, keepdims=True) + 1e-6)


def modulate(h, shift, scale):
    return h * (1 + scale) + shift


def dw_conv(u, w):
    k_w, ch = w.shape
    return lax.conv_general_dilated(u, w[:, None, :].astype(u.dtype), window_strides=(1,),
                                    padding=[(k_w // 2, k_w // 2)],
                                    dimension_numbers=('NWC', 'WIO', 'NWC'),
                                    feature_group_count=ch)


def axial_rope(n_tok, rot_dim):
    rows = n_tok // GRID_W
    row = jnp.repeat(jnp.arange(rows, dtype=jnp.float32), GRID_W)
    col = jnp.tile(jnp.arange(GRID_W, dtype=jnp.float32), rows)
    quarter = rot_dim // 4
    inv = ROPE_THETA ** (-jnp.arange(quarter, dtype=jnp.float32) / quarter)
    ar = row[:, None] * inv
    ac = col[:, None] * inv
    ang = jnp.concatenate([ar, ar, ac, ac], axis=-1)
    return jnp.cos(ang), jnp.sin(ang)


def apply_rope(x, cos, sin):
    half = x.shape[-1] // 2
    qd = half // 2
    rh = lambda t: jnp.concatenate([-t[..., qd:], t[..., :qd]], axis=-1)
    rot = jnp.concatenate([rh(x[..., :half]), rh(x[..., half:])], axis=-1)
    return x * cos[None, :, None, :].astype(x.dtype) + rot * sin[None, :, None, :].astype(x.dtype)


def block_attend(q, k, v):
    bn, h, sq, d = q.shape
    nb = sq // Q_BLOCK
    qb = jnp.moveaxis(q.reshape(bn, h, nb, Q_BLOCK, d), 2, 0)
    scale = d ** -0.5

    def one_block(qi):
        s = jnp.einsum('bhqd,bhkd->bhqk', qi, k, preferred_element_type=jnp.float32) * scale
        p = jax.nn.softmax(s, axis=-1).astype(v.dtype)
        return jnp.einsum('bhqk,bhkd->bhqd', p, v)

    o = lax.map(one_block, qb)
    return jnp.moveaxis(o, 0, 2).reshape(bn, h, sq, v.shape[-1])


def diff_attend(q, k, v, lam):
    bn, h, m, sq, d = q.shape
    nb = sq // Q_BLOCK
    qb = jnp.moveaxis(q.reshape(bn, h, m, nb, Q_BLOCK, d), 3, 0)
    scale = d ** -0.5

    def one_block(qi):
        s = jnp.einsum('bhmqd,bhmkd->bhmqk', qi, k, preferred_element_type=jnp.float32) * scale
        p = jax.nn.softmax(s, axis=-1)
        w = (p[:, :, 0] - lam * p[:, :, 1]).astype(v.dtype)
        return jnp.einsum('bhqk,bhkd->bhqd', w, v)

    o = lax.map(one_block, qb)
    return jnp.moveaxis(o, 0, 2).reshape(bn, h, sq, v.shape[-1])


def gdn_chunk_scan(q, k, v, g, beta, s0):
    bn, h, s, dk = q.shape
    dv = v.shape[-1]
    c = GDN_CHUNK
    n = s // c
    q = q.reshape(bn, h, n, c, dk)
    k = k.reshape(bn, h, n, c, dk)
    v = v.reshape(bn, h, n, c, dv)
    g = jnp.cumsum(g.reshape(bn, h, n, c), axis=-1)
    beta = beta.reshape(bn, h, n, c)
    incl = jnp.tril(jnp.ones((c, c), bool))
    strict = jnp.tril(jnp.ones((c, c), bool), -1)
    diff = g[..., :, None] - g[..., None, :]
    decay = jnp.where(incl, jnp.exp(jnp.where(incl, diff, 0.0)), 0.0)
    kb = k * beta[..., None]
    lmat = jnp.where(strict, jnp.einsum('bhncd,bhnjd->bhncj', kb, k) * decay, 0.0)
    eye = jnp.broadcast_to(jnp.eye(c, dtype=lmat.dtype), lmat.shape)
    tmat = lax.linalg.triangular_solve(lmat + eye, eye, left_side=True, lower=True,
                                       unit_diagonal=True)
    eg = jnp.exp(g)
    u = tmat @ (v * beta[..., None])
    w = tmat @ (kb * eg[..., None])
    attn = jnp.einsum('bhncd,bhnjd->bhncj', q, k) * decay
    qg = q * eg[..., None]
    kd = k * jnp.exp(g[..., -1:] - g)[..., None]
    gl = jnp.exp(g[..., -1])

    def step(state, inp):
        u_c, w_c, a_c, qg_c, kd_c, gl_c = inp
        v_new = u_c - jnp.einsum('bhcd,bhde->bhce', w_c, state)
        o = jnp.einsum('bhcd,bhde->bhce', qg_c, state) + jnp.einsum('bhcj,bhje->bhce', a_c, v_new)
        state = state * gl_c[..., None, None] + jnp.einsum('bhcd,bhce->bhde', kd_c, v_new)
        return state, o

    xs = tuple(jnp.moveaxis(t, 2, 0) for t in (u, w, attn, qg, kd, gl))
    s_fin, o = lax.scan(step, s0, xs)
    return jnp.moveaxis(o, 0, 2).reshape(bn, h, s, dv), s_fin


def ssd_chunk_scan(xs, dt, a, bm, cm, s0):
    bn, s, h, p = xs.shape
    g_n, n_st = bm.shape[2], bm.shape[3]
    r = h // g_n
    c = SSM_CHUNK
    n = s // c
    xdt = (xs * dt[..., None]).reshape(bn, n, c, g_n, r, p)
    a_cum = jnp.moveaxis(jnp.cumsum((dt * a).reshape(bn, n, c, g_n, r), axis=2), 2, -1)
    bc = bm.reshape(bn, n, c, g_n, n_st)
    cc = cm.reshape(bn, n, c, g_n, n_st)
    incl = jnp.tril(jnp.ones((c, c), bool))
    seg = a_cum[..., :, None] - a_cum[..., None, :]
    lmat = jnp.where(incl, jnp.exp(jnp.where(incl, seg, 0.0)), 0.0)
    scores = jnp.einsum('bnlgz,bnsgz->bngls', cc, bc)
    y_diag = jnp.einsum('bngls,bngrls,bnsgrp->bnlgrp', scores, lmat, xdt)
    decay_states = jnp.exp(a_cum[..., -1:] - a_cum)
    chunk_states = jnp.einsum('bnsgz,bngrs,bnsgrp->bngrpz', bc, decay_states, xdt)
    chunk_decay = jnp.exp(a_cum[..., -1])
    in_decay = jnp.exp(a_cum)

    def step(state, inp):
        cs, cd, c_c, idc = inp
        y_off = jnp.einsum('blgz,bgrpz,bgrl->blgrp', c_c, state, idc)
        state = state * cd[..., None, None] + cs
        return state, y_off

    xs_scan = tuple(jnp.moveaxis(t, 1, 0) for t in (chunk_states, chunk_decay, cc, in_decay))
    s_fin, y_off = lax.scan(step, s0, xs_scan)
    y = y_diag + jnp.moveaxis(y_off, 0, 1)
    return y.reshape(bn, s, h, p), s_fin


def gdn_branch(u_lat, u_ctx, conv_w, a_log, dt_bias, norm_g, need_ctx):
    def prep(u):
        bn, s = u.shape[:2]
        q, k, v, z, a, b = _split(u, [GDN_QK, GDN_QK, GDN_VW, GDN_VW, 2 * GDN_HEADS, 2 * GDN_HEADS])
        qkv = jax.nn.silu(dw_conv(jnp.concatenate([q, k, v], axis=-1), conv_w))
        q, k, v = _split(qkv, [GDN_QK, GDN_QK, GDN_VW])
        heads = lambda t, d: t.reshape(bn, s, GDN_HEADS, d).transpose(0, 2, 1, 3).astype(jnp.float32)
        q = l2_normalize(heads(q, GDN_DK)) * (GDN_DK ** -0.5)
        k = l2_normalize(heads(k, GDN_DK))
        v = heads(v, GDN_DV)
        a = a.reshape(bn, s, 2, GDN_HEADS).astype(jnp.float32)
        b = b.reshape(bn, s, 2, GDN_HEADS).astype(jnp.float32)
        g = -jnp.exp(a_log.astype(jnp.float32)) * jax.nn.softplus(a + dt_bias.astype(jnp.float32))
        beta = jax.nn.sigmoid(b)
        return q, k, v, g.transpose(2, 0, 3, 1), beta.transpose(2, 0, 3, 1), z

    def run(pp, d, s_init):
        q, k, v, g, beta = pp[0], pp[1], pp[2], pp[3][d], pp[4][d]
        if d == 1:
            q, k, v = jnp.flip(q, 2), jnp.flip(k, 2), jnp.flip(v, 2)
            g, beta = jnp.flip(g, -1), jnp.flip(beta, -1)
        o, s_fin = gdn_chunk_scan(q, k, v, g, beta, s_init)
        return (jnp.flip(o, 2) if d == 1 else o), s_fin

    def finish(o, z):
        bn, h, s, dv = o.shape
        o = o.transpose(0, 2, 1, 3)
        y = rms_norm(o, norm_g) * jax.nn.silu(z.reshape(bn, s, h, dv).astype(jnp.float32))
        return y.reshape(bn, s, h * dv).astype(z.dtype)

    pc, pl = prep(u_ctx), prep(u_lat)
    s0 = jnp.zeros((u_lat.shape[0], GDN_HEADS, GDN_DK, GDN_DV), jnp.float32)
    oc_f, sc_f = run(pc, 0, s0)
    ol_f, _ = run(pl, 0, sc_f)
    oc_b, sc_b = run(pc, 1, s0)
    ol_b, _ = run(pl, 1, sc_b)
    y_lat = finish(ol_f + ol_b, pl[5])
    y_ctx = finish(oc_f + oc_b, pc[5]) if need_ctx else None
    return y_lat, y_ctx


def mla_branch(u_lat, u_ctx, w_uq, w_ukv, q_lora_g, kv_lora_g, qn_g, kn_g, cos, sin, need_ctx):
    def rope_tail(t):
        return jnp.concatenate([t[..., :MLA_NOPE], apply_rope(t[..., MLA_NOPE:], cos, sin)], axis=-1)

    def prep(u, rope, need_q):
        bn, s = u.shape[:2]
        cq, ckv, kpe = _split(u, [MLA_Q_LORA, MLA_KV_LORA, MLA_ROPE])
        kv = (rms_norm(ckv, kv_lora_g) @ w_ukv).reshape(bn, s, MLA_HEADS, MLA_NOPE + MLA_V)
        k_nope, v = kv[..., :MLA_NOPE], kv[..., MLA_NOPE:]
        k_pe = jnp.broadcast_to(kpe[:, :, None, :], (bn, s, MLA_HEADS, MLA_ROPE))
        k = rms_norm(jnp.concatenate([k_nope, k_pe], axis=-1), kn_g)
        if rope:
            k = rope_tail(k)
        q = None
        if need_q:
            q = rms_norm((rms_norm(cq, q_lora_g) @ w_uq).reshape(bn, s, MLA_HEADS, MLA_QK), qn_g)
            if rope:
                q = rope_tail(q)
            q = q.transpose(0, 2, 1, 3)
        return q, k.transpose(0, 2, 1, 3), v.transpose(0, 2, 1, 3)

    def to_seq(o):
        bn, h, s, dv = o.shape
        return o.transpose(0, 2, 1, 3).reshape(bn, s, h * dv)

    qc, kc, vc = prep(u_ctx, False, need_ctx)
    ql, kl, vl = prep(u_lat, True, True)
    y_lat = to_seq(block_attend(ql, jnp.concatenate([kl, kc], axis=2), jnp.concatenate([vl, vc], axis=2)))
    y_ctx = to_seq(block_attend(qc, kc, vc)) if need_ctx else None
    return y_lat, y_ctx


def diff_branch(u_lat, u_ctx, qn_g, kn_g, lam_p, sub_g, lam_init, cos, sin, need_ctx):
    def prep(u, rope, need_q):
        bn, s = u.shape[:2]
        q, k, v = _split(u, [DIFF_QK, DIFF_QK, DIFF_VW])

        def qk(t, g):
            t = rms_norm(t.reshape(bn, s, 2 * DIFF_HEADS, DIFF_HD), g)
            if rope:
                t = apply_rope(t, cos, sin)
            return t.reshape(bn, s, DIFF_HEADS, 2, DIFF_HD).transpose(0, 2, 3, 1, 4)

        kk = qk(k, kn_g)
        qq = qk(q, qn_g) if need_q else None
        vv = v.reshape(bn, s, DIFF_HEADS, 2 * DIFF_HD).transpose(0, 2, 1, 3)
        return qq, kk, vv

    lp = lam_p.astype(jnp.float32)
    lam = jnp.exp(jnp.sum(lp[0] * lp[1])) - jnp.exp(jnp.sum(lp[2] * lp[3])) + lam_init

    def finish(o):
        bn, h, s, dv = o.shape
        o = rms_norm(o.transpose(0, 2, 1, 3), sub_g) * (1.0 - lam_init)
        return o.reshape(bn, s, h * dv)

    qc, kc, vc = prep(u_ctx, False, need_ctx)
    ql, kl, vl = prep(u_lat, True, True)
    y_lat = finish(diff_attend(ql, jnp.concatenate([kl, kc], axis=3), jnp.concatenate([vl, vc], axis=2), lam))
    y_ctx = finish(diff_attend(qc, kc, vc, lam)) if need_ctx else None
    return y_lat, y_ctx


def ssm_branch(u_lat, u_ctx, conv_w, conv_b, a_log, dt_bias, d_skip, norm_g, need_ctx):
    def prep(u):
        bn, s = u.shape[:2]
        z, xbc, dt = _split(u, [SSM_INNER, SSM_INNER + 2 * SSM_BC, 2 * SSM_HEADS])
        xbc = jax.nn.silu(dw_conv(xbc, conv_w) + conv_b)
        xs, bm, cm = _split(xbc, [SSM_INNER, SSM_BC, SSM_BC])
        xs = xs.reshape(bn, s, SSM_HEADS, SSM_HEAD_DIM).astype(jnp.float32)
        bm = bm.reshape(bn, s, SSM_GROUPS, SSM_STATE).astype(jnp.float32)
        cm = cm.reshape(bn, s, SSM_GROUPS, SSM_STATE).astype(jnp.float32)
        dt = jax.nn.softplus(dt.reshape(bn, s, 2, SSM_HEADS).astype(jnp.float32) + dt_bias.astype(jnp.float32))
        return xs, dt, bm, cm, z

    a = -jnp.exp(a_log.astype(jnp.float32))

    def run(pp, d, s_init):
        xs, dt, bm, cm = pp[0], pp[1][:, :, d], pp[2], pp[3]
        if d == 1:
            xs, dt, bm, cm = (jnp.flip(t, 1) for t in (xs, dt, bm, cm))
        y, s_fin = ssd_chunk_scan(xs, dt, a[d], bm, cm, s_init)
        return (jnp.flip(y, 1) if d == 1 else y), s_fin

    def finish(pp, y):
        xs, z = pp[0], pp[4]
        bn, s = xs.shape[:2]
        y = y + d_skip.astype(jnp.float32)[:, None] * xs
        gsz = SSM_INNER // SSM_GROUPS
        y = y.reshape(bn, s, SSM_GROUPS, gsz) * jax.nn.silu(z.reshape(bn, s, SSM_GROUPS, gsz).astype(jnp.float32))
        return rms_norm(y, norm_g.reshape(SSM_GROUPS, gsz)).reshape(bn, s, SSM_INNER).astype(z.dtype)

    pc, pl = prep(u_ctx), prep(u_lat)
    s0 = jnp.zeros((u_lat.shape[0], SSM_GROUPS, SSM_HEADS // SSM_GROUPS, SSM_HEAD_DIM, SSM_STATE), jnp.float32)
    yc_f, sc_f = run(pc, 0, s0)
    yl_f, _ = run(pl, 0, sc_f)
    yc_b, sc_b = run(pc, 1, s0)
    yl_b, _ = run(pl, 1, sc_b)
    y_lat = finish(pl, yl_f + yl_b)
    y_ctx = finish(pc, yc_f + yc_b) if need_ctx else None
    return y_lat, y_ctx


def merge_branches(ys, gate_logits, w_branch, w_out):
    gates = jnp.split(gate_logits, N_BRANCH, axis=-1)
    m = jax.nn.sigmoid(gates[0]) * (ys[0] @ w_branch[0])
    for i in range(1, N_BRANCH):
        m = m + jax.nn.sigmoid(gates[i]) * (ys[i] @ w_branch[i])
    return m @ w_out


def sq_relu_mlp(h, w1, w2):
    return jnp.square(jax.nn.relu(h @ w1)) @ w2


def setup_inputs(seed: int = 0) -> dict:
    key = jax.random.key(seed)
    ks = list(jax.random.split(key, 64))
    nk = lambda: ks.pop()
    nrm = lambda shape, std: std * jax.random.normal(nk(), shape, jnp.float32)
    gain = lambda shape: 1.0 + nrm(shape, 0.02)

    def dt_bias(shape):
        dt = jnp.exp(jax.random.uniform(nk(), shape, jnp.float32, math.log(1e-3), math.log(1e-1)))
        return dt + jnp.log(-jnp.expm1(-dt))

    def a_log(shape):
        return jnp.log(jax.random.uniform(nk(), shape, jnp.float32, 1.0, 16.0))

    L = DEPTH
    return {
        'x': nrm((BATCH, SEQ, D_MODEL), 1.0),
        'c': nrm((BATCH, D_MODEL), 1.0),
        'ctx': nrm((BATCH, CTX_LEN, D_MODEL), 1.0),
        'c_ctx': nrm((D_MODEL,), 1.0),
        'ada_w': nrm((L, D_MODEL, 6 * D_MODEL), 0.5 * D_MODEL ** -0.5),
        'ada_b': nrm((L, 6 * D_MODEL), 0.01),
        'norm1_g': gain((L, D_MODEL)),
        'norm2_g': gain((L, D_MODEL)),
        'w_in': nrm((L, D_MODEL, IN_COLS), D_MODEL ** -0.5),
        'gdn_conv': nrm((L, CONV_K, 2 * GDN_QK + GDN_VW), CONV_K ** -0.5),
        'gdn_a_log': a_log((L, 2, GDN_HEADS)),
        'gdn_dt_bias': dt_bias((L, 2, GDN_HEADS)),
        'gdn_norm_g': gain((L, GDN_DV)),
        'mla_q_lora_g': gain((L, MLA_Q_LORA)),
        'mla_kv_lora_g': gain((L, MLA_KV_LORA)),
        'mla_w_uq': nrm((L, MLA_Q_LORA, MLA_HEADS * MLA_QK), MLA_Q_LORA ** -0.5),
        'mla_w_ukv': nrm((L, MLA_KV_LORA, MLA_HEADS * (MLA_NOPE + MLA_V)), MLA_KV_LORA ** -0.5),
        'mla_qn_g': gain((L, MLA_QK)),
        'mla_kn_g': gain((L, MLA_QK)),
        'diff_qn_g': gain((L, DIFF_HD)),
        'diff_kn_g': gain((L, DIFF_HD)),
        'diff_lambda': nrm((L, 4, DIFF_HD), 0.1),
        'diff_sub_g': gain((L, 2 * DIFF_HD)),
        'ssm_conv': nrm((L, CONV_K, SSM_INNER + 2 * SSM_BC), CONV_K ** -0.5),
        'ssm_conv_b': nrm((L, SSM_INNER + 2 * SSM_BC), 0.01),
        'ssm_a_log': a_log((L, 2, SSM_HEADS)),
        'ssm_dt_bias': dt_bias((L, 2, SSM_HEADS)),
        'ssm_d': gain((L, SSM_HEADS)),
        'ssm_norm_g': gain((L, SSM_INNER)),
        'w_branch': nrm((L, N_BRANCH, BRANCH_W, D_MODEL), BRANCH_W ** -0.5),
        'w_out': nrm((L, D_MODEL, D_MODEL), D_MODEL ** -0.5),
        'mlp_w1': nrm((L, D_MODEL, D_FF), D_MODEL ** -0.5),
        'mlp_w2': nrm((L, D_FF, D_MODEL), D_FF ** -0.5),
    }


def reference(x, c, ctx, c_ctx, ada_w, ada_b, norm1_g, norm2_g, w_in, gdn_conv, gdn_a_log,
              gdn_dt_bias, gdn_norm_g, mla_q_lora_g, mla_kv_lora_g, mla_w_uq, mla_w_ukv, mla_qn_g,
              mla_kn_g, diff_qn_g, diff_kn_g, diff_lambda, diff_sub_g, ssm_conv, ssm_conv_b,
              ssm_a_log, ssm_dt_bias, ssm_d, ssm_norm_g, w_branch, w_out, mlp_w1, mlp_w2):
    n_tok = x.shape[1]
    cos_m, sin_m = axial_rope(n_tok, MLA_ROPE)
    cos_d, sin_d = axial_rope(n_tok, DIFF_HD)
    xc = ctx
    s_c = jax.nn.silu(c)
    s_cc = jax.nn.silu(c_ctx)
    for l in range(DEPTH):
        last = l == DEPTH - 1
        need_ctx = not last
        mod = jnp.split((s_c @ ada_w[l] + ada_b[l])[:, None, :], 6, axis=-1)
        modc = jnp.split(s_cc @ ada_w[l] + ada_b[l], 6, axis=-1)
        h = modulate(rms_norm(x, norm1_g[l]), mod[0], mod[1])
        hc = modulate(rms_norm(xc, norm1_g[l]), modc[0], modc[1])
        w_in_l = w_in[l]
        u = h @ w_in_l
        uc = hc @ (w_in_l if need_ctx else w_in_l[:, :MIX_COLS])
        ua, ub, ucd, ud, gates = _split(u, [GDN_COLS, MLA_COLS, DIFF_COLS, SSM_COLS, GATE_COLS])
        parts_c = _split(uc, [GDN_COLS, MLA_COLS, DIFF_COLS, SSM_COLS] + ([GATE_COLS] if need_ctx else []))
        lam_init = LAMBDA_BASE - LAMBDA_AMP * math.exp(-LAMBDA_RATE * l)
        ya = gdn_branch(ua, parts_c[0], gdn_conv[l], gdn_a_log[l], gdn_dt_bias[l], gdn_norm_g[l], need_ctx)
        yb = mla_branch(ub, parts_c[1], mla_w_uq[l], mla_w_ukv[l], mla_q_lora_g[l], mla_kv_lora_g[l],
                        mla_qn_g[l], mla_kn_g[l], cos_m, sin_m, need_ctx)
        yc = diff_branch(ucd, parts_c[2], diff_qn_g[l], diff_kn_g[l], diff_lambda[l], diff_sub_g[l],
                         lam_init, cos_d, sin_d, need_ctx)
        yd = ssm_branch(ud, parts_c[3], ssm_conv[l], ssm_conv_b[l], ssm_a_log[l], ssm_dt_bias[l],
                        ssm_d[l], ssm_norm_g[l], need_ctx)
        x = x + mod[2] * merge_branches([ya[0], yb[0], yc[0], yd[0]], gates, w_branch[l], w_out[l])
        if need_ctx:
            xc = xc + modc[2] * merge_branches([ya[1], yb[1], yc[1], yd[1]], parts_c[4], w_branch[l], w_out[l])
        h2 = modulate(rms_norm(x, norm2_g[l]), mod[3], mod[4])
        x = x + mod[5] * sq_relu_mlp(h2, mlp_w1[l], mlp_w2[l])
        if need_ctx:
            hc2 = modulate(rms_norm(xc, norm2_g[l]), modc[3], modc[4])
            xc = xc + modc[5] * sq_relu_mlp(hc2, mlp_w1[l], mlp_w2[l])
    return x
```

```python
import functools
import math

import jax
import jax.numpy as jnp
import numpy as np
from jax import lax
from jax.experimental import pallas as pl
from jax.experimental.pallas import tpu as pltpu

F32 = jnp.float32
BF16 = jnp.bfloat16

D_MODEL = 1024
CTX_LEN = 256
GRID_W = 64
ROPE_THETA = 10000.0
NORM_EPS = 1e-6
CONV_K = 5
N_BRANCH = 4
D_FF = 4 * D_MODEL
GDN_HEADS = 4
GDN_DK = 128
GDN_DV = 128
MLA_HEADS = 8
MLA_NOPE = 64
MLA_ROPE = 32
MLA_V = 64
MLA_QK = MLA_NOPE + MLA_ROPE
MLA_Q_LORA = 384
MLA_KV_LORA = 256
DIFF_HEADS = 4
DIFF_HD = 64
LAMBDA_BASE = 0.8
LAMBDA_AMP = 0.6
LAMBDA_RATE = 0.3
SSM_HEADS = 8
SSM_HEAD_DIM = 64
SSM_GROUPS = 2
SSM_STATE = 128
GDN_QK = GDN_HEADS * GDN_DK
GDN_VW = GDN_HEADS * GDN_DV
DIFF_QK = DIFF_HEADS * 2 * DIFF_HD
DIFF_VW = DIFF_HEADS * 2 * DIFF_HD
SSM_INNER = SSM_HEADS * SSM_HEAD_DIM
SSM_BC = SSM_GROUPS * SSM_STATE
BRANCH_W = 512
GDN_COLS = 2 * GDN_QK + 2 * GDN_VW + 4 * GDN_HEADS
MLA_COLS = MLA_Q_LORA + MLA_KV_LORA + MLA_ROPE
DIFF_COLS = 2 * DIFF_QK + DIFF_VW
SSM_COLS = 2 * SSM_INNER + 2 * SSM_BC + 2 * SSM_HEADS
MIX_COLS = GDN_COLS + MLA_COLS + DIFF_COLS + SSM_COLS
GATE_COLS = N_BRANCH * D_MODEL

LANES = 128
TILE = 256
CHUNK = 64
CPT = TILE // CHUNK
HALO = 8
VMEM_LIMIT = 56 * 1024 * 1024

C_GDN = 0
C_XBC = 2048
C_DIFF = 3072
C_MLA = 4608
C_SMALL = 5376
C_SSMZ = 5632
C_GATE = 6144
N_IN = 10240
IN_TN = 2560

SM_A = 0
SM_B = 8
SM_DT = 16


def _cparams(sem):
    return pltpu.CompilerParams(dimension_semantics=sem, vmem_limit_bytes=VMEM_LIMIT)


def _dot(a, b):
    return jnp.dot(a, b, preferred_element_type=F32)


def _dot_nt(a, b):
    return lax.dot_general(a, b, (((1,), (1,)), ((), ())), preferred_element_type=F32)


def _dot_tn(a, b):
    return lax.dot_general(a, b, (((0,), (0,)), ((), ())), preferred_element_type=F32)


def _split3(x):
    x1 = x.astype(BF16)
    r1 = x - x1.astype(F32)
    x2 = r1.astype(BF16)
    r2 = r1 - x2.astype(F32)
    return x1, x2, r2.astype(BF16)


def _dot_exact_lhs(a_bf, x):
    x1, x2, x3 = _split3(x)
    return _dot(a_bf, x1) + _dot(a_bf, x2) + _dot(a_bf, x3)


def _dot_exact_rhs(x, b_bf):
    x1, x2, x3 = _split3(x)
    return _dot(x1, b_bf) + _dot(x2, b_bf) + _dot(x3, b_bf)


def _sigmoid(x):
    return 1.0 / (1.0 + jnp.exp(-x))


def _silu(x):
    return x * _sigmoid(x)


def _softplus(x):
    return jnp.maximum(x, 0.0) + jnp.log1p(jnp.exp(-jnp.abs(x)))


def _iota(shape, dim):
    return lax.broadcasted_iota(jnp.int32, shape, dim)


def _chunk_tri(n, upper):
    r = _iota((n, n), 0)
    c = _iota((n, n), 1)
    same = (r // CHUNK) == (c // CHUNK)
    tri = (r <= c) if upper else (r >= c)
    return jnp.where(same & tri, 1.0, 0.0).astype(BF16)


def _expander(n_rows, first_row, n_groups, width):
    r = _iota((n_rows, n_groups * width), 0)
    c = _iota((n_rows, n_groups * width), 1)
    return jnp.where(r == first_row + c // width, 1.0, 0.0).astype(BF16)


def _ada_kernel(c_ref, w_ref, b_ref, o_ref):
    a = _silu(c_ref[...])
    a1, a2, a3 = _split3(a)
    w1, w2, w3 = _split3(w_ref[...])
    acc = _dot(a1, w1) + (_dot(a1, w2) + _dot(a2, w1)) + (_dot(a1, w3) + _dot(a2, w2) + _dot(a3, w1))
    o_ref[...] = acc + b_ref[...]


def _ada(cc, w, b):
    rows, d = cc.shape
    n = w.shape[1]
    tn = 1536
    return pl.pallas_call(
        _ada_kernel,
        grid=(n // tn,),
        in_specs=[pl.BlockSpec((rows, d), lambda j: (0, 0)),
                  pl.BlockSpec((d, tn), lambda j: (0, j)),
                  pl.BlockSpec((1, tn), lambda j: (0, j))],
        out_specs=pl.BlockSpec((rows, tn), lambda j: (0, j)),
        out_shape=jax.ShapeDtypeStruct((rows, n), F32),
        compiler_params=_cparams(("arbitrary",)),
        name="ada",
    )(cc, w, b)


def _rms(x, g):
    ms = jnp.mean(x * x, axis=-1, keepdims=True)
    return x * lax.rsqrt(ms + NORM_EPS) * g


def _inproj_kernel(x_ref, mod_ref, g_ref, w_ref, o_ref):
    h = _rms(x_ref[0], g_ref[...])
    h = h * (1.0 + mod_ref[1:2, :]) + mod_ref[0:1, :]
    o_ref[0] = _dot(h.astype(BF16), w_ref[...])


def _inproj(xs, mods, g, w):
    b, n_tok, d = xs.shape
    t = n_tok // TILE
    nt = N_IN // IN_TN
    return pl.pallas_call(
        _inproj_kernel,
        grid=(nt, b, t),
        in_specs=[pl.BlockSpec((1, TILE, d), lambda n, i, j: (i, j, 0)),
                  pl.BlockSpec((None, None, 6, d), lambda n, i, j: (i, jnp.minimum(j, 1), 0, 0)),
                  pl.BlockSpec((1, d), lambda n, i, j: (0, 0)),
                  pl.BlockSpec((d, IN_TN), lambda n, i, j: (0, n))],
        out_specs=pl.BlockSpec((1, TILE, IN_TN), lambda n, i, j: (i, j, n)),
        out_shape=jax.ShapeDtypeStruct((b, n_tok, N_IN), F32),
        compiler_params=_cparams(("arbitrary", "arbitrary", "arbitrary")),
        name="inproj",
    )(xs, mods, g, w)


def _halo_specs(width, col_block, n_tiles, tile_of):
    rpt = TILE // HALO
    last = n_tiles * rpt - 1
    main = pl.BlockSpec((1, TILE, width), lambda *ids: (ids[0], tile_of(*ids), col_block))
    prev = pl.BlockSpec((1, HALO, width),
                        lambda *ids: (ids[0], jnp.maximum(tile_of(*ids) * rpt - 1, 0), col_block))
    nxt = pl.BlockSpec((1, HALO, width),
                       lambda *ids: (ids[0], jnp.minimum(tile_of(*ids) * rpt + rpt, last), col_block))
    return main, prev, nxt


def _conv_tile(main_ref, prev_ref, next_ref, w_ref, ext_ref, tile, n_tiles):
    prev_ok = (tile >= 2).astype(F32)
    next_ok = jnp.logical_and(tile >= 1, tile < n_tiles - 1).astype(F32)
    ext_ref[0:HALO, :] = prev_ref[0] * prev_ok
    ext_ref[HALO:HALO + TILE, :] = main_ref[0]
    ext_ref[HALO + TILE:2 * HALO + TILE, :] = next_ref[0] * next_ok
    half = CONV_K // 2
    acc = None
    for k in range(CONV_K):
        term = w_ref[k:k + 1, :] * ext_ref[HALO - half + k:HALO - half + k + TILE, :]
        acc = term if acc is None else acc + term
    return acc


def _neumann_inverse(a):
    n = a.shape[0]
    eye = jnp.where(_iota((n, n), 0) == _iota((n, n), 1), 1.0, 0.0).astype(F32)
    m = -a
    p = eye + m
    steps = int(math.log2(n)) - 1
    for _ in range(steps):
        mb = m.astype(BF16)
        m = _dot(mb, mb)
        p = p + _dot(p.astype(BF16), m.astype(BF16))
    return p


def _gdn_prep_kernel(main_ref, prev_ref, next_ref, sm_ref, cw_ref, alog_ref, dtb_ref,
                     u0_ref, wq0_ref, kd0_ref, at0_ref, gl0_ref,
                     u1_ref, wq1_ref, kd1_ref, at1_ref, gl1_ref,
                     ext_ref, *, n_tiles):
    tile = pl.program_id(1)
    qkv = _silu(_conv_tile(main_ref, prev_ref, next_ref, cw_ref, ext_ref, tile, n_tiles))
    sm = sm_ref[0]
    g_all = -jnp.exp(alog_ref[...]) * _softplus(sm + dtb_ref[...])
    beta_all = _sigmoid(sm)
    nh = GDN_HEADS
    exp_g = _expander(LANES, SM_A, 2 * nh, LANES)
    exp_b = _expander(LANES, SM_B, 2 * nh, LANES)
    beta_x = _dot_exact_rhs(beta_all, exp_b)
    outs = ((u0_ref, wq0_ref, kd0_ref, at0_ref, gl0_ref), (u1_ref, wq1_ref, kd1_ref, at1_ref, gl1_ref))
    cums = []
    for d in range(2):
        cum = _dot_exact_lhs(_chunk_tri(TILE, upper=(d == 1)), g_all)
        cums.append((_dot_exact_rhs(cum, exp_g), cum.T))
    ri = _iota((CHUNK, CHUNK), 0)
    ci = _iota((CHUNK, CHUNK), 1)
    for h in range(nh):
        q = qkv[:, h * GDN_DK:(h + 1) * GDN_DK]
        k = qkv[:, GDN_QK + h * GDN_DK:GDN_QK + (h + 1) * GDN_DK]
        v = qkv[:, 2 * GDN_QK + h * GDN_DV:2 * GDN_QK + (h + 1) * GDN_DV]
        q = q * lax.rsqrt(jnp.sum(q * q, axis=-1, keepdims=True) + 1e-6) * (GDN_DK ** -0.5)
        k = k * lax.rsqrt(jnp.sum(k * k, axis=-1, keepdims=True) + 1e-6)
        for c in range(CPT):
            rows = slice(c * CHUNK, (c + 1) * CHUNK)
            qc, kc, vc = q[rows], k[rows], v[rows]
            kcb = kc.astype(BF16)
            kk = _dot_nt(kcb, kcb)
            qk = _dot_nt(qc.astype(BF16), kcb)
            for d in range(2):
                col = d * nh + h
                lanes = slice(col * LANES, (col + 1) * LANES)
                cum_x, cum_t = cums[d]
                g_col = cum_x[rows, lanes]
                g_row = cum_t[SM_A + col:SM_A + col + 1, rows]
                b_col = beta_x[rows, lanes]
                last = CHUNK - 1 if d == 0 else 0
                g_last = g_col[last:last + 1, :]
                incl = (ri >= ci) if d == 0 else (ri <= ci)
                strict = (ri > ci) if d == 0 else (ri < ci)
                diff = g_col[:, :CHUNK] - g_row
                decay = jnp.where(incl, jnp.exp(jnp.where(incl, diff, 0.0)), 0.0)
                a_mat = jnp.where(strict, kk * b_col[:, :CHUNK] * decay, 0.0)
                t_mat = _neumann_inverse(a_mat).astype(BF16)
                e_g = jnp.exp(g_col)
                u = _dot(t_mat, (vc * b_col).astype(BF16))
                w = _dot(t_mat, (kc * b_col * e_g).astype(BF16))
                attn = jnp.where(incl, qk * decay, 0.0)
                u_ref, wq_ref, kd_ref, at_ref, gl_ref = outs[d]
                slot = c if d == 0 else CPT - 1 - c
                hl = slice(h * LANES, (h + 1) * LANES)
                u_ref[0, 0, slot, :, hl] = u
                wq_ref[0, 0, slot, 0:CHUNK, hl] = w.astype(BF16)
                wq_ref[0, 0, slot, CHUNK:2 * CHUNK, hl] = (qc * e_g).astype(BF16)
                kd_ref[0, 0, slot, :, hl] = (kc * jnp.exp(g_last - g_col)).astype(BF16)
                at_ref[0, 0, slot, :, h * CHUNK:(h + 1) * CHUNK] = attn.astype(BF16)
                gl_ref[0, 0, slot, :, hl] = jnp.exp(g_last)


def _bwd_block(j, n_tiles):
    return jnp.where(j == 0, 0, n_tiles - j)


def _gdn_prep(u, conv_w, alog_row, dtb_row):
    b, n_tok, _ = u.shape
    t = n_tok // TILE
    w = GDN_VW
    cw = 2 * GDN_QK + GDN_VW
    main, prev, nxt = _halo_specs(cw, 0, t, lambda i, j: j)
    fwd = lambda i, j: (i, j, 0, 0, 0)
    bwd = lambda i, j: (i, _bwd_block(j, t), 0, 0, 0)

    def outs(imap):
        return [pl.BlockSpec((1, 1, CPT, CHUNK, w), imap),
                pl.BlockSpec((1, 1, CPT, 2 * CHUNK, w), imap),
                pl.BlockSpec((1, 1, CPT, CHUNK, w), imap),
                pl.BlockSpec((1, 1, CPT, CHUNK, GDN_HEADS * CHUNK), imap),
                pl.BlockSpec((1, 1, CPT, 1, w), imap)]

    shapes = [jax.ShapeDtypeStruct((b, t, CPT, CHUNK, w), F32),
              jax.ShapeDtypeStruct((b, t, CPT, 2 * CHUNK, w), BF16),
              jax.ShapeDtypeStruct((b, t, CPT, CHUNK, w), BF16),
              jax.ShapeDtypeStruct((b, t, CPT, CHUNK, GDN_HEADS * CHUNK), BF16),
              jax.ShapeDtypeStruct((b, t, CPT, 1, w), F32)]
    return pl.pallas_call(
        functools.partial(_gdn_prep_kernel, n_tiles=t),
        grid=(b, t),
        in_specs=[main, prev, nxt,
                  pl.BlockSpec((1, TILE, LANES), lambda i, j: (i, j, C_SMALL // LANES)),
                  pl.BlockSpec((CONV_K, cw), lambda i, j: (0, 0)),
                  pl.BlockSpec((1, LANES), lambda i, j: (0, 0)),
                  pl.BlockSpec((1, LANES), lambda i, j: (0, 0))],
        out_specs=outs(fwd) + outs(bwd),
        out_shape=shapes + shapes,
        scratch_shapes=[pltpu.VMEM((TILE + 2 * HALO, cw), F32)],
        compiler_params=_cparams(("arbitrary", "arbitrary")),
        name="gdn_prep",
    )(u, u, u, u, conv_w, alog_row, dtb_row)


def _gdn_scan_kernel(u0_ref, wq0_ref, kd0_ref, at0_ref, gl0_ref,
                     u1_ref, wq1_ref, kd1_ref, at1_ref, gl1_ref,
                     of_ref, ob_ref, s_ref):
    @pl.when(pl.program_id(1) == 0)
    def _():
        s_ref[...] = jnp.zeros_like(s_ref)

    ins = ((u0_ref, wq0_ref, kd0_ref, at0_ref, gl0_ref), (u1_ref, wq1_ref, kd1_ref, at1_ref, gl1_ref))
    for slot in range(CPT):
        for d in range(2):
            u_ref, wq_ref, kd_ref, at_ref, gl_ref = ins[d]
            o_ref = of_ref if d == 0 else ob_ref
            c = slot if d == 0 else CPT - 1 - slot
            for h in range(GDN_HEADS):
                hl = slice(h * LANES, (h + 1) * LANES)
                s = s_ref[d, h]
                r = _dot(wq_ref[0, 0, slot, :, hl], s.astype(BF16))
                v_new = u_ref[0, 0, slot, :, hl] - r[0:CHUNK]
                vb = v_new.astype(BF16)
                o = r[CHUNK:2 * CHUNK] + _dot(at_ref[0, 0, slot, :, h * CHUNK:(h + 1) * CHUNK], vb)
                s_ref[d, h] = s * gl_ref[0, 0, slot, :, hl] + _dot_tn(kd_ref[0, 0, slot, :, hl], vb)
                o_ref[0, c * CHUNK:(c + 1) * CHUNK, hl] = o


def _gdn_scan(prep):
    b, t = prep[0].shape[:2]
    w = GDN_VW
    imap = lambda i, j: (i, j, 0, 0, 0)
    specs = [pl.BlockSpec((1, 1, CPT, CHUNK, w), imap),
             pl.BlockSpec((1, 1, CPT, 2 * CHUNK, w), imap),
             pl.BlockSpec((1, 1, CPT, CHUNK, w), imap),
             pl.BlockSpec((1, 1, CPT, CHUNK, GDN_HEADS * CHUNK), imap),
             pl.BlockSpec((1, 1, CPT, 1, w), imap)]
    return pl.pallas_call(
        _gdn_scan_kernel,
        grid=(b, t),
        in_specs=specs + specs,
        out_specs=[pl.BlockSpec((1, TILE, w), lambda i, j: (i, j, 0)),
                   pl.BlockSpec((1, TILE, w), lambda i, j: (i, _bwd_block(j, t), 0))],
        out_shape=[jax.ShapeDtypeStruct((b, t * TILE, w), F32)] * 2,
        scratch_shapes=[pltpu.VMEM((2, GDN_HEADS, GDN_DK, GDN_DV), F32)],
        compiler_params=_cparams(("arbitrary", "arbitrary")),
        name="gdn_scan",
    )(*prep)


def _ssd_kernel(mf_ref, pf_ref, nf_ref, smf_ref, mb_ref, pb_ref, nb_ref, smb_ref,
                cw_ref, cb_ref, alog_ref, dtb_ref, dskip_ref,
                yf_ref, yb_ref, ext_ref, st_ref, *, n_tiles):
    j = pl.program_id(1)

    @pl.when(j == 0)
    def _():
        st_ref[...] = jnp.zeros_like(st_ref)

    nh, hd, ng = SSM_HEADS, SSM_HEAD_DIM, SSM_GROUPS
    hpg = nh // ng
    gw = hpg * hd
    ri = _iota((CHUNK, CHUNK), 0)
    ci = _iota((CHUNK, CHUNK), 1)
    dirs = ((mf_ref, pf_ref, nf_ref, smf_ref, yf_ref, j), (mb_ref, pb_ref, nb_ref, smb_ref, yb_ref, _bwd_block(j, n_tiles)))
    for d in range(2):
        m_ref, p_ref, n_ref, sm_ref, y_ref, tile = dirs[d]
        xbc = _silu(_conv_tile(m_ref, p_ref, n_ref, cw_ref, ext_ref, tile, n_tiles) + cb_ref[...])
        xs = xbc[:, :SSM_INNER]
        dt_all = _softplus(sm_ref[0] + dtb_ref[...])
        da_all = dt_all * (-jnp.exp(alog_ref[...]))
        cum = _dot_exact_lhs(_chunk_tri(TILE, upper=(d == 1)), da_all)
        cum_t = cum.T
        expand = _expander(LANES, SM_DT + d * nh, nh, hd)
        dt_x = _dot_exact_rhs(dt_all, expand)
        cum_x = _dot_exact_rhs(cum, expand)
        xdt = xs * dt_x
        incl = (ri >= ci) if d == 0 else (ri <= ci)
        for step in range(CPT):
            c = step if d == 0 else CPT - 1 - step
            rows = slice(c * CHUNK, (c + 1) * CHUNK)
            last = c * CHUNK + (CHUNK - 1 if d == 0 else 0)
            cum_c = cum_x[rows]
            cum_last = cum_x[last:last + 1]
            decay_states = jnp.exp(cum_last - cum_c)
            in_decay = jnp.exp(cum_c)
            chunk_decay = jnp.exp(cum_last)
            for g in range(ng):
                gl = slice(g * gw, (g + 1) * gw)
                bm = xbc[rows, SSM_INNER + g * SSM_STATE:SSM_INNER + (g + 1) * SSM_STATE].astype(BF16)
                cm = xbc[rows, SSM_INNER + SSM_BC + g * SSM_STATE:SSM_INNER + SSM_BC + (g + 1) * SSM_STATE].astype(BF16)
                scores = _dot_nt(cm, bm)
                state = st_ref[d, g]
                y_off = _dot(cm, state.astype(BF16)) * in_decay[:, gl]
                new_states = _dot_tn(bm, (xdt[rows, gl] * decay_states[:, gl]).astype(BF16))
                st_ref[d, g] = state * chunk_decay[:, gl] + new_states
                for hh in range(hpg):
                    h = g * hpg + hh
                    hl = slice(h * hd, (h + 1) * hd)
                    col = SM_DT + d * nh + h
                    seg = cum_c[:, hl] - cum_t[col:col + 1, rows]
                    lmat = jnp.where(incl, jnp.exp(jnp.where(incl, seg, 0.0)), 0.0)
                    y = _dot((scores * lmat).astype(BF16), xdt[rows, hl].astype(BF16)) + y_off[:, hh * hd:(hh + 1) * hd]
                    if d == 0:
                        y = y + dskip_ref[:, hl] * xs[rows, hl]
                    y_ref[0, rows, hl] = y


def _ssd(u, conv_w, conv_b, alog_row, dtb_row, dskip_row):
    b, n_tok, _ = u.shape
    t = n_tok // TILE
    cw = SSM_INNER + 2 * SSM_BC
    cb = C_XBC // cw
    mf, pf, nf = _halo_specs(cw, cb, t, lambda i, j: j)
    mb, pb, nb = _halo_specs(cw, cb, t, lambda i, j: _bwd_block(j, t))
    smf = pl.BlockSpec((1, TILE, LANES), lambda i, j: (i, j, C_SMALL // LANES))
    smb = pl.BlockSpec((1, TILE, LANES), lambda i, j: (i, _bwd_block(j, t), C_SMALL // LANES))
    const = lambda r, c: pl.BlockSpec((r, c), lambda i, j: (0, 0))
    return pl.pallas_call(
        functools.partial(_ssd_kernel, n_tiles=t),
        grid=(b, t),
        in_specs=[mf, pf, nf, smf, mb, pb, nb, smb,
                  const(CONV_K, cw), const(1, cw), const(1, LANES), const(1, LANES), const(1, SSM_INNER)],
        out_specs=[pl.BlockSpec((1, TILE, SSM_INNER), lambda i, j: (i, j, 0)),
                   pl.BlockSpec((1, TILE, SSM_INNER), lambda i, j: (i, _bwd_block(j, t), 0))],
        out_shape=[jax.ShapeDtypeStruct((b, n_tok, SSM_INNER), F32)] * 2,
        scratch_shapes=[pltpu.VMEM((TILE + 2 * HALO, cw), F32),
                        pltpu.VMEM((2, SSM_GROUPS, SSM_STATE, SSM_INNER // SSM_GROUPS), F32)],
        compiler_params=_cparams(("arbitrary", "arbitrary")),
        name="ssd",
    )(u, u, u, u, u, u, u, u, conv_w, conv_b, alog_row, dtb_row, dskip_row)


def _rope(x, tab_ref, shift):
    n = x.shape[-1]
    return (x * tab_ref[0] + pltpu.roll(x, n - shift, 1) * tab_ref[1] + pltpu.roll(x, shift, 1) * tab_ref[2])


def _mla_prep_kernel(u_ref, tab_ref, qlg_ref, kvlg_ref, wq_ref, wk_ref, wvt_ref, qng_ref, kng_ref,
                     q_ref, k_ref, vt_ref):
    u = u_ref[0]
    cq = _rms(u[:, :MLA_Q_LORA], qlg_ref[...]).astype(BF16)
    ckv = _rms(u[:, MLA_Q_LORA:MLA_Q_LORA + MLA_KV_LORA], kvlg_ref[...]).astype(BF16)
    kpe = u[:, MLA_Q_LORA + MLA_KV_LORA:]
    q_all = _dot(cq, wq_ref[...])
    k_all = _dot(ckv, wk_ref[...])
    vt_ref[0] = _dot_nt(wvt_ref[...], ckv).astype(BF16)
    scale = MLA_QK ** -0.5
    for h in range(MLA_HEADS):
        hl = slice(h * LANES, (h + 1) * LANES)
        qh = q_all[:, hl]
        qh = qh * lax.rsqrt(jnp.sum(qh * qh, axis=-1, keepdims=True) / MLA_QK + NORM_EPS) * qng_ref[...]
        q_ref[0, :, hl] = (_rope(qh, tab_ref, MLA_ROPE // 4) * scale).astype(BF16)
        kh = k_all[:, hl] + kpe
        kh = kh * lax.rsqrt(jnp.sum(kh * kh, axis=-1, keepdims=True) / MLA_QK + NORM_EPS) * kng_ref[...]
        k_ref[0, :, hl] = _rope(kh, tab_ref, MLA_ROPE // 4).astype(BF16)


def _mla_prep(u, tab, qlg, kvlg, wq, wk, wvt, qng, kng):
    b, n_tok, _ = u.shape
    t = n_tok // TILE
    w = MLA_HEADS * LANES
    cw = MLA_Q_LORA + MLA_KV_LORA + LANES
    const = lambda r, c: pl.BlockSpec((r, c), lambda i, j: (0, 0))
    return pl.pallas_call(
        _mla_prep_kernel,
        grid=(b, t),
        in_specs=[pl.BlockSpec((1, TILE, cw), lambda i, j: (i, j, C_MLA // cw)),
                  pl.BlockSpec((3, TILE, LANES), lambda i, j: (0, j, 0)),
                  const(1, MLA_Q_LORA), const(1, MLA_KV_LORA), const(MLA_Q_LORA, w), const(MLA_KV_LORA, w),
                  const(MLA_HEADS * MLA_V, MLA_KV_LORA), const(1, LANES), const(1, LANES)],
        out_specs=[pl.BlockSpec((1, TILE, w), lambda i, j: (i, j, 0)),
                   pl.BlockSpec((1, TILE, w), lambda i, j: (i, j, 0)),
                   pl.BlockSpec((1, MLA_HEADS * MLA_V, TILE), lambda i, j: (i, 0, j))],
        out_shape=[jax.ShapeDtypeStruct((b, n_tok, w), BF16),
                   jax.ShapeDtypeStruct((b, n_tok, w), BF16),
                   jax.ShapeDtypeStruct((b, MLA_HEADS * MLA_V, n_tok), BF16)],
        compiler_params=_cparams(("arbitrary", "arbitrary")),
        name="mla_prep",
    )(u, tab, qlg, kvlg, wq, wk, wvt, qng, kng)


def _diff_prep_kernel(u_ref, tab_ref, qng_ref, kng_ref, q_ref, k_ref, vt_ref):
    u = u_ref[0]
    lane = _iota((TILE, LANES), 1)
    lo = lane < DIFF_HD

    def norm_rope(x, g_ref):
        x2 = x * x
        s_lo = jnp.sum(jnp.where(lo, x2, 0.0), axis=-1, keepdims=True)
        s_hi = jnp.sum(jnp.where(lo, 0.0, x2), axis=-1, keepdims=True)
        ms = jnp.where(lo, s_lo, s_hi) / DIFF_HD
        return _rope(x * lax.rsqrt(ms + NORM_EPS) * g_ref[...], tab_ref, DIFF_HD // 4)

    for h in range(DIFF_HEADS):
        hl = slice(h * LANES, (h + 1) * LANES)
        q_ref[0, :, hl] = (norm_rope(u[:, hl], qng_ref) * (DIFF_HD ** -0.5)).astype(BF16)
        k_ref[0, :, hl] = norm_rope(u[:, DIFF_QK + h * LANES:DIFF_QK + (h + 1) * LANES], kng_ref).astype(BF16)
    vt_ref[0] = u[:, 2 * DIFF_QK:].T.astype(BF16)


def _diff_prep(u, tab, qng, kng):
    b, n_tok, _ = u.shape
    t = n_tok // TILE
    const = lambda r, c: pl.BlockSpec((r, c), lambda i, j: (0, 0))
    return pl.pallas_call(
        _diff_prep_kernel,
        grid=(b, t),
        in_specs=[pl.BlockSpec((1, TILE, DIFF_COLS), lambda i, j: (i, j, C_DIFF // DIFF_COLS)),
                  pl.BlockSpec((3, TILE, LANES), lambda i, j: (0, j, 0)),
                  const(1, LANES), const(1, LANES)],
        out_specs=[pl.BlockSpec((1, TILE, DIFF_QK), lambda i, j: (i, j, 0)),
                   pl.BlockSpec((1, TILE, DIFF_QK), lambda i, j: (i, j, 0)),
                   pl.BlockSpec((1, DIFF_VW, TILE), lambda i, j: (i, 0, j))],
        out_shape=[jax.ShapeDtypeStruct((b, n_tok, DIFF_QK), BF16),
                   jax.ShapeDtypeStruct((b, n_tok, DIFF_QK), BF16),
                   jax.ShapeDtypeStruct((b, DIFF_VW, n_tok), BF16)],
        compiler_params=_cparams(("arbitrary", "arbitrary")),
        name="diff_prep",
    )(u, tab, qng, kng)


def _softmax_pv(k, q, vt):
    s = _dot_nt(k, q)
    p = jnp.exp(s - jnp.max(s, axis=0, keepdims=True))
    return _dot(vt, p.astype(BF16)), jnp.sum(p, axis=0, keepdims=True)


def _attn_kernel(q_ref, k_ref, vt_ref, *rest, n_heads, dv, diff, lam_init, first_tile):
    if diff:
        lamp_ref, subg_ref, o_ref, ot_ref = rest
        lp = lamp_ref[...]
        lam = (jnp.exp(jnp.sum(lp[0:1] * lp[1:2], axis=-1, keepdims=True))
               - jnp.exp(jnp.sum(lp[2:3] * lp[3:4], axis=-1, keepdims=True)) + lam_init)
        lane = _iota((TILE, LANES), 1)
    else:
        o_ref, ot_ref = rest
    tile = pl.program_id(1) + first_tile

    def run(n_keys):
        def head(h, carry):
            hl = pl.ds(pl.multiple_of(h * LANES, LANES), LANES)
            q = q_ref[0, :, hl]
            k = k_ref[0, 0:n_keys, hl]
            vt = vt_ref[0, pl.ds(pl.multiple_of(h * dv, dv), dv), 0:n_keys]
            if diff:
                zero = jnp.zeros_like(q)
                o0, l0 = _softmax_pv(k, jnp.where(lane < DIFF_HD, q, zero), vt)
                o1, l1 = _softmax_pv(k, jnp.where(lane < DIFF_HD, zero, q), vt)
                o = o0 / l0 - lam * (o1 / l1)
                ms = jnp.mean(o * o, axis=0, keepdims=True)
                o = o * lax.rsqrt(ms + NORM_EPS) * subg_ref[...] * (1.0 - lam_init)
            else:
                o, l = _softmax_pv(k, q, vt)
                o = o / l
            ot_ref[pl.ds(pl.multiple_of(h * dv, dv), dv), :] = o
            return carry

        lax.fori_loop(0, n_heads, head, 0)
        o_ref[0] = ot_ref[...].T

    if first_tile == 0:
        pl.when(tile == 0)(lambda: run(TILE))
        pl.when(tile > 0)(lambda: run(k_ref.shape[1]))
    else:
        run(k_ref.shape[1])


def _attention(q, k, vt, *, n_heads, dv, need_ctx, diff=False, lam_p=None, sub_g=None, lam_init=0.0):
    b, n_tok, w = q.shape
    t = n_tok // TILE
    first = 0 if need_ctx else 1
    ow = n_heads * dv
    in_specs = [pl.BlockSpec((1, TILE, w), lambda i, j: (i, j + first, 0)),
                pl.BlockSpec((1, n_tok, w), lambda i, j: (i, 0, 0)),
                pl.BlockSpec((1, ow, n_tok), lambda i, j: (i, 0, 0))]
    args = [q, k, vt]
    if diff:
        in_specs += [pl.BlockSpec(lam_p.shape, lambda i, j: (0, 0)), pl.BlockSpec(sub_g.shape, lambda i, j: (0, 0))]
        args += [lam_p, sub_g]
    return pl.pallas_call(
        functools.partial(_attn_kernel, n_heads=n_heads, dv=dv, diff=diff, lam_init=lam_init, first_tile=first),
        grid=(b, t - first),
        in_specs=in_specs,
        out_specs=pl.BlockSpec((1, TILE, ow), lambda i, j: (i, j + first, 0)),
        out_shape=jax.ShapeDtypeStruct((b, n_tok, ow), F32),
        scratch_shapes=[pltpu.VMEM((ow, TILE), F32)],
        compiler_params=_cparams(("arbitrary", "arbitrary")),
        name="diff_attn" if diff else "mla_attn",
    )(*args)


def _merge_kernel(x_ref, mod_ref, gof_ref, gob_ref, gz_ref, mla_ref, dif_ref, syf_ref, syb_ref, sz_ref,
                  gates01_ref, gates23_ref, gng_ref, sng_ref, wb_ref, wo_ref, o_ref):
    o = gof_ref[0] + gob_ref[0]
    z = gz_ref[0]
    ya = []
    for h in range(GDN_HEADS):
        hl = slice(h * GDN_DV, (h + 1) * GDN_DV)
        ya.append(_rms(o[:, hl], gng_ref[...]) * _silu(z[:, hl]))
    ya = jnp.concatenate(ya, axis=-1)
    y = (syf_ref[0] + syb_ref[0]) * _silu(sz_ref[0])
    gsz = SSM_INNER // SSM_GROUPS
    yd = jnp.concatenate([_rms(y[:, g * gsz:(g + 1) * gsz], sng_ref[:, g * gsz:(g + 1) * gsz])
                          for g in range(SSM_GROUPS)], axis=-1)
    ys = (ya, mla_ref[0], dif_ref[0], yd)
    m = None
    for i in range(N_BRANCH):
        gates_ref = gates01_ref if i < 2 else gates23_ref
        gate = _sigmoid(gates_ref[0, :, (i % 2) * D_MODEL:(i % 2 + 1) * D_MODEL])
        term = gate * _dot(ys[i].astype(BF16), wb_ref[i])
        m = term if m is None else m + term
    o_ref[0] = x_ref[0] + mod_ref[2:3, :] * _dot(m.astype(BF16), wo_ref[...])


def _merge(xs, mods, gdn_o, u, y_mla, y_diff, ssd_y, gng, sng, wb, wo, *, need_ctx):
    b, n_tok, d = xs.shape
    t = n_tok // TILE
    first = 0 if need_ctx else 1
    row = lambda w, cb=0: pl.BlockSpec((1, TILE, w), lambda i, j: (i, j + first, cb))
    const = lambda shape: pl.BlockSpec(shape, lambda i, j: (0,) * len(shape))
    return pl.pallas_call(
        _merge_kernel,
        grid=(b, t - first),
        in_specs=[row(d),
                  pl.BlockSpec((None, None, 6, d), lambda i, j: (i, jnp.minimum(j + first, 1), 0, 0)),
                  row(BRANCH_W), row(BRANCH_W), row(BRANCH_W, (C_GDN + 3 * BRANCH_W) // BRANCH_W),
                  row(BRANCH_W), row(BRANCH_W), row(BRANCH_W), row(BRANCH_W), row(BRANCH_W, C_SSMZ // BRANCH_W),
                  row(GATE_COLS // 2, C_GATE // (GATE_COLS // 2)), row(GATE_COLS // 2, C_GATE // (GATE_COLS // 2) + 1),
                  const((1, GDN_DV)), const((1, SSM_INNER)), const((N_BRANCH, BRANCH_W, d)), const((d, d))],
        out_specs=row(d),
        out_shape=jax.ShapeDtypeStruct((b, n_tok, d), F32),
        input_output_aliases={0: 0},
        compiler_params=_cparams(("arbitrary", "arbitrary")),
        name="merge",
    )(xs, mods, gdn_o[0], gdn_o[1], u, y_mla, y_diff, ssd_y[0], ssd_y[1], u,
      u, u, gng, sng, wb, wo)


def _mlp_kernel(x_ref, mod_ref, g_ref, w1_ref, w2_ref, o_ref):
    x = x_ref[0]
    h = _rms(x, g_ref[...]) * (1.0 + mod_ref[4:5, :]) + mod_ref[3:4, :]
    a = jnp.maximum(_dot(h.astype(BF16), w1_ref[...]), 0.0)
    o_ref[0] = x + mod_ref[5:6, :] * _dot((a * a).astype(BF16), w2_ref[...])


def _mlp(xs, mods, g, w1, w2, *, need_ctx):
    b, n_tok, d = xs.shape
    t = n_tok // TILE
    first = 0 if need_ctx else 1
    const = lambda shape: pl.BlockSpec(shape, lambda i, j: (0,) * len(shape))
    n_out = n_tok - first * TILE
    return pl.pallas_call(
        _mlp_kernel,
        grid=(b, t - first),
        in_specs=[pl.BlockSpec((1, TILE, d), lambda i, j: (i, j + first, 0)),
                  pl.BlockSpec((None, None, 6, d), lambda i, j: (i, jnp.minimum(j + first, 1), 0, 0)),
                  const((1, d)), const((d, D_FF)), const((D_FF, d))],
        out_specs=pl.BlockSpec((1, TILE, d), lambda i, j: (i, j, 0)),
        out_shape=jax.ShapeDtypeStruct((b, n_out, d), F32),
        compiler_params=_cparams(("arbitrary", "arbitrary")),
        name="mlp",
    )(xs, mods, g, w1, w2)


def _repack_w_in(w):
    d = w.shape[0]
    z = lambda n: jnp.zeros((d, n), w.dtype)
    o_mla = GDN_COLS
    o_diff = o_mla + MLA_COLS
    o_ssm = o_diff + DIFF_COLS
    o_gate = MIX_COLS
    gdn_ab = w[:, 2 * GDN_QK + 2 * GDN_VW:GDN_COLS]
    ssm_dt = w[:, o_ssm + 2 * SSM_INNER + 2 * SSM_BC:o_ssm + SSM_COLS]
    kpe = w[:, o_mla + MLA_Q_LORA + MLA_KV_LORA:o_mla + MLA_COLS]
    parts = [
        w[:, 0:2 * GDN_QK + 2 * GDN_VW],
        w[:, o_ssm + SSM_INNER:o_ssm + 2 * SSM_INNER + 2 * SSM_BC],
        w[:, o_diff:o_diff + DIFF_COLS],
        w[:, o_mla:o_mla + MLA_Q_LORA + MLA_KV_LORA], z(MLA_NOPE), kpe, z(LANES - MLA_QK),
        gdn_ab, ssm_dt, z(LANES - 4 * GDN_HEADS - 2 * SSM_HEADS),
        z(C_SSMZ - C_SMALL - LANES),
        w[:, o_ssm:o_ssm + SSM_INNER],
        w[:, o_gate:o_gate + GATE_COLS],
    ]
    out = jnp.concatenate(parts, axis=1)
    assert out.shape[1] == N_IN
    return out.astype(BF16)


def _lane_row(vals, offset):
    row = jnp.zeros((1, LANES), F32)
    return row.at[0, offset:offset + vals.shape[0]].set(vals.astype(F32))


def _rope_tables(n_lat, rot_dim, lane_offset, period):
    rows = n_lat // GRID_W
    row = jnp.repeat(jnp.arange(rows, dtype=F32), GRID_W)
    col = jnp.tile(jnp.arange(GRID_W, dtype=F32), rows)
    quarter = rot_dim // 4
    inv = ROPE_THETA ** (-jnp.arange(quarter, dtype=F32) / quarter)
    ar = row[:, None] * inv
    ac = col[:, None] * inv
    ang = jnp.concatenate([ar, ar, ac, ac], axis=-1)
    cos, sin = jnp.cos(ang), jnp.sin(ang)
    first = (np.arange(rot_dim) % (2 * quarter)) < quarter
    sin_a = jnp.where(first, -sin, 0.0)
    sin_b = jnp.where(first, 0.0, sin)
    reps = LANES // period
    def place(t, fill):
        blk = jnp.full((n_lat, period), fill, F32).at[:, lane_offset:lane_offset + rot_dim].set(t)
        blk = jnp.tile(blk, (1, reps))
        ctx = jnp.full((CTX_LEN, LANES), fill, F32)
        return jnp.concatenate([ctx, blk], axis=0)
    return jnp.stack([place(cos, 1.0), place(sin_a, 0.0), place(sin_b, 0.0)])


def kernel(x, c, ctx, c_ctx, ada_w, ada_b, norm1_g, norm2_g, w_in, gdn_conv, gdn_a_log, gdn_dt_bias, gdn_norm_g,
           mla_q_lora_g, mla_kv_lora_g, mla_w_uq, mla_w_ukv, mla_qn_g, mla_kn_g, diff_qn_g, diff_kn_g,
           diff_lambda, diff_sub_g, ssm_conv, ssm_conv_b, ssm_a_log, ssm_dt_bias, ssm_d, ssm_norm_g, w_branch,
           w_out, mlp_w1, mlp_w2):
    b, n_lat, d = x.shape
    depth = w_in.shape[0]
    assert ctx.shape[1] == CTX_LEN == TILE and n_lat % TILE == 0 and d == D_MODEL
    xs = jnp.concatenate([ctx, x], axis=1)
    tab_mla = _rope_tables(n_lat, MLA_ROPE, MLA_NOPE, LANES)
    tab_diff = _rope_tables(n_lat, DIFF_HD, 0, DIFF_HD)
    rows = b + 8
    cc = jnp.zeros((rows, d), F32).at[:b].set(c).at[b].set(c_ctx)
    row2 = lambda v: v.reshape(1, -1).astype(F32)

    for l in range(depth):
        need_ctx = l < depth - 1
        lam_init = LAMBDA_BASE - LAMBDA_AMP * math.exp(-LAMBDA_RATE * l)
        mod = _ada(cc, ada_w[l], ada_b[l].reshape(1, -1))
        mod_lat = mod[:b].reshape(b, 6, d)
        mod_ctx = jnp.broadcast_to(mod[b].reshape(1, 6, d), (b, 6, d))
        mods = jnp.stack([mod_ctx, mod_lat], axis=1)

        u = _inproj(xs, mods, row2(norm1_g[l]), _repack_w_in(w_in[l]))

        gdn_o = _gdn_scan(_gdn_prep(u, gdn_conv[l].astype(F32),
                                    _lane_row(gdn_a_log[l].reshape(-1), SM_A),
                                    _lane_row(gdn_dt_bias[l].reshape(-1), SM_A)))

        wq = mla_w_uq[l].reshape(MLA_Q_LORA, MLA_HEADS, MLA_QK)
        wq = jnp.pad(wq, ((0, 0), (0, 0), (0, LANES - MLA_QK))).reshape(MLA_Q_LORA, MLA_HEADS * LANES)
        wkv = mla_w_ukv[l].reshape(MLA_KV_LORA, MLA_HEADS, MLA_NOPE + MLA_V)
        wk = jnp.pad(wkv[:, :, :MLA_NOPE], ((0, 0), (0, 0), (0, LANES - MLA_NOPE))).reshape(MLA_KV_LORA, MLA_HEADS * LANES)
        wvt = wkv[:, :, MLA_NOPE:].reshape(MLA_KV_LORA, MLA_HEADS * MLA_V).T
        pad_g = lambda g: jnp.pad(g.astype(F32), (0, LANES - MLA_QK)).reshape(1, LANES)
        q_m, k_m, vt_m = _mla_prep(u, tab_mla, row2(mla_q_lora_g[l]), row2(mla_kv_lora_g[l]),
                                   wq.astype(BF16), wk.astype(BF16), wvt.astype(BF16),
                                   pad_g(mla_qn_g[l]), pad_g(mla_kn_g[l]))
        y_mla = _attention(q_m, k_m, vt_m, n_heads=MLA_HEADS, dv=MLA_V, need_ctx=need_ctx)

        rep_g = lambda g: jnp.tile(g.astype(F32), LANES // DIFF_HD).reshape(1, LANES)
        q_d, k_d, vt_d = _diff_prep(u, tab_diff, rep_g(diff_qn_g[l]), rep_g(diff_kn_g[l]))
        y_diff = _attention(q_d, k_d, vt_d, n_heads=DIFF_HEADS, dv=2 * DIFF_HD, need_ctx=need_ctx, diff=True,
                            lam_p=diff_lambda[l].astype(F32), sub_g=diff_sub_g[l].astype(F32).reshape(-1, 1),
                            lam_init=lam_init)

        ssd_y = _ssd(u, ssm_conv[l].astype(F32), row2(ssm_conv_b[l]),
                     _lane_row(ssm_a_log[l].reshape(-1), SM_DT), _lane_row(ssm_dt_bias[l].reshape(-1), SM_DT),
                     jnp.repeat(ssm_d[l].astype(F32), SSM_HEAD_DIM).reshape(1, -1))

        xs = _merge(xs, mods, gdn_o, u, y_mla, y_diff, ssd_y, row2(gdn_norm_g[l]), row2(ssm_norm_g[l]),
                    w_branch[l].astype(BF16), w_out[l].astype(BF16), need_ctx=need_ctx)
        xs = _mlp(xs, mods, row2(norm2_g[l]), mlp_w1[l].astype(BF16), mlp_w2[l].astype(BF16), need_ctx=need_ctx)
    return xs
```

```python
import functools
import math

import jax
import jax.numpy as jnp
import numpy as np
from jax import lax
from jax.experimental import pallas as pl
from jax.experimental.pallas import tpu as pltpu

F32 = jnp.float32
BF16 = jnp.bfloat16

D_MODEL = 1024
CTX_LEN = 256
GRID_W = 64
ROPE_THETA = 10000.0
NORM_EPS = 1e-6
CONV_K = 5
N_BRANCH = 4
D_FF = 4 * D_MODEL
GDN_HEADS = 4
GDN_DK = 128
GDN_DV = 128
MLA_HEADS = 8
MLA_NOPE = 64
MLA_ROPE = 32
MLA_V = 64
MLA_QK = MLA_NOPE + MLA_ROPE
MLA_Q_LORA = 384
MLA_KV_LORA = 256
DIFF_HEADS = 4
DIFF_HD = 64
LAMBDA_BASE = 0.8
LAMBDA_AMP = 0.6
LAMBDA_RATE = 0.3
SSM_HEADS = 8
SSM_HEAD_DIM = 64
SSM_GROUPS = 2
SSM_STATE = 128
GDN_QK = GDN_HEADS * GDN_DK
GDN_VW = GDN_HEADS * GDN_DV
DIFF_QK = DIFF_HEADS * 2 * DIFF_HD
DIFF_VW = DIFF_HEADS * 2 * DIFF_HD
SSM_INNER = SSM_HEADS * SSM_HEAD_DIM
SSM_BC = SSM_GROUPS * SSM_STATE
BRANCH_W = 512
GDN_COLS = 2 * GDN_QK + 2 * GDN_VW + 4 * GDN_HEADS
MLA_COLS = MLA_Q_LORA + MLA_KV_LORA + MLA_ROPE
DIFF_COLS = 2 * DIFF_QK + DIFF_VW
SSM_COLS = 2 * SSM_INNER + 2 * SSM_BC + 2 * SSM_HEADS
MIX_COLS = GDN_COLS + MLA_COLS + DIFF_COLS + SSM_COLS
GATE_COLS = N_BRANCH * D_MODEL

LANES = 128
TILE = 256
CHUNK = 64
CPT = TILE // CHUNK
HALO = 8
VMEM_LIMIT = 56 * 1024 * 1024
V_ONES = 16
KEY_BLOCK = 256
LOG2E = 1.4426950408889634

C_GDN = 0
C_XBC = 2048
C_DIFF = 3072
C_MLA = 4608
C_SMALL = 5376
C_SSMZ = 5632
C_GATE = 6144
N_IN = 10240
IN_TN = 2560

SM_A = 0
SM_B = 8
SM_DT = 16


def _cparams(sem):
    return pltpu.CompilerParams(dimension_semantics=sem, vmem_limit_bytes=VMEM_LIMIT)


def _dot(a, b):
    return jnp.dot(a, b, preferred_element_type=F32)


def _dot_nt(a, b):
    return lax.dot_general(a, b, (((1,), (1,)), ((), ())), preferred_element_type=F32)


def _dot_tn(a, b):
    return lax.dot_general(a, b, (((0,), (0,)), ((), ())), preferred_element_type=F32)


def _split3(x):
    x1 = x.astype(BF16)
    r1 = x - x1.astype(F32)
    x2 = r1.astype(BF16)
    r2 = r1 - x2.astype(F32)
    return x1, x2, r2.astype(BF16)


def _dot_exact_lhs(a_bf, x):
    x1, x2, x3 = _split3(x)
    return _dot(a_bf, x1) + _dot(a_bf, x2) + _dot(a_bf, x3)


def _dot_exact_rhs(x, b_bf):
    x1, x2, x3 = _split3(x)
    return _dot(x1, b_bf) + _dot(x2, b_bf) + _dot(x3, b_bf)


def _sigmoid(x):
    return 1.0 / (1.0 + jnp.exp(-x))


def _silu(x):
    return x * _sigmoid(x)


def _softplus(x):
    return jnp.maximum(x, 0.0) + jnp.log1p(jnp.exp(-jnp.abs(x)))


def _iota(shape, dim):
    return lax.broadcasted_iota(jnp.int32, shape, dim)


def _chunk_tri(n, upper):
    r = _iota((n, n), 0)
    c = _iota((n, n), 1)
    same = (r // CHUNK) == (c // CHUNK)
    tri = (r <= c) if upper else (r >= c)
    return jnp.where(same & tri, 1.0, 0.0).astype(BF16)


def _expander(n_rows, first_row, n_groups, width):
    r = _iota((n_rows, n_groups * width), 0)
    c = _iota((n_rows, n_groups * width), 1)
    return jnp.where(r == first_row + c // width, 1.0, 0.0).astype(BF16)


def _ada_kernel(c_ref, w_ref, b_ref, o_ref):
    a = _silu(c_ref[...])
    a1, a2, a3 = _split3(a)
    w1, w2, w3 = _split3(w_ref[...])
    acc = _dot(a1, w1) + (_dot(a1, w2) + _dot(a2, w1)) + (_dot(a1, w3) + _dot(a2, w2) + _dot(a3, w1))
    o_ref[...] = acc + b_ref[...]


def _ada(cc, w, b):
    rows, d = cc.shape
    n = w.shape[1]
    tn = 1536
    return pl.pallas_call(
        _ada_kernel,
        grid=(n // tn,),
        in_specs=[pl.BlockSpec((rows, d), lambda j: (0, 0)),
                  pl.BlockSpec((d, tn), lambda j: (0, j)),
                  pl.BlockSpec((1, tn), lambda j: (0, j))],
        out_specs=pl.BlockSpec((rows, tn), lambda j: (0, j)),
        out_shape=jax.ShapeDtypeStruct((rows, n), F32),
        compiler_params=_cparams(("arbitrary",)),
        name="ada",
    )(cc, w, b)


def _rms(x, g):
    ms = jnp.mean(x * x, axis=-1, keepdims=True)
    return x * lax.rsqrt(ms + NORM_EPS) * g


def _inproj_kernel(x_ref, mod_ref, g_ref, w_ref, o_ref):
    h = _rms(x_ref[0], g_ref[...])
    h = h * (1.0 + mod_ref[1:2, :]) + mod_ref[0:1, :]
    o_ref[0] = _dot(h.astype(BF16), w_ref[...])


def _inproj(xs, mods, g, w):
    b, n_tok, d = xs.shape
    t = n_tok // TILE
    nt = N_IN // IN_TN
    return pl.pallas_call(
        _inproj_kernel,
        grid=(nt, b, t),
        in_specs=[pl.BlockSpec((1, TILE, d), lambda n, i, j: (i, j, 0)),
                  pl.BlockSpec((None, None, 6, d), lambda n, i, j: (i, jnp.minimum(j, 1), 0, 0)),
                  pl.BlockSpec((1, d), lambda n, i, j: (0, 0)),
                  pl.BlockSpec((d, IN_TN), lambda n, i, j: (0, n))],
        out_specs=pl.BlockSpec((1, TILE, IN_TN), lambda n, i, j: (i, j, n)),
        out_shape=jax.ShapeDtypeStruct((b, n_tok, N_IN), F32),
        compiler_params=_cparams(("arbitrary", "arbitrary", "arbitrary")),
        name="inproj",
    )(xs, mods, g, w)


def _halo_specs(width, col_block, n_tiles, tile_of):
    rpt = TILE // HALO
    last = n_tiles * rpt - 1
    main = pl.BlockSpec((1, TILE, width), lambda *ids: (ids[0], tile_of(*ids), col_block))
    prev = pl.BlockSpec((1, HALO, width),
                        lambda *ids: (ids[0], jnp.maximum(tile_of(*ids) * rpt - 1, 0), col_block))
    nxt = pl.BlockSpec((1, HALO, width),
                       lambda *ids: (ids[0], jnp.minimum(tile_of(*ids) * rpt + rpt, last), col_block))
    return main, prev, nxt


def _conv_tile(main_ref, prev_ref, next_ref, w_ref, ext_ref, tile, n_tiles):
    prev_ok = (tile >= 2).astype(F32)
    next_ok = jnp.logical_and(tile >= 1, tile < n_tiles - 1).astype(F32)
    ext_ref[0:HALO, :] = prev_ref[0] * prev_ok
    ext_ref[HALO:HALO + TILE, :] = main_ref[0]
    ext_ref[HALO + TILE:2 * HALO + TILE, :] = next_ref[0] * next_ok
    half = CONV_K // 2
    acc = None
    for k in range(CONV_K):
        term = w_ref[k:k + 1, :] * ext_ref[HALO - half + k:HALO - half + k + TILE, :]
        acc = term if acc is None else acc + term
    return acc


def _block_inverse_many(mats):
    n = mats[0].shape[0]
    ri = _iota((n, n), 0)
    ci = _iota((n, n), 1)
    eye = jnp.where(ri == ci, 1.0, 0.0).astype(F32)
    ts = [eye - jnp.where((ri // 2) == (ci // 2), a, 0.0) for a in mats]
    abs_ = [a.astype(BF16) for a in mats]
    zero = jnp.zeros((n, n), BF16)
    size = 2
    while size < CHUNK:
        off = ((ri // (2 * size)) == (ci // (2 * size))) & ((ri // size) != (ci // size))
        tbs = [t.astype(BF16) for t in ts]
        xs = [_dot(jnp.where(off, ab, zero), tb).astype(BF16) for ab, tb in zip(abs_, tbs)]
        ts = [t - _dot(tb, x) for t, tb, x in zip(ts, tbs, xs)]
        size *= 2
    return ts


def _gdn_prep_kernel(main_ref, prev_ref, next_ref, sm_ref, cw_ref, alog_ref, dtb_ref,
                     u0_ref, wq0_ref, kd0_ref, at0_ref, gl0_ref,
                     u1_ref, wq1_ref, kd1_ref, at1_ref, gl1_ref,
                     ext_ref, *, n_tiles):
    tile = pl.program_id(1)
    qkv = _silu(_conv_tile(main_ref, prev_ref, next_ref, cw_ref, ext_ref, tile, n_tiles))
    sm = sm_ref[0]
    g_all = -jnp.exp(alog_ref[...]) * _softplus(sm + dtb_ref[...])
    beta_all = _sigmoid(sm)
    nh = GDN_HEADS
    exp_g = _expander(LANES, SM_A, 2 * nh, LANES)
    exp_b = _expander(LANES, SM_B, 2 * nh, LANES)
    beta_x = _dot_exact_rhs(beta_all, exp_b)
    outs = ((u0_ref, wq0_ref, kd0_ref, at0_ref, gl0_ref), (u1_ref, wq1_ref, kd1_ref, at1_ref, gl1_ref))
    ri = _iota((TILE, TILE), 0)
    ci = _iota((TILE, TILE), 1)
    same = (ri // CHUNK) == (ci // CHUNK)
    cums, masks = [], []
    for d in range(2):
        cum = _dot_exact_lhs(_chunk_tri(TILE, upper=(d == 1)), g_all)
        cums.append((_dot_exact_rhs(cum, exp_g), cum.T))
        masks.append((same & ((ri >= ci) if d == 0 else (ri <= ci)), same & ((ri > ci) if d == 0 else (ri < ci))))

    chains, a_mats = [], []
    for h in range(nh):
        q = qkv[:, h * GDN_DK:(h + 1) * GDN_DK]
        k = qkv[:, GDN_QK + h * GDN_DK:GDN_QK + (h + 1) * GDN_DK]
        v = qkv[:, 2 * GDN_QK + h * GDN_DV:2 * GDN_QK + (h + 1) * GDN_DV]
        q = q * lax.rsqrt(jnp.sum(q * q, axis=-1, keepdims=True) + 1e-6) * (GDN_DK ** -0.5)
        k = k * lax.rsqrt(jnp.sum(k * k, axis=-1, keepdims=True) + 1e-6)
        kb = k.astype(BF16)
        kq = _dot_nt(jnp.concatenate([kb, q.astype(BF16)], axis=0), kb)
        for d in range(2):
            col = d * nh + h
            lanes = slice(col * LANES, (col + 1) * LANES)
            cum_x, cum_t = cums[d]
            incl, strict = masks[d]
            g_col = cum_x[:, lanes]
            b_col = beta_x[:, lanes]
            diff = jnp.concatenate([g_col, g_col], axis=1) - cum_t[SM_A + col:SM_A + col + 1, :]
            decay = jnp.where(incl, jnp.exp(jnp.where(incl, diff, 0.0)), 0.0)
            a_mats.append(jnp.where(strict, kq[:TILE] * jnp.concatenate([b_col, b_col], axis=1) * decay, 0.0))
            e_g = jnp.exp(g_col)
            rhs = jnp.concatenate([v * b_col, k * b_col * e_g], axis=1).astype(BF16)
            chains.append((h, d, q, k, g_col, e_g, rhs, kq[TILE:] * decay))

    t_mats = _block_inverse_many(a_mats)
    for (h, d, q, k, g_col, e_g, rhs, attn), t_mat in zip(chains, t_mats):
        uw = _dot(t_mat.astype(BF16), rhs)
        qg = (q * e_g).astype(BF16)
        u_ref, wq_ref, kd_ref, at_ref, gl_ref = outs[d]
        hl = slice(h * LANES, (h + 1) * LANES)
        for c in range(CPT):
            rows = slice(c * CHUNK, (c + 1) * CHUNK)
            last = c * CHUNK + (CHUNK - 1 if d == 0 else 0)
            g_last = g_col[last:last + 1, :]
            slot = c if d == 0 else CPT - 1 - c
            u_ref[0, 0, slot, :, hl] = uw[rows, :GDN_DV]
            wq_ref[0, 0, slot, 0:CHUNK, hl] = uw[rows, GDN_DV:].astype(BF16)
            wq_ref[0, 0, slot, CHUNK:2 * CHUNK, hl] = qg[rows]
            kd_ref[0, 0, slot, :, hl] = (k[rows] * jnp.exp(g_last - g_col[rows])).astype(BF16)
            at_ref[0, 0, slot, :, h * CHUNK:(h + 1) * CHUNK] = attn[rows, rows].astype(BF16)
            gl_ref[0, 0, slot, :, hl] = jnp.exp(g_last)


def _bwd_block(j, n_tiles):
    return jnp.where(j == 0, 0, n_tiles - j)


def _gdn_prep(u, conv_w, alog_row, dtb_row):
    b, n_tok, _ = u.shape
    t = n_tok // TILE
    w = GDN_VW
    cw = 2 * GDN_QK + GDN_VW
    main, prev, nxt = _halo_specs(cw, 0, t, lambda i, j: j)
    fwd = lambda i, j: (i, j, 0, 0, 0)
    bwd = lambda i, j: (i, _bwd_block(j, t), 0, 0, 0)

    def outs(imap):
        return [pl.BlockSpec((1, 1, CPT, CHUNK, w), imap),
                pl.BlockSpec((1, 1, CPT, 2 * CHUNK, w), imap),
                pl.BlockSpec((1, 1, CPT, CHUNK, w), imap),
                pl.BlockSpec((1, 1, CPT, CHUNK, GDN_HEADS * CHUNK), imap),
                pl.BlockSpec((1, 1, CPT, 1, w), imap)]

    shapes = [jax.ShapeDtypeStruct((b, t, CPT, CHUNK, w), F32),
              jax.ShapeDtypeStruct((b, t, CPT, 2 * CHUNK, w), BF16),
              jax.ShapeDtypeStruct((b, t, CPT, CHUNK, w), BF16),
              jax.ShapeDtypeStruct((b, t, CPT, CHUNK, GDN_HEADS * CHUNK), BF16),
              jax.ShapeDtypeStruct((b, t, CPT, 1, w), F32)]
    return pl.pallas_call(
        functools.partial(_gdn_prep_kernel, n_tiles=t),
        grid=(b, t),
        in_specs=[main, prev, nxt,
                  pl.BlockSpec((1, TILE, LANES), lambda i, j: (i, j, C_SMALL // LANES)),
                  pl.BlockSpec((CONV_K, cw), lambda i, j: (0, 0)),
                  pl.BlockSpec((1, LANES), lambda i, j: (0, 0)),
                  pl.BlockSpec((1, LANES), lambda i, j: (0, 0))],
        out_specs=outs(fwd) + outs(bwd),
        out_shape=shapes + shapes,
        scratch_shapes=[pltpu.VMEM((TILE + 2 * HALO, cw), F32)],
        compiler_params=_cparams(("arbitrary", "arbitrary")),
        name="gdn_prep",
    )(u, u, u, u, conv_w, alog_row, dtb_row)


def _gdn_scan_kernel(u0_ref, wq0_ref, kd0_ref, at0_ref, gl0_ref,
                     u1_ref, wq1_ref, kd1_ref, at1_ref, gl1_ref,
                     of_ref, ob_ref, s_ref):
    @pl.when(pl.program_id(1) == 0)
    def _():
        s_ref[...] = jnp.zeros_like(s_ref)

    ins = ((u0_ref, wq0_ref, kd0_ref, at0_ref, gl0_ref), (u1_ref, wq1_ref, kd1_ref, at1_ref, gl1_ref))
    for slot in range(CPT):
        for d in range(2):
            u_ref, wq_ref, kd_ref, at_ref, gl_ref = ins[d]
            o_ref = of_ref if d == 0 else ob_ref
            c = slot if d == 0 else CPT - 1 - slot
            for h in range(GDN_HEADS):
                hl = slice(h * LANES, (h + 1) * LANES)
                s = s_ref[d, h]
                r = _dot(wq_ref[0, 0, slot, :, hl], s.astype(BF16))
                v_new = u_ref[0, 0, slot, :, hl] - r[0:CHUNK]
                vb = v_new.astype(BF16)
                o = r[CHUNK:2 * CHUNK] + _dot(at_ref[0, 0, slot, :, h * CHUNK:(h + 1) * CHUNK], vb)
                s_ref[d, h] = s * gl_ref[0, 0, slot, :, hl] + _dot_tn(kd_ref[0, 0, slot, :, hl], vb)
                o_ref[0, c * CHUNK:(c + 1) * CHUNK, hl] = o


def _gdn_scan(prep):
    b, t = prep[0].shape[:2]
    w = GDN_VW
    imap = lambda i, j: (i, j, 0, 0, 0)
    specs = [pl.BlockSpec((1, 1, CPT, CHUNK, w), imap),
             pl.BlockSpec((1, 1, CPT, 2 * CHUNK, w), imap),
             pl.BlockSpec((1, 1, CPT, CHUNK, w), imap),
             pl.BlockSpec((1, 1, CPT, CHUNK, GDN_HEADS * CHUNK), imap),
             pl.BlockSpec((1, 1, CPT, 1, w), imap)]
    return pl.pallas_call(
        _gdn_scan_kernel,
        grid=(b, t),
        in_specs=specs + specs,
        out_specs=[pl.BlockSpec((1, TILE, w), lambda i, j: (i, j, 0)),
                   pl.BlockSpec((1, TILE, w), lambda i, j: (i, _bwd_block(j, t), 0))],
        out_shape=[jax.ShapeDtypeStruct((b, t * TILE, w), F32)] * 2,
        scratch_shapes=[pltpu.VMEM((2, GDN_HEADS, GDN_DK, GDN_DV), F32)],
        compiler_params=_cparams(("arbitrary", "arbitrary")),
        name="gdn_scan",
    )(*prep)


def _ssd_kernel(mf_ref, pf_ref, nf_ref, smf_ref, mb_ref, pb_ref, nb_ref, smb_ref,
                cw_ref, cb_ref, alog_ref, dtb_ref, dskip_ref,
                yf_ref, yb_ref, ext_ref, st_ref, *, n_tiles):
    j = pl.program_id(1)

    @pl.when(j == 0)
    def _():
        st_ref[...] = jnp.zeros_like(st_ref)

    nh, hd, ng = SSM_HEADS, SSM_HEAD_DIM, SSM_GROUPS
    hpg = nh // ng
    gw = hpg * hd
    ri = _iota((CHUNK, CHUNK), 0)
    ci = _iota((CHUNK, CHUNK), 1)
    dirs = ((mf_ref, pf_ref, nf_ref, smf_ref, yf_ref, j), (mb_ref, pb_ref, nb_ref, smb_ref, yb_ref, _bwd_block(j, n_tiles)))
    for d in range(2):
        m_ref, p_ref, n_ref, sm_ref, y_ref, tile = dirs[d]
        xbc = _silu(_conv_tile(m_ref, p_ref, n_ref, cw_ref, ext_ref, tile, n_tiles) + cb_ref[...])
        xs = xbc[:, :SSM_INNER]
        dt_all = _softplus(sm_ref[0] + dtb_ref[...])
        da_all = dt_all * (-jnp.exp(alog_ref[...]))
        cum = _dot_exact_lhs(_chunk_tri(TILE, upper=(d == 1)), da_all)
        cum_t = cum.T
        expand = _expander(LANES, SM_DT + d * nh, nh, hd)
        dt_x = _dot_exact_rhs(dt_all, expand)
        cum_x = _dot_exact_rhs(cum, expand)
        xdt = xs * dt_x
        incl = (ri >= ci) if d == 0 else (ri <= ci)
        for step in range(CPT):
            c = step if d == 0 else CPT - 1 - step
            rows = slice(c * CHUNK, (c + 1) * CHUNK)
            last = c * CHUNK + (CHUNK - 1 if d == 0 else 0)
            cum_c = cum_x[rows]
            cum_last = cum_x[last:last + 1]
            decay_states = jnp.exp(cum_last - cum_c)
            in_decay = jnp.exp(cum_c)
            chunk_decay = jnp.exp(cum_last)
            for g in range(ng):
                gl = slice(g * gw, (g + 1) * gw)
                bm = xbc[rows, SSM_INNER + g * SSM_STATE:SSM_INNER + (g + 1) * SSM_STATE].astype(BF16)
                cm = xbc[rows, SSM_INNER + SSM_BC + g * SSM_STATE:SSM_INNER + SSM_BC + (g + 1) * SSM_STATE].astype(BF16)
                scores = _dot_nt(cm, bm)
                state = st_ref[d, g]
                y_off = _dot(cm, state.astype(BF16)) * in_decay[:, gl]
                new_states = _dot_tn(bm, (xdt[rows, gl] * decay_states[:, gl]).astype(BF16))
                st_ref[d, g] = state * chunk_decay[:, gl] + new_states
                for hh in range(hpg):
                    h = g * hpg + hh
                    hl = slice(h * hd, (h + 1) * hd)
                    col = SM_DT + d * nh + h
                    seg = cum_c[:, hl] - cum_t[col:col + 1, rows]
                    lmat = jnp.where(incl, jnp.exp(jnp.where(incl, seg, 0.0)), 0.0)
                    y = _dot((scores * lmat).astype(BF16), xdt[rows, hl].astype(BF16)) + y_off[:, hh * hd:(hh + 1) * hd]
                    if d == 0:
                        y = y + dskip_ref[:, hl] * xs[rows, hl]
                    y_ref[0, rows, hl] = y


def _ssd(u, conv_w, conv_b, alog_row, dtb_row, dskip_row):
    b, n_tok, _ = u.shape
    t = n_tok // TILE
    cw = SSM_INNER + 2 * SSM_BC
    cb = C_XBC // cw
    mf, pf, nf = _halo_specs(cw, cb, t, lambda i, j: j)
    mb, pb, nb = _halo_specs(cw, cb, t, lambda i, j: _bwd_block(j, t))
    smf = pl.BlockSpec((1, TILE, LANES), lambda i, j: (i, j, C_SMALL // LANES))
    smb = pl.BlockSpec((1, TILE, LANES), lambda i, j: (i, _bwd_block(j, t), C_SMALL // LANES))
    const = lambda r, c: pl.BlockSpec((r, c), lambda i, j: (0, 0))
    return pl.pallas_call(
        functools.partial(_ssd_kernel, n_tiles=t),
        grid=(b, t),
        in_specs=[mf, pf, nf, smf, mb, pb, nb, smb,
                  const(CONV_K, cw), const(1, cw), const(1, LANES), const(1, LANES), const(1, SSM_INNER)],
        out_specs=[pl.BlockSpec((1, TILE, SSM_INNER), lambda i, j: (i, j, 0)),
                   pl.BlockSpec((1, TILE, SSM_INNER), lambda i, j: (i, _bwd_block(j, t), 0))],
        out_shape=[jax.ShapeDtypeStruct((b, n_tok, SSM_INNER), F32)] * 2,
        scratch_shapes=[pltpu.VMEM((TILE + 2 * HALO, cw), F32),
                        pltpu.VMEM((2, SSM_GROUPS, SSM_STATE, SSM_INNER // SSM_GROUPS), F32)],
        compiler_params=_cparams(("arbitrary", "arbitrary")),
        name="ssd",
    )(u, u, u, u, u, u, u, u, conv_w, conv_b, alog_row, dtb_row, dskip_row)


def _rope(x, tab_ref, shift):
    n = x.shape[-1]
    return (x * tab_ref[0] + pltpu.roll(x, n - shift, 1) * tab_ref[1] + pltpu.roll(x, shift, 1) * tab_ref[2])


def _store_vt(vt_ref, vt, n_heads, dv):
    ones = jnp.ones((V_ONES, vt.shape[1]), BF16)
    for h in range(n_heads):
        base = h * (dv + V_ONES)
        vt_ref[0, base:base + dv, :] = vt[h * dv:(h + 1) * dv].astype(BF16)
        vt_ref[0, base + dv:base + dv + V_ONES, :] = ones


def _mla_prep_kernel(u_ref, tab_ref, qlg_ref, kvlg_ref, wq_ref, wk_ref, wvt_ref, qng_ref, kng_ref,
                     q_ref, k_ref, vt_ref):
    u = u_ref[0]
    cq = _rms(u[:, :MLA_Q_LORA], qlg_ref[...]).astype(BF16)
    ckv = _rms(u[:, MLA_Q_LORA:MLA_Q_LORA + MLA_KV_LORA], kvlg_ref[...]).astype(BF16)
    kpe = u[:, MLA_Q_LORA + MLA_KV_LORA:]
    q_all = _dot(cq, wq_ref[...])
    k_all = _dot(ckv, wk_ref[...])
    _store_vt(vt_ref, _dot_nt(wvt_ref[...], ckv), MLA_HEADS, MLA_V)
    scale = MLA_QK ** -0.5 * LOG2E
    for h in range(MLA_HEADS):
        hl = slice(h * LANES, (h + 1) * LANES)
        qh = q_all[:, hl]
        qh = qh * lax.rsqrt(jnp.sum(qh * qh, axis=-1, keepdims=True) / MLA_QK + NORM_EPS) * qng_ref[...]
        q_ref[0, :, hl] = (_rope(qh, tab_ref, MLA_ROPE // 4) * scale).astype(BF16)
        kh = k_all[:, hl] + kpe
        kh = kh * lax.rsqrt(jnp.sum(kh * kh, axis=-1, keepdims=True) / MLA_QK + NORM_EPS) * kng_ref[...]
        k_ref[0, :, hl] = _rope(kh, tab_ref, MLA_ROPE // 4).astype(BF16)


def _mla_prep(u, tab, qlg, kvlg, wq, wk, wvt, qng, kng):
    b, n_tok, _ = u.shape
    t = n_tok // TILE
    w = MLA_HEADS * LANES
    cw = MLA_Q_LORA + MLA_KV_LORA + LANES
    const = lambda r, c: pl.BlockSpec((r, c), lambda i, j: (0, 0))
    return pl.pallas_call(
        _mla_prep_kernel,
        grid=(b, t),
        in_specs=[pl.BlockSpec((1, TILE, cw), lambda i, j: (i, j, C_MLA // cw)),
                  pl.BlockSpec((3, TILE, LANES), lambda i, j: (0, j, 0)),
                  const(1, MLA_Q_LORA), const(1, MLA_KV_LORA), const(MLA_Q_LORA, w), const(MLA_KV_LORA, w),
                  const(MLA_HEADS * MLA_V, MLA_KV_LORA), const(1, LANES), const(1, LANES)],
        out_specs=[pl.BlockSpec((1, TILE, w), lambda i, j: (i, j, 0)),
                   pl.BlockSpec((1, TILE, w), lambda i, j: (i, j, 0)),
                   pl.BlockSpec((1, MLA_HEADS * (MLA_V + V_ONES), TILE), lambda i, j: (i, 0, j))],
        out_shape=[jax.ShapeDtypeStruct((b, n_tok, w), BF16),
                   jax.ShapeDtypeStruct((b, n_tok, w), BF16),
                   jax.ShapeDtypeStruct((b, MLA_HEADS * (MLA_V + V_ONES), n_tok), BF16)],
        compiler_params=_cparams(("arbitrary", "arbitrary")),
        name="mla_prep",
    )(u, tab, qlg, kvlg, wq, wk, wvt, qng, kng)


def _diff_prep_kernel(u_ref, tab_ref, qng_ref, kng_ref, q_ref, k_ref, vt_ref):
    u = u_ref[0]
    lane = _iota((TILE, LANES), 1)
    lo = lane < DIFF_HD

    def norm_rope(x, g_ref):
        x2 = x * x
        s_lo = jnp.sum(jnp.where(lo, x2, 0.0), axis=-1, keepdims=True)
        s_hi = jnp.sum(jnp.where(lo, 0.0, x2), axis=-1, keepdims=True)
        ms = jnp.where(lo, s_lo, s_hi) / DIFF_HD
        return _rope(x * lax.rsqrt(ms + NORM_EPS) * g_ref[...], tab_ref, DIFF_HD // 4)

    for h in range(DIFF_HEADS):
        hl = slice(h * LANES, (h + 1) * LANES)
        q_ref[0, :, hl] = (norm_rope(u[:, hl], qng_ref) * (DIFF_HD ** -0.5 * LOG2E)).astype(BF16)
        k_ref[0, :, hl] = norm_rope(u[:, DIFF_QK + h * LANES:DIFF_QK + (h + 1) * LANES], kng_ref).astype(BF16)
    _store_vt(vt_ref, u[:, 2 * DIFF_QK:].T, DIFF_HEADS, 2 * DIFF_HD)


def _diff_prep(u, tab, qng, kng):
    b, n_tok, _ = u.shape
    t = n_tok // TILE
    const = lambda r, c: pl.BlockSpec((r, c), lambda i, j: (0, 0))
    return pl.pallas_call(
        _diff_prep_kernel,
        grid=(b, t),
        in_specs=[pl.BlockSpec((1, TILE, DIFF_COLS), lambda i, j: (i, j, C_DIFF // DIFF_COLS)),
                  pl.BlockSpec((3, TILE, LANES), lambda i, j: (0, j, 0)),
                  const(1, LANES), const(1, LANES)],
        out_specs=[pl.BlockSpec((1, TILE, DIFF_QK), lambda i, j: (i, j, 0)),
                   pl.BlockSpec((1, TILE, DIFF_QK), lambda i, j: (i, j, 0)),
                   pl.BlockSpec((1, DIFF_HEADS * (2 * DIFF_HD + V_ONES), TILE), lambda i, j: (i, 0, j))],
        out_shape=[jax.ShapeDtypeStruct((b, n_tok, DIFF_QK), BF16),
                   jax.ShapeDtypeStruct((b, n_tok, DIFF_QK), BF16),
                   jax.ShapeDtypeStruct((b, DIFF_HEADS * (2 * DIFF_HD + V_ONES), n_tok), BF16)],
        compiler_params=_cparams(("arbitrary", "arbitrary")),
        name="diff_prep",
    )(u, tab, qng, kng)


def _scores_pass(k_ref, s_ref, hl, q, n_keys):
    m = None
    for start in range(0, n_keys, KEY_BLOCK):
        s = _dot_nt(k_ref[0, start:start + KEY_BLOCK, hl], q)
        s_ref[start:start + KEY_BLOCK, :] = s
        part = jnp.max(s, axis=0, keepdims=True)
        m = part if m is None else jnp.maximum(m, part)
    return m


def _pv_pass(vt_ref, s_ref, vrows, m, n_keys):
    acc = None
    for start in range(0, n_keys, KEY_BLOCK):
        p = jnp.exp2(s_ref[start:start + KEY_BLOCK, :] - m).astype(BF16)
        pv = _dot(vt_ref[0, vrows, start:start + KEY_BLOCK], p)
        acc = pv if acc is None else acc + pv
    return acc


def _attn_kernel(q_ref, k_ref, vt_ref, *rest, n_heads, dv, diff, lam_init, first_tile):
    if diff:
        lamp_ref, subg_ref, o_ref, ot_ref, s0_ref, s1_ref = rest
        lp = lamp_ref[...]
        lam = (jnp.exp(jnp.sum(lp[0:1] * lp[1:2], axis=-1, keepdims=True))
               - jnp.exp(jnp.sum(lp[2:3] * lp[3:4], axis=-1, keepdims=True)) + lam_init)
        lane = _iota((TILE, LANES), 1)
    else:
        o_ref, ot_ref, s0_ref = rest
    tile = pl.program_id(1) + first_tile
    dva = dv + V_ONES

    def run(n_keys):
        def head(h, carry):
            hl = pl.ds(pl.multiple_of(h * LANES, LANES), LANES)
            vrows = pl.ds(pl.multiple_of(h * dva, V_ONES), dva)
            q = q_ref[0, :, hl]
            if diff:
                zero = jnp.zeros_like(q)
                m0 = _scores_pass(k_ref, s0_ref, hl, jnp.where(lane < DIFF_HD, q, zero), n_keys)
                m1 = _scores_pass(k_ref, s1_ref, hl, jnp.where(lane < DIFF_HD, zero, q), n_keys)
                a0 = _pv_pass(vt_ref, s0_ref, vrows, m0, n_keys)
                a1 = _pv_pass(vt_ref, s1_ref, vrows, m1, n_keys)
                o = a0[:dv] / a0[dv:dv + 1] - lam * (a1[:dv] / a1[dv:dv + 1])
                ms = jnp.mean(o * o, axis=0, keepdims=True)
                o = o * lax.rsqrt(ms + NORM_EPS) * subg_ref[...] * (1.0 - lam_init)
            else:
                a = _pv_pass(vt_ref, s0_ref, vrows, _scores_pass(k_ref, s0_ref, hl, q, n_keys), n_keys)
                o = a[:dv] / a[dv:dv + 1]
            ot_ref[pl.ds(pl.multiple_of(h * dv, dv), dv), :] = o
            return carry

        lax.fori_loop(0, n_heads, head, 0)
        o_ref[0] = ot_ref[...].T

    if first_tile == 0:
        pl.when(tile == 0)(lambda: run(TILE))
        pl.when(tile > 0)(lambda: run(k_ref.shape[1]))
    else:
        run(k_ref.shape[1])


def _attention(q, k, vt, *, n_heads, dv, need_ctx, diff=False, lam_p=None, sub_g=None, lam_init=0.0):
    b, n_tok, w = q.shape
    t = n_tok // TILE
    first = 0 if need_ctx else 1
    ow = n_heads * dv
    in_specs = [pl.BlockSpec((1, TILE, w), lambda i, j: (i, j + first, 0)),
                pl.BlockSpec((1, n_tok, w), lambda i, j: (i, 0, 0)),
                pl.BlockSpec((1, n_heads * (dv + V_ONES), n_tok), lambda i, j: (i, 0, 0))]
    args = [q, k, vt]
    if diff:
        in_specs += [pl.BlockSpec(lam_p.shape, lambda i, j: (0, 0)), pl.BlockSpec(sub_g.shape, lambda i, j: (0, 0))]
        args += [lam_p, sub_g]
    return pl.pallas_call(
        functools.partial(_attn_kernel, n_heads=n_heads, dv=dv, diff=diff, lam_init=lam_init, first_tile=first),
        grid=(b, t - first),
        in_specs=in_specs,
        out_specs=pl.BlockSpec((1, TILE, ow), lambda i, j: (i, j + first, 0)),
        out_shape=jax.ShapeDtypeStruct((b, n_tok, ow), F32),
        scratch_shapes=[pltpu.VMEM((ow, TILE), F32)] + [pltpu.VMEM((n_tok, TILE), F32)] * (2 if diff else 1),
        compiler_params=_cparams(("arbitrary", "arbitrary")),
        name="diff_attn" if diff else "mla_attn",
    )(*args)


def _merge_kernel(x_ref, mod_ref, gof_ref, gob_ref, gz_ref, mla_ref, dif_ref, syf_ref, syb_ref, sz_ref,
                  gates01_ref, gates23_ref, gng_ref, sng_ref, wb_ref, wo_ref, o_ref):
    o = gof_ref[0] + gob_ref[0]
    z = gz_ref[0]
    ya = []
    for h in range(GDN_HEADS):
        hl = slice(h * GDN_DV, (h + 1) * GDN_DV)
        ya.append(_rms(o[:, hl], gng_ref[...]) * _silu(z[:, hl]))
    ya = jnp.concatenate(ya, axis=-1)
    y = (syf_ref[0] + syb_ref[0]) * _silu(sz_ref[0])
    gsz = SSM_INNER // SSM_GROUPS
    yd = jnp.concatenate([_rms(y[:, g * gsz:(g + 1) * gsz], sng_ref[:, g * gsz:(g + 1) * gsz])
                          for g in range(SSM_GROUPS)], axis=-1)
    ys = (ya, mla_ref[0], dif_ref[0], yd)
    m = None
    for i in range(N_BRANCH):
        gates_ref = gates01_ref if i < 2 else gates23_ref
        gate = _sigmoid(gates_ref[0, :, (i % 2) * D_MODEL:(i % 2 + 1) * D_MODEL])
        term = gate * _dot(ys[i].astype(BF16), wb_ref[i])
        m = term if m is None else m + term
    o_ref[0] = x_ref[0] + mod_ref[2:3, :] * _dot(m.astype(BF16), wo_ref[...])


def _merge(xs, mods, gdn_o, u, y_mla, y_diff, ssd_y, gng, sng, wb, wo, *, need_ctx):
    b, n_tok, d = xs.shape
    t = n_tok // TILE
    first = 0 if need_ctx else 1
    row = lambda w, cb=0: pl.BlockSpec((1, TILE, w), lambda i, j: (i, j + first, cb))
    const = lambda shape: pl.BlockSpec(shape, lambda i, j: (0,) * len(shape))
    return pl.pallas_call(
        _merge_kernel,
        grid=(b, t - first),
        in_specs=[row(d),
                  pl.BlockSpec((None, None, 6, d), lambda i, j: (i, jnp.minimum(j + first, 1), 0, 0)),
                  row(BRANCH_W), row(BRANCH_W), row(BRANCH_W, (C_GDN + 3 * BRANCH_W) // BRANCH_W),
                  row(BRANCH_W), row(BRANCH_W), row(BRANCH_W), row(BRANCH_W), row(BRANCH_W, C_SSMZ // BRANCH_W),
                  row(GATE_COLS // 2, C_GATE // (GATE_COLS // 2)), row(GATE_COLS // 2, C_GATE // (GATE_COLS // 2) + 1),
                  const((1, GDN_DV)), const((1, SSM_INNER)), const((N_BRANCH, BRANCH_W, d)), const((d, d))],
        out_specs=row(d),
        out_shape=jax.ShapeDtypeStruct((b, n_tok, d), F32),
        input_output_aliases={0: 0},
        compiler_params=_cparams(("arbitrary", "arbitrary")),
        name="merge",
    )(xs, mods, gdn_o[0], gdn_o[1], u, y_mla, y_diff, ssd_y[0], ssd_y[1], u,
      u, u, gng, sng, wb, wo)


def _mlp_kernel(x_ref, mod_ref, g_ref, w1_ref, w2_ref, o_ref):
    x = x_ref[0]
    h = _rms(x, g_ref[...]) * (1.0 + mod_ref[4:5, :]) + mod_ref[3:4, :]
    a = jnp.maximum(_dot(h.astype(BF16), w1_ref[...]), 0.0)
    o_ref[0] = x + mod_ref[5:6, :] * _dot((a * a).astype(BF16), w2_ref[...])


def _mlp(xs, mods, g, w1, w2, *, need_ctx):
    b, n_tok, d = xs.shape
    t = n_tok // TILE
    first = 0 if need_ctx else 1
    const = lambda shape: pl.BlockSpec(shape, lambda i, j: (0,) * len(shape))
    n_out = n_tok - first * TILE
    return pl.pallas_call(
        _mlp_kernel,
        grid=(b, t - first),
        in_specs=[pl.BlockSpec((1, TILE, d), lambda i, j: (i, j + first, 0)),
                  pl.BlockSpec((None, None, 6, d), lambda i, j: (i, jnp.minimum(j + first, 1), 0, 0)),
                  const((1, d)), const((d, D_FF)), const((D_FF, d))],
        out_specs=pl.BlockSpec((1, TILE, d), lambda i, j: (i, j, 0)),
        out_shape=jax.ShapeDtypeStruct((b, n_out, d), F32),
        compiler_params=_cparams(("arbitrary", "arbitrary")),
        name="mlp",
    )(xs, mods, g, w1, w2)


def _repack_w_in(w):
    d = w.shape[0]
    z = lambda n: jnp.zeros((d, n), w.dtype)
    o_mla = GDN_COLS
    o_diff = o_mla + MLA_COLS
    o_ssm = o_diff + DIFF_COLS
    o_gate = MIX_COLS
    gdn_ab = w[:, 2 * GDN_QK + 2 * GDN_VW:GDN_COLS]
    ssm_dt = w[:, o_ssm + 2 * SSM_INNER + 2 * SSM_BC:o_ssm + SSM_COLS]
    kpe = w[:, o_mla + MLA_Q_LORA + MLA_KV_LORA:o_mla + MLA_COLS]
    parts = [
        w[:, 0:2 * GDN_QK + 2 * GDN_VW],
        w[:, o_ssm + SSM_INNER:o_ssm + 2 * SSM_INNER + 2 * SSM_BC],
        w[:, o_diff:o_diff + DIFF_COLS],
        w[:, o_mla:o_mla + MLA_Q_LORA + MLA_KV_LORA], z(MLA_NOPE), kpe, z(LANES - MLA_QK),
        gdn_ab, ssm_dt, z(LANES - 4 * GDN_HEADS - 2 * SSM_HEADS),
        z(C_SSMZ - C_SMALL - LANES),
        w[:, o_ssm:o_ssm + SSM_INNER],
        w[:, o_gate:o_gate + GATE_COLS],
    ]
    out = jnp.concatenate(parts, axis=1)
    assert out.shape[1] == N_IN
    return out.astype(BF16)


def _lane_row(vals, offset):
    row = jnp.zeros((1, LANES), F32)
    return row.at[0, offset:offset + vals.shape[0]].set(vals.astype(F32))


def _rope_tables(n_lat, rot_dim, lane_offset, period):
    rows = n_lat // GRID_W
    row = jnp.repeat(jnp.arange(rows, dtype=F32), GRID_W)
    col = jnp.tile(jnp.arange(GRID_W, dtype=F32), rows)
    quarter = rot_dim // 4
    inv = ROPE_THETA ** (-jnp.arange(quarter, dtype=F32) / quarter)
    ar = row[:, None] * inv
    ac = col[:, None] * inv
    ang = jnp.concatenate([ar, ar, ac, ac], axis=-1)
    cos, sin = jnp.cos(ang), jnp.sin(ang)
    first = (np.arange(rot_dim) % (2 * quarter)) < quarter
    sin_a = jnp.where(first, -sin, 0.0)
    sin_b = jnp.where(first, 0.0, sin)
    reps = LANES // period
    def place(t, fill):
        blk = jnp.full((n_lat, period), fill, F32).at[:, lane_offset:lane_offset + rot_dim].set(t)
        blk = jnp.tile(blk, (1, reps))
        ctx = jnp.full((CTX_LEN, LANES), fill, F32)
        return jnp.concatenate([ctx, blk], axis=0)
    return jnp.stack([place(cos, 1.0), place(sin_a, 0.0), place(sin_b, 0.0)])


def kernel(x, c, ctx, c_ctx, ada_w, ada_b, norm1_g, norm2_g, w_in, gdn_conv, gdn_a_log, gdn_dt_bias, gdn_norm_g,
           mla_q_lora_g, mla_kv_lora_g, mla_w_uq, mla_w_ukv, mla_qn_g, mla_kn_g, diff_qn_g, diff_kn_g,
           diff_lambda, diff_sub_g, ssm_conv, ssm_conv_b, ssm_a_log, ssm_dt_bias, ssm_d, ssm_norm_g, w_branch,
           w_out, mlp_w1, mlp_w2):
    b, n_lat, d = x.shape
    depth = w_in.shape[0]
    assert ctx.shape[1] == CTX_LEN == TILE and n_lat % TILE == 0 and d == D_MODEL
    xs = jnp.concatenate([ctx, x], axis=1)
    tab_mla = _rope_tables(n_lat, MLA_ROPE, MLA_NOPE, LANES)
    tab_diff = _rope_tables(n_lat, DIFF_HD, 0, DIFF_HD)
    rows = b + 8
    cc = jnp.zeros((rows, d), F32).at[:b].set(c).at[b].set(c_ctx)
    row2 = lambda v: v.reshape(1, -1).astype(F32)

    for l in range(depth):
        need_ctx = l < depth - 1
        lam_init = LAMBDA_BASE - LAMBDA_AMP * math.exp(-LAMBDA_RATE * l)
        mod = _ada(cc, ada_w[l], ada_b[l].reshape(1, -1))
        mod_lat = mod[:b].reshape(b, 6, d)
        mod_ctx = jnp.broadcast_to(mod[b].reshape(1, 6, d), (b, 6, d))
        mods = jnp.stack([mod_ctx, mod_lat], axis=1)

        u = _inproj(xs, mods, row2(norm1_g[l]), _repack_w_in(w_in[l]))

        gdn_o = _gdn_scan(_gdn_prep(u, gdn_conv[l].astype(F32),
                                    _lane_row(gdn_a_log[l].reshape(-1), SM_A),
                                    _lane_row(gdn_dt_bias[l].reshape(-1), SM_A)))

        wq = mla_w_uq[l].reshape(MLA_Q_LORA, MLA_HEADS, MLA_QK)
        wq = jnp.pad(wq, ((0, 0), (0, 0), (0, LANES - MLA_QK))).reshape(MLA_Q_LORA, MLA_HEADS * LANES)
        wkv = mla_w_ukv[l].reshape(MLA_KV_LORA, MLA_HEADS, MLA_NOPE + MLA_V)
        wk = jnp.pad(wkv[:, :, :MLA_NOPE], ((0, 0), (0, 0), (0, LANES - MLA_NOPE))).reshape(MLA_KV_LORA, MLA_HEADS * LANES)
        wvt = wkv[:, :, MLA_NOPE:].reshape(MLA_KV_LORA, MLA_HEADS * MLA_V).T
        pad_g = lambda g: jnp.pad(g.astype(F32), (0, LANES - MLA_QK)).reshape(1, LANES)
        q_m, k_m, vt_m = _mla_prep(u, tab_mla, row2(mla_q_lora_g[l]), row2(mla_kv_lora_g[l]),
                                   wq.astype(BF16), wk.astype(BF16), wvt.astype(BF16),
                                   pad_g(mla_qn_g[l]), pad_g(mla_kn_g[l]))
        y_mla = _attention(q_m, k_m, vt_m, n_heads=MLA_HEADS, dv=MLA_V, need_ctx=need_ctx)

        rep_g = lambda g: jnp.tile(g.astype(F32), LANES // DIFF_HD).reshape(1, LANES)
        q_d, k_d, vt_d = _diff_prep(u, tab_diff, rep_g(diff_qn_g[l]), rep_g(diff_kn_g[l]))
        y_diff = _attention(q_d, k_d, vt_d, n_heads=DIFF_HEADS, dv=2 * DIFF_HD, need_ctx=need_ctx, diff=True,
                            lam_p=diff_lambda[l].astype(F32), sub_g=diff_sub_g[l].astype(F32).reshape(-1, 1),
                            lam_init=lam_init)

        ssd_y = _ssd(u, ssm_conv[l].astype(F32), row2(ssm_conv_b[l]),
                     _lane_row(ssm_a_log[l].reshape(-1), SM_DT), _lane_row(ssm_dt_bias[l].reshape(-1), SM_DT),
                     jnp.repeat(ssm_d[l].astype(F32), SSM_HEAD_DIM).reshape(1, -1))

        xs = _merge(xs, mods, gdn_o, u, y_mla, y_diff, ssd_y, row2(gdn_norm_g[l]), row2(ssm_norm_g[l]),
                    w_branch[l].astype(BF16), w_out[l].astype(BF16), need_ctx=need_ctx)
        xs = _mlp(xs, mods, row2(norm2_g[l]), mlp_w1[l].astype(BF16), mlp_w2[l].astype(BF16), need_ctx=need_ctx)
    return xs
```

```python
import functools
import math

import jax
import jax.numpy as jnp
import numpy as np
from jax import lax
from jax.experimental import pallas as pl
from jax.experimental.pallas import tpu as pltpu

F32 = jnp.float32
BF16 = jnp.bfloat16

D_MODEL = 1024
CTX_LEN = 256
GRID_W = 64
ROPE_THETA = 10000.0
NORM_EPS = 1e-6
CONV_K = 5
N_BRANCH = 4
D_FF = 4 * D_MODEL
GDN_HEADS = 4
GDN_DK = 128
GDN_DV = 128
MLA_HEADS = 8
MLA_NOPE = 64
MLA_ROPE = 32
MLA_V = 64
MLA_QK = MLA_NOPE + MLA_ROPE
MLA_Q_LORA = 384
MLA_KV_LORA = 256
DIFF_HEADS = 4
DIFF_HD = 64
LAMBDA_BASE = 0.8
LAMBDA_AMP = 0.6
LAMBDA_RATE = 0.3
SSM_HEADS = 8
SSM_HEAD_DIM = 64
SSM_GROUPS = 2
SSM_STATE = 128
GDN_QK = GDN_HEADS * GDN_DK
GDN_VW = GDN_HEADS * GDN_DV
DIFF_QK = DIFF_HEADS * 2 * DIFF_HD
DIFF_VW = DIFF_HEADS * 2 * DIFF_HD
SSM_INNER = SSM_HEADS * SSM_HEAD_DIM
SSM_BC = SSM_GROUPS * SSM_STATE
BRANCH_W = 512
GDN_COLS = 2 * GDN_QK + 2 * GDN_VW + 4 * GDN_HEADS
MLA_COLS = MLA_Q_LORA + MLA_KV_LORA + MLA_ROPE
DIFF_COLS = 2 * DIFF_QK + DIFF_VW
SSM_COLS = 2 * SSM_INNER + 2 * SSM_BC + 2 * SSM_HEADS
MIX_COLS = GDN_COLS + MLA_COLS + DIFF_COLS + SSM_COLS
GATE_COLS = N_BRANCH * D_MODEL

LANES = 128
TILE = 256
CHUNK = 64
CPT = TILE // CHUNK
HALO = 8
VMEM_LIMIT = 56 * 1024 * 1024
V_ONES = 16
KEY_BLOCK = 256
LOG2E = 1.4426950408889634

C_GDN = 0
C_XBC = 2048
C_DIFF = 3072
C_MLA = 4608
C_SMALL = 5376
C_SSMZ = 5632
C_GATE = 6144
N_IN = 10240
IN_TN = 2560

SM_A = 0
SM_B = 8
SM_DT = 16


def _cparams(sem):
    return pltpu.CompilerParams(dimension_semantics=sem, vmem_limit_bytes=VMEM_LIMIT)


def _dot(a, b):
    return jnp.dot(a, b, preferred_element_type=F32)


def _dot_nt(a, b):
    return lax.dot_general(a, b, (((1,), (1,)), ((), ())), preferred_element_type=F32)


def _dot_tn(a, b):
    return lax.dot_general(a, b, (((0,), (0,)), ((), ())), preferred_element_type=F32)


def _split3(x):
    x1 = x.astype(BF16)
    r1 = x - x1.astype(F32)
    x2 = r1.astype(BF16)
    r2 = r1 - x2.astype(F32)
    return x1, x2, r2.astype(BF16)


def _dot_exact_lhs(a_bf, x):
    x1, x2, x3 = _split3(x)
    return _dot(a_bf, x1) + _dot(a_bf, x2) + _dot(a_bf, x3)


def _dot_exact_rhs(x, b_bf):
    x1, x2, x3 = _split3(x)
    return _dot(x1, b_bf) + _dot(x2, b_bf) + _dot(x3, b_bf)


def _sigmoid(x):
    return 1.0 / (1.0 + jnp.exp(-x))


def _silu(x):
    return x * _sigmoid(x)


def _softplus(x):
    return jnp.maximum(x, 0.0) + jnp.log1p(jnp.exp(-jnp.abs(x)))


def _iota(shape, dim):
    return lax.broadcasted_iota(jnp.int32, shape, dim)


def _chunk_tri(n, upper):
    r = _iota((n, n), 0)
    c = _iota((n, n), 1)
    same = (r // CHUNK) == (c // CHUNK)
    tri = (r <= c) if upper else (r >= c)
    return jnp.where(same & tri, 1.0, 0.0).astype(BF16)


def _expander(n_rows, first_row, n_groups, width):
    r = _iota((n_rows, n_groups * width), 0)
    c = _iota((n_rows, n_groups * width), 1)
    return jnp.where(r == first_row + c // width, 1.0, 0.0).astype(BF16)


def _ada_kernel(c_ref, w_ref, b_ref, o_ref):
    a = _silu(c_ref[...])
    a1, a2, a3 = _split3(a)
    w1, w2, w3 = _split3(w_ref[...])
    acc = _dot(a1, w1) + (_dot(a1, w2) + _dot(a2, w1)) + (_dot(a1, w3) + _dot(a2, w2) + _dot(a3, w1))
    o_ref[...] = acc + b_ref[...]


def _ada(cc, w, b):
    rows, d = cc.shape
    n = w.shape[1]
    tn = 1536
    return pl.pallas_call(
        _ada_kernel,
        grid=(n // tn,),
        in_specs=[pl.BlockSpec((rows, d), lambda j: (0, 0)),
                  pl.BlockSpec((d, tn), lambda j: (0, j)),
                  pl.BlockSpec((1, tn), lambda j: (0, j))],
        out_specs=pl.BlockSpec((rows, tn), lambda j: (0, j)),
        out_shape=jax.ShapeDtypeStruct((rows, n), F32),
        compiler_params=_cparams(("arbitrary",)),
        name="ada",
    )(cc, w, b)


def _rms(x, g):
    ms = jnp.mean(x * x, axis=-1, keepdims=True)
    return x * lax.rsqrt(ms + NORM_EPS) * g


def _inproj_kernel(x_ref, mod_ref, g_ref, w_ref, o_ref):
    h = _rms(x_ref[0], g_ref[...])
    h = h * (1.0 + mod_ref[1:2, :]) + mod_ref[0:1, :]
    o_ref[0] = _dot(h.astype(BF16), w_ref[...])


def _inproj(xs, mods, g, w):
    b, n_tok, d = xs.shape
    t = n_tok // TILE
    nt = N_IN // IN_TN
    return pl.pallas_call(
        _inproj_kernel,
        grid=(nt, b, t),
        in_specs=[pl.BlockSpec((1, TILE, d), lambda n, i, j: (i, j, 0)),
                  pl.BlockSpec((None, None, 6, d), lambda n, i, j: (i, jnp.minimum(j, 1), 0, 0)),
                  pl.BlockSpec((1, d), lambda n, i, j: (0, 0)),
                  pl.BlockSpec((d, IN_TN), lambda n, i, j: (0, n))],
        out_specs=pl.BlockSpec((1, TILE, IN_TN), lambda n, i, j: (i, j, n)),
        out_shape=jax.ShapeDtypeStruct((b, n_tok, N_IN), F32),
        compiler_params=_cparams(("arbitrary", "arbitrary", "arbitrary")),
        name="inproj",
    )(xs, mods, g, w)


def _halo_specs(width, col_block, n_tiles, tile_of):
    rpt = TILE // HALO
    last = n_tiles * rpt - 1
    main = pl.BlockSpec((1, TILE, width), lambda *ids: (ids[0], tile_of(*ids), col_block))
    prev = pl.BlockSpec((1, HALO, width),
                        lambda *ids: (ids[0], jnp.maximum(tile_of(*ids) * rpt - 1, 0), col_block))
    nxt = pl.BlockSpec((1, HALO, width),
                       lambda *ids: (ids[0], jnp.minimum(tile_of(*ids) * rpt + rpt, last), col_block))
    return main, prev, nxt


def _conv_tile(main_ref, prev_ref, next_ref, w_ref, ext_ref, tile, n_tiles):
    prev_ok = (tile >= 2).astype(F32)
    next_ok = jnp.logical_and(tile >= 1, tile < n_tiles - 1).astype(F32)
    ext_ref[0:HALO, :] = prev_ref[0] * prev_ok
    ext_ref[HALO:HALO + TILE, :] = main_ref[0]
    ext_ref[HALO + TILE:2 * HALO + TILE, :] = next_ref[0] * next_ok
    half = CONV_K // 2
    acc = None
    for k in range(CONV_K):
        term = w_ref[k:k + 1, :] * ext_ref[HALO - half + k:HALO - half + k + TILE, :]
        acc = term if acc is None else acc + term
    return acc


def _block_inverse_many(mats):
    n = mats[0].shape[0]
    ri = _iota((n, n), 0)
    ci = _iota((n, n), 1)
    eye = jnp.where(ri == ci, 1.0, 0.0).astype(F32)
    ts = [eye - jnp.where((ri // 2) == (ci // 2), a, 0.0) for a in mats]
    abs_ = [a.astype(BF16) for a in mats]
    zero = jnp.zeros((n, n), BF16)
    size = 2
    while size < CHUNK:
        off = ((ri // (2 * size)) == (ci // (2 * size))) & ((ri // size) != (ci // size))
        tbs = [t.astype(BF16) for t in ts]
        xs = [_dot(jnp.where(off, ab, zero), tb).astype(BF16) for ab, tb in zip(abs_, tbs)]
        ts = [t - _dot(tb, x) for t, tb, x in zip(ts, tbs, xs)]
        size *= 2
    return ts


def _gdn_prep_kernel(main_ref, prev_ref, next_ref, sm_ref, cw_ref, alog_ref, dtb_ref,
                     u0_ref, wq0_ref, kd0_ref, at0_ref, gl0_ref,
                     u1_ref, wq1_ref, kd1_ref, at1_ref, gl1_ref,
                     ext_ref, *, n_tiles):
    tile = pl.program_id(1)
    qkv = _silu(_conv_tile(main_ref, prev_ref, next_ref, cw_ref, ext_ref, tile, n_tiles))
    sm = sm_ref[0]
    g_all = -jnp.exp(alog_ref[...]) * _softplus(sm + dtb_ref[...])
    beta_all = _sigmoid(sm)
    nh = GDN_HEADS
    exp_g = _expander(LANES, SM_A, 2 * nh, LANES)
    exp_b = _expander(LANES, SM_B, 2 * nh, LANES)
    beta_x = _dot_exact_rhs(beta_all, exp_b)
    outs = ((u0_ref, wq0_ref, kd0_ref, at0_ref, gl0_ref), (u1_ref, wq1_ref, kd1_ref, at1_ref, gl1_ref))
    ri = _iota((TILE, TILE), 0)
    ci = _iota((TILE, TILE), 1)
    same = (ri // CHUNK) == (ci // CHUNK)
    cums, masks = [], []
    for d in range(2):
        cum = _dot_exact_lhs(_chunk_tri(TILE, upper=(d == 1)), g_all)
        cums.append((_dot_exact_rhs(cum, exp_g), cum.T))
        masks.append((same & ((ri >= ci) if d == 0 else (ri <= ci)), same & ((ri > ci) if d == 0 else (ri < ci))))

    chains, a_mats = [], []
    for h in range(nh):
        q = qkv[:, h * GDN_DK:(h + 1) * GDN_DK]
        k = qkv[:, GDN_QK + h * GDN_DK:GDN_QK + (h + 1) * GDN_DK]
        v = qkv[:, 2 * GDN_QK + h * GDN_DV:2 * GDN_QK + (h + 1) * GDN_DV]
        q = q * lax.rsqrt(jnp.sum(q * q, axis=-1, keepdims=True) + 1e-6) * (GDN_DK ** -0.5)
        k = k * lax.rsqrt(jnp.sum(k * k, axis=-1, keepdims=True) + 1e-6)
        kb = k.astype(BF16)
        kq = _dot_nt(jnp.concatenate([kb, q.astype(BF16)], axis=0), kb)
        for d in range(2):
            col = d * nh + h
            lanes = slice(col * LANES, (col + 1) * LANES)
            cum_x, cum_t = cums[d]
            incl, strict = masks[d]
            g_col = cum_x[:, lanes]
            b_col = beta_x[:, lanes]
            diff = jnp.concatenate([g_col, g_col], axis=1) - cum_t[SM_A + col:SM_A + col + 1, :]
            decay = jnp.where(incl, jnp.exp(jnp.where(incl, diff, 0.0)), 0.0)
            a_mats.append(jnp.where(strict, kq[:TILE] * jnp.concatenate([b_col, b_col], axis=1) * decay, 0.0))
            e_g = jnp.exp(g_col)
            rhs = jnp.concatenate([v * b_col, k * b_col * e_g], axis=1).astype(BF16)
            chains.append((h, d, q, k, g_col, e_g, rhs, kq[TILE:] * decay))

    t_mats = _block_inverse_many(a_mats)
    for (h, d, q, k, g_col, e_g, rhs, attn), t_mat in zip(chains, t_mats):
        uw = _dot(t_mat.astype(BF16), rhs)
        qg = (q * e_g).astype(BF16)
        u_ref, wq_ref, kd_ref, at_ref, gl_ref = outs[d]
        hl = slice(h * LANES, (h + 1) * LANES)
        for c in range(CPT):
            rows = slice(c * CHUNK, (c + 1) * CHUNK)
            last = c * CHUNK + (CHUNK - 1 if d == 0 else 0)
            g_last = g_col[last:last + 1, :]
            slot = c if d == 0 else CPT - 1 - c
            u_ref[0, 0, slot, :, hl] = uw[rows, :GDN_DV]
            wq_ref[0, 0, slot, 0:CHUNK, hl] = uw[rows, GDN_DV:].astype(BF16)
            wq_ref[0, 0, slot, CHUNK:2 * CHUNK, hl] = qg[rows]
            kd_ref[0, 0, slot, :, hl] = (k[rows] * jnp.exp(g_last - g_col[rows])).astype(BF16)
            at_ref[0, 0, slot, :, h * CHUNK:(h + 1) * CHUNK] = attn[rows, rows].astype(BF16)
            gl_ref[0, 0, slot, :, hl] = jnp.exp(g_last)


def _bwd_block(j, n_tiles):
    return jnp.where(j == 0, 0, n_tiles - j)


def _gdn_prep(u, conv_w, alog_row, dtb_row):
    b, n_tok, _ = u.shape
    t = n_tok // TILE
    w = GDN_VW
    cw = 2 * GDN_QK + GDN_VW
    main, prev, nxt = _halo_specs(cw, 0, t, lambda i, j: j)
    fwd = lambda i, j: (i, j, 0, 0, 0)
    bwd = lambda i, j: (i, _bwd_block(j, t), 0, 0, 0)

    def outs(imap):
        return [pl.BlockSpec((1, 1, CPT, CHUNK, w), imap),
                pl.BlockSpec((1, 1, CPT, 2 * CHUNK, w), imap),
                pl.BlockSpec((1, 1, CPT, CHUNK, w), imap),
                pl.BlockSpec((1, 1, CPT, CHUNK, GDN_HEADS * CHUNK), imap),
                pl.BlockSpec((1, 1, CPT, 1, w), imap)]

    shapes = [jax.ShapeDtypeStruct((b, t, CPT, CHUNK, w), F32),
              jax.ShapeDtypeStruct((b, t, CPT, 2 * CHUNK, w), BF16),
              jax.ShapeDtypeStruct((b, t, CPT, CHUNK, w), BF16),
              jax.ShapeDtypeStruct((b, t, CPT, CHUNK, GDN_HEADS * CHUNK), BF16),
              jax.ShapeDtypeStruct((b, t, CPT, 1, w), F32)]
    return pl.pallas_call(
        functools.partial(_gdn_prep_kernel, n_tiles=t),
        grid=(b, t),
        in_specs=[main, prev, nxt,
                  pl.BlockSpec((1, TILE, LANES), lambda i, j: (i, j, C_SMALL // LANES)),
                  pl.BlockSpec((CONV_K, cw), lambda i, j: (0, 0)),
                  pl.BlockSpec((1, LANES), lambda i, j: (0, 0)),
                  pl.BlockSpec((1, LANES), lambda i, j: (0, 0))],
        out_specs=outs(fwd) + outs(bwd),
        out_shape=shapes + shapes,
        scratch_shapes=[pltpu.VMEM((TILE + 2 * HALO, cw), F32)],
        compiler_params=_cparams(("arbitrary", "arbitrary")),
        name="gdn_prep",
    )(u, u, u, u, conv_w, alog_row, dtb_row)


def _gdn_scan_kernel(u0_ref, wq0_ref, kd0_ref, at0_ref, gl0_ref,
                     u1_ref, wq1_ref, kd1_ref, at1_ref, gl1_ref,
                     of_ref, ob_ref, s_ref):
    @pl.when(pl.program_id(1) == 0)
    def _():
        s_ref[...] = jnp.zeros_like(s_ref)

    ins = ((u0_ref, wq0_ref, kd0_ref, at0_ref, gl0_ref), (u1_ref, wq1_ref, kd1_ref, at1_ref, gl1_ref))
    chains = [(d, h) for d in range(2) for h in range(GDN_HEADS)]
    hls = [slice(h * LANES, (h + 1) * LANES) for _, h in chains]
    states = [s_ref[d, h] for d, h in chains]
    for slot in range(CPT):
        rs = [_dot(ins[d][1][0, 0, slot, :, hl], s.astype(BF16)) for (d, _), hl, s in zip(chains, hls, states)]
        vbs = [(ins[d][0][0, 0, slot, :, hl] - r[0:CHUNK]).astype(BF16) for (d, _), hl, r in zip(chains, hls, rs)]
        outs = [r[CHUNK:2 * CHUNK] + _dot(ins[d][3][0, 0, slot, :, h * CHUNK:(h + 1) * CHUNK], vb)
                for (d, h), r, vb in zip(chains, rs, vbs)]
        states = [s * ins[d][4][0, 0, slot, :, hl] + _dot_tn(ins[d][2][0, 0, slot, :, hl], vb)
                  for (d, _), hl, s, vb in zip(chains, hls, states, vbs)]
        for (d, _), hl, o in zip(chains, hls, outs):
            c = slot if d == 0 else CPT - 1 - slot
            (of_ref if d == 0 else ob_ref)[0, c * CHUNK:(c + 1) * CHUNK, hl] = o
    for (d, h), s in zip(chains, states):
        s_ref[d, h] = s


def _gdn_scan(prep):
    b, t = prep[0].shape[:2]
    w = GDN_VW
    imap = lambda i, j: (i, j, 0, 0, 0)
    specs = [pl.BlockSpec((1, 1, CPT, CHUNK, w), imap),
             pl.BlockSpec((1, 1, CPT, 2 * CHUNK, w), imap),
             pl.BlockSpec((1, 1, CPT, CHUNK, w), imap),
             pl.BlockSpec((1, 1, CPT, CHUNK, GDN_HEADS * CHUNK), imap),
             pl.BlockSpec((1, 1, CPT, 1, w), imap)]
    return pl.pallas_call(
        _gdn_scan_kernel,
        grid=(b, t),
        in_specs=specs + specs,
        out_specs=[pl.BlockSpec((1, TILE, w), lambda i, j: (i, j, 0)),
                   pl.BlockSpec((1, TILE, w), lambda i, j: (i, _bwd_block(j, t), 0))],
        out_shape=[jax.ShapeDtypeStruct((b, t * TILE, w), F32)] * 2,
        scratch_shapes=[pltpu.VMEM((2, GDN_HEADS, GDN_DK, GDN_DV), F32)],
        compiler_params=_cparams(("arbitrary", "arbitrary")),
        name="gdn_scan",
    )(*prep)


def _ssd_kernel(mf_ref, pf_ref, nf_ref, smf_ref, mb_ref, pb_ref, nb_ref, smb_ref,
                cw_ref, cb_ref, alog_ref, dtb_ref, dskip_ref,
                yf_ref, yb_ref, ext_ref, st_ref, *, n_tiles):
    j = pl.program_id(1)

    @pl.when(j == 0)
    def _():
        st_ref[...] = jnp.zeros_like(st_ref)

    nh, hd, ng = SSM_HEADS, SSM_HEAD_DIM, SSM_GROUPS
    hpg = nh // ng
    gw = hpg * hd
    ri = _iota((CHUNK, CHUNK), 0)
    ci = _iota((CHUNK, CHUNK), 1)
    dirs = ((mf_ref, pf_ref, nf_ref, smf_ref, yf_ref, j), (mb_ref, pb_ref, nb_ref, smb_ref, yb_ref, _bwd_block(j, n_tiles)))
    for d in range(2):
        m_ref, p_ref, n_ref, sm_ref, y_ref, tile = dirs[d]
        xbc = _silu(_conv_tile(m_ref, p_ref, n_ref, cw_ref, ext_ref, tile, n_tiles) + cb_ref[...])
        xs = xbc[:, :SSM_INNER]
        dt_all = _softplus(sm_ref[0] + dtb_ref[...])
        da_all = dt_all * (-jnp.exp(alog_ref[...]))
        cum = _dot_exact_lhs(_chunk_tri(TILE, upper=(d == 1)), da_all)
        cum_t = cum.T
        expand = _expander(LANES, SM_DT + d * nh, nh, hd)
        dt_x = _dot_exact_rhs(dt_all, expand)
        cum_x = _dot_exact_rhs(cum, expand)
        xdt = xs * dt_x
        incl = (ri >= ci) if d == 0 else (ri <= ci)
        for step in range(CPT):
            c = step if d == 0 else CPT - 1 - step
            rows = slice(c * CHUNK, (c + 1) * CHUNK)
            last = c * CHUNK + (CHUNK - 1 if d == 0 else 0)
            cum_c = cum_x[rows]
            cum_last = cum_x[last:last + 1]
            decay_states = jnp.exp(cum_last - cum_c)
            in_decay = jnp.exp(cum_c)
            chunk_decay = jnp.exp(cum_last)
            for g in range(ng):
                gl = slice(g * gw, (g + 1) * gw)
                bm = xbc[rows, SSM_INNER + g * SSM_STATE:SSM_INNER + (g + 1) * SSM_STATE].astype(BF16)
                cm = xbc[rows, SSM_INNER + SSM_BC + g * SSM_STATE:SSM_INNER + SSM_BC + (g + 1) * SSM_STATE].astype(BF16)
                scores = _dot_nt(cm, bm)
                state = st_ref[d, g]
                y_off = _dot(cm, state.astype(BF16)) * in_decay[:, gl]
                new_states = _dot_tn(bm, (xdt[rows, gl] * decay_states[:, gl]).astype(BF16))
                st_ref[d, g] = state * chunk_decay[:, gl] + new_states
                for hh in range(hpg):
                    h = g * hpg + hh
                    hl = slice(h * hd, (h + 1) * hd)
                    col = SM_DT + d * nh + h
                    seg = cum_c[:, hl] - cum_t[col:col + 1, rows]
                    lmat = jnp.where(incl, jnp.exp(jnp.where(incl, seg, 0.0)), 0.0)
                    y = _dot((scores * lmat).astype(BF16), xdt[rows, hl].astype(BF16)) + y_off[:, hh * hd:(hh + 1) * hd]
                    if d == 0:
                        y = y + dskip_ref[:, hl] * xs[rows, hl]
                    y_ref[0, rows, hl] = y


def _ssd(u, conv_w, conv_b, alog_row, dtb_row, dskip_row):
    b, n_tok, _ = u.shape
    t = n_tok // TILE
    cw = SSM_INNER + 2 * SSM_BC
    cb = C_XBC // cw
    mf, pf, nf = _halo_specs(cw, cb, t, lambda i, j: j)
    mb, pb, nb = _halo_specs(cw, cb, t, lambda i, j: _bwd_block(j, t))
    smf = pl.BlockSpec((1, TILE, LANES), lambda i, j: (i, j, C_SMALL // LANES))
    smb = pl.BlockSpec((1, TILE, LANES), lambda i, j: (i, _bwd_block(j, t), C_SMALL // LANES))
    const = lambda r, c: pl.BlockSpec((r, c), lambda i, j: (0, 0))
    return pl.pallas_call(
        functools.partial(_ssd_kernel, n_tiles=t),
        grid=(b, t),
        in_specs=[mf, pf, nf, smf, mb, pb, nb, smb,
                  const(CONV_K, cw), const(1, cw), const(1, LANES), const(1, LANES), const(1, SSM_INNER)],
        out_specs=[pl.BlockSpec((1, TILE, SSM_INNER), lambda i, j: (i, j, 0)),
                   pl.BlockSpec((1, TILE, SSM_INNER), lambda i, j: (i, _bwd_block(j, t), 0))],
        out_shape=[jax.ShapeDtypeStruct((b, n_tok, SSM_INNER), F32)] * 2,
        scratch_shapes=[pltpu.VMEM((TILE + 2 * HALO, cw), F32),
                        pltpu.VMEM((2, SSM_GROUPS, SSM_STATE, SSM_INNER // SSM_GROUPS), F32)],
        compiler_params=_cparams(("arbitrary", "arbitrary")),
        name="ssd",
    )(u, u, u, u, u, u, u, u, conv_w, conv_b, alog_row, dtb_row, dskip_row)


def _rope(x, tab_ref, shift):
    n = x.shape[-1]
    return (x * tab_ref[0] + pltpu.roll(x, n - shift, 1) * tab_ref[1] + pltpu.roll(x, shift, 1) * tab_ref[2])


def _store_vt(vt_ref, vt, n_heads, dv):
    ones = jnp.ones((V_ONES, vt.shape[1]), BF16)
    for h in range(n_heads):
        base = h * (dv + V_ONES)
        vt_ref[0, base:base + dv, :] = vt[h * dv:(h + 1) * dv].astype(BF16)
        vt_ref[0, base + dv:base + dv + V_ONES, :] = ones


def _mla_prep_kernel(u_ref, tab_ref, qlg_ref, kvlg_ref, wq_ref, wk_ref, wvt_ref, qng_ref, kng_ref,
                     q_ref, k_ref, vt_ref):
    u = u_ref[0]
    cq = _rms(u[:, :MLA_Q_LORA], qlg_ref[...]).astype(BF16)
    ckv = _rms(u[:, MLA_Q_LORA:MLA_Q_LORA + MLA_KV_LORA], kvlg_ref[...]).astype(BF16)
    kpe = u[:, MLA_Q_LORA + MLA_KV_LORA:]
    q_all = _dot(cq, wq_ref[...])
    k_all = _dot(ckv, wk_ref[...])
    _store_vt(vt_ref, _dot_nt(wvt_ref[...], ckv), MLA_HEADS, MLA_V)
    scale = MLA_QK ** -0.5 * LOG2E
    for h in range(MLA_HEADS):
        hl = slice(h * LANES, (h + 1) * LANES)
        qh = q_all[:, hl]
        qh = qh * lax.rsqrt(jnp.sum(qh * qh, axis=-1, keepdims=True) / MLA_QK + NORM_EPS) * qng_ref[...]
        q_ref[0, :, hl] = (_rope(qh, tab_ref, MLA_ROPE // 4) * scale).astype(BF16)
        kh = k_all[:, hl] + kpe
        kh = kh * lax.rsqrt(jnp.sum(kh * kh, axis=-1, keepdims=True) / MLA_QK + NORM_EPS) * kng_ref[...]
        k_ref[0, :, hl] = _rope(kh, tab_ref, MLA_ROPE // 4).astype(BF16)


def _mla_prep(u, tab, qlg, kvlg, wq, wk, wvt, qng, kng):
    b, n_tok, _ = u.shape
    t = n_tok // TILE
    w = MLA_HEADS * LANES
    cw = MLA_Q_LORA + MLA_KV_LORA + LANES
    const = lambda r, c: pl.BlockSpec((r, c), lambda i, j: (0, 0))
    return pl.pallas_call(
        _mla_prep_kernel,
        grid=(b, t),
        in_specs=[pl.BlockSpec((1, TILE, cw), lambda i, j: (i, j, C_MLA // cw)),
                  pl.BlockSpec((3, TILE, LANES), lambda i, j: (0, j, 0)),
                  const(1, MLA_Q_LORA), const(1, MLA_KV_LORA), const(MLA_Q_LORA, w), const(MLA_KV_LORA, w),
                  const(MLA_HEADS * MLA_V, MLA_KV_LORA), const(1, LANES), const(1, LANES)],
        out_specs=[pl.BlockSpec((1, TILE, w), lambda i, j: (i, j, 0)),
                   pl.BlockSpec((1, TILE, w), lambda i, j: (i, j, 0)),
                   pl.BlockSpec((1, MLA_HEADS * (MLA_V + V_ONES), TILE), lambda i, j: (i, 0, j))],
        out_shape=[jax.ShapeDtypeStruct((b, n_tok, w), BF16),
                   jax.ShapeDtypeStruct((b, n_tok, w), BF16),
                   jax.ShapeDtypeStruct((b, MLA_HEADS * (MLA_V + V_ONES), n_tok), BF16)],
        compiler_params=_cparams(("arbitrary", "arbitrary")),
        name="mla_prep",
    )(u, tab, qlg, kvlg, wq, wk, wvt, qng, kng)


def _diff_prep_kernel(u_ref, tab_ref, qng_ref, kng_ref, q_ref, k_ref, vt_ref):
    u = u_ref[0]
    lane = _iota((TILE, LANES), 1)
    lo = lane < DIFF_HD

    def norm_rope(x, g_ref):
        x2 = x * x
        s_lo = jnp.sum(jnp.where(lo, x2, 0.0), axis=-1, keepdims=True)
        s_hi = jnp.sum(jnp.where(lo, 0.0, x2), axis=-1, keepdims=True)
        ms = jnp.where(lo, s_lo, s_hi) / DIFF_HD
        return _rope(x * lax.rsqrt(ms + NORM_EPS) * g_ref[...], tab_ref, DIFF_HD // 4)

    for h in range(DIFF_HEADS):
        hl = slice(h * LANES, (h + 1) * LANES)
        q_ref[0, :, hl] = (norm_rope(u[:, hl], qng_ref) * (DIFF_HD ** -0.5 * LOG2E)).astype(BF16)
        k_ref[0, :, hl] = norm_rope(u[:, DIFF_QK + h * LANES:DIFF_QK + (h + 1) * LANES], kng_ref).astype(BF16)
    _store_vt(vt_ref, u[:, 2 * DIFF_QK:].T, DIFF_HEADS, 2 * DIFF_HD)


def _diff_prep(u, tab, qng, kng):
    b, n_tok, _ = u.shape
    t = n_tok // TILE
    const = lambda r, c: pl.BlockSpec((r, c), lambda i, j: (0, 0))
    return pl.pallas_call(
        _diff_prep_kernel,
        grid=(b, t),
        in_specs=[pl.BlockSpec((1, TILE, DIFF_COLS), lambda i, j: (i, j, C_DIFF // DIFF_COLS)),
                  pl.BlockSpec((3, TILE, LANES), lambda i, j: (0, j, 0)),
                  const(1, LANES), const(1, LANES)],
        out_specs=[pl.BlockSpec((1, TILE, DIFF_QK), lambda i, j: (i, j, 0)),
                   pl.BlockSpec((1, TILE, DIFF_QK), lambda i, j: (i, j, 0)),
                   pl.BlockSpec((1, DIFF_HEADS * (2 * DIFF_HD + V_ONES), TILE), lambda i, j: (i, 0, j))],
        out_shape=[jax.ShapeDtypeStruct((b, n_tok, DIFF_QK), BF16),
                   jax.ShapeDtypeStruct((b, n_tok, DIFF_QK), BF16),
                   jax.ShapeDtypeStruct((b, DIFF_HEADS * (2 * DIFF_HD + V_ONES), n_tok), BF16)],
        compiler_params=_cparams(("arbitrary", "arbitrary")),
        name="diff_prep",
    )(u, tab, qng, kng)


def _scores_pass(k_ref, s_ref, hl, q, n_keys):
    m = None
    for start in range(0, n_keys, KEY_BLOCK):
        s = _dot_nt(k_ref[0, start:start + KEY_BLOCK, hl], q)
        s_ref[start:start + KEY_BLOCK, :] = s
        part = jnp.max(s, axis=0, keepdims=True)
        m = part if m is None else jnp.maximum(m, part)
    return m


def _pv_pass(vt_ref, s_ref, vrows, m, n_keys):
    acc = None
    for start in range(0, n_keys, KEY_BLOCK):
        p = jnp.exp2(s_ref[start:start + KEY_BLOCK, :] - m).astype(BF16)
        pv = _dot(vt_ref[0, vrows, start:start + KEY_BLOCK], p)
        acc = pv if acc is None else acc + pv
    return acc


def _attn_kernel(q_ref, k_ref, vt_ref, *rest, n_heads, dv, diff, lam_init, first_tile):
    if diff:
        lamp_ref, subg_ref, o_ref, ot_ref, s0_ref, s1_ref = rest
        lp = lamp_ref[...]
        lam = (jnp.exp(jnp.sum(lp[0:1] * lp[1:2], axis=-1, keepdims=True))
               - jnp.exp(jnp.sum(lp[2:3] * lp[3:4], axis=-1, keepdims=True)) + lam_init)
        lane = _iota((TILE, LANES), 1)
    else:
        o_ref, ot_ref, s0_ref, s1_ref = rest
    tile = pl.program_id(1) + first_tile
    dva = dv + V_ONES

    def run(n_keys):
        def step(i, carry):
            if diff:
                hl = pl.ds(pl.multiple_of(i * LANES, LANES), LANES)
                vrows = pl.ds(pl.multiple_of(i * dva, V_ONES), dva)
                q = q_ref[0, :, hl]
                zero = jnp.zeros_like(q)
                m0 = _scores_pass(k_ref, s0_ref, hl, jnp.where(lane < DIFF_HD, q, zero), n_keys)
                m1 = _scores_pass(k_ref, s1_ref, hl, jnp.where(lane < DIFF_HD, zero, q), n_keys)
                a0 = _pv_pass(vt_ref, s0_ref, vrows, m0, n_keys)
                a1 = _pv_pass(vt_ref, s1_ref, vrows, m1, n_keys)
                o = a0[:dv] / a0[dv:dv + 1] - lam * (a1[:dv] / a1[dv:dv + 1])
                ms = jnp.mean(o * o, axis=0, keepdims=True)
                o = o * lax.rsqrt(ms + NORM_EPS) * subg_ref[...] * (1.0 - lam_init)
                ot_ref[pl.ds(pl.multiple_of(i * dv, dv), dv), :] = o
            else:
                hl0 = pl.ds(pl.multiple_of(2 * i * LANES, LANES), LANES)
                hl1 = pl.ds(pl.multiple_of((2 * i + 1) * LANES, LANES), LANES)
                vr0 = pl.ds(pl.multiple_of(2 * i * dva, V_ONES), dva)
                vr1 = pl.ds(pl.multiple_of((2 * i + 1) * dva, V_ONES), dva)
                m0 = _scores_pass(k_ref, s0_ref, hl0, q_ref[0, :, hl0], n_keys)
                m1 = _scores_pass(k_ref, s1_ref, hl1, q_ref[0, :, hl1], n_keys)
                a0 = _pv_pass(vt_ref, s0_ref, vr0, m0, n_keys)
                a1 = _pv_pass(vt_ref, s1_ref, vr1, m1, n_keys)
                o = jnp.concatenate([a0[:dv] / a0[dv:dv + 1], a1[:dv] / a1[dv:dv + 1]], axis=0)
                ot_ref[pl.ds(pl.multiple_of(2 * i * dv, 2 * dv), 2 * dv), :] = o
            return carry

        lax.fori_loop(0, n_heads if diff else n_heads // 2, step, 0)
        o_ref[0] = ot_ref[...].T

    pl.when(tile == 0)(lambda: run(TILE))
    pl.when(tile > 0)(lambda: run(k_ref.shape[1]))


def _attention(q, k, vt, *, n_heads, dv, need_ctx, diff=False, lam_p=None, sub_g=None, lam_init=0.0):
    b, n_tok, w = q.shape
    t = n_tok // TILE
    first = 0 if need_ctx else 1
    ow = n_heads * dv
    in_specs = [pl.BlockSpec((1, TILE, w), lambda i, j: (i, j + first, 0)),
                pl.BlockSpec((1, n_tok, w), lambda i, j: (i, 0, 0)),
                pl.BlockSpec((1, n_heads * (dv + V_ONES), n_tok), lambda i, j: (i, 0, 0))]
    args = [q, k, vt]
    if diff:
        in_specs += [pl.BlockSpec(lam_p.shape, lambda i, j: (0, 0)), pl.BlockSpec(sub_g.shape, lambda i, j: (0, 0))]
        args += [lam_p, sub_g]
    return pl.pallas_call(
        functools.partial(_attn_kernel, n_heads=n_heads, dv=dv, diff=diff, lam_init=lam_init, first_tile=first),
        grid=(b, t - first),
        in_specs=in_specs,
        out_specs=pl.BlockSpec((1, TILE, ow), lambda i, j: (i, j + first, 0)),
        out_shape=jax.ShapeDtypeStruct((b, n_tok, ow), F32),
        scratch_shapes=[pltpu.VMEM((ow, TILE), F32)] + [pltpu.VMEM((n_tok, TILE), F32)] * 2,
        compiler_params=_cparams(("arbitrary", "arbitrary")),
        name="diff_attn" if diff else "mla_attn",
    )(*args)


def _merge_kernel(x_ref, mod_ref, gof_ref, gob_ref, gz_ref, mla_ref, dif_ref, syf_ref, syb_ref, sz_ref,
                  gates01_ref, gates23_ref, gng_ref, sng_ref, wb_ref, wo_ref, o_ref):
    o = gof_ref[0] + gob_ref[0]
    z = gz_ref[0]
    ya = []
    for h in range(GDN_HEADS):
        hl = slice(h * GDN_DV, (h + 1) * GDN_DV)
        ya.append(_rms(o[:, hl], gng_ref[...]) * _silu(z[:, hl]))
    ya = jnp.concatenate(ya, axis=-1)
    y = (syf_ref[0] + syb_ref[0]) * _silu(sz_ref[0])
    gsz = SSM_INNER // SSM_GROUPS
    yd = jnp.concatenate([_rms(y[:, g * gsz:(g + 1) * gsz], sng_ref[:, g * gsz:(g + 1) * gsz])
                          for g in range(SSM_GROUPS)], axis=-1)
    ys = (ya, mla_ref[0], dif_ref[0], yd)
    m = None
    for i in range(N_BRANCH):
        gates_ref = gates01_ref if i < 2 else gates23_ref
        gate = _sigmoid(gates_ref[0, :, (i % 2) * D_MODEL:(i % 2 + 1) * D_MODEL])
        term = gate * _dot(ys[i].astype(BF16), wb_ref[i])
        m = term if m is None else m + term
    o_ref[0] = x_ref[0] + mod_ref[2:3, :] * _dot(m.astype(BF16), wo_ref[...])


def _merge(xs, mods, gdn_o, u, y_mla, y_diff, ssd_y, gng, sng, wb, wo, *, need_ctx):
    b, n_tok, d = xs.shape
    t = n_tok // TILE
    first = 0 if need_ctx else 1
    row = lambda w, cb=0: pl.BlockSpec((1, TILE, w), lambda i, j: (i, j + first, cb))
    const = lambda shape: pl.BlockSpec(shape, lambda i, j: (0,) * len(shape))
    return pl.pallas_call(
        _merge_kernel,
        grid=(b, t - first),
        in_specs=[row(d),
                  pl.BlockSpec((None, None, 6, d), lambda i, j: (i, jnp.minimum(j + first, 1), 0, 0)),
                  row(BRANCH_W), row(BRANCH_W), row(BRANCH_W, (C_GDN + 3 * BRANCH_W) // BRANCH_W),
                  row(BRANCH_W), row(BRANCH_W), row(BRANCH_W), row(BRANCH_W), row(BRANCH_W, C_SSMZ // BRANCH_W),
                  row(GATE_COLS // 2, C_GATE // (GATE_COLS // 2)), row(GATE_COLS // 2, C_GATE // (GATE_COLS // 2) + 1),
                  const((1, GDN_DV)), const((1, SSM_INNER)), const((N_BRANCH, BRANCH_W, d)), const((d, d))],
        out_specs=row(d),
        out_shape=jax.ShapeDtypeStruct((b, n_tok, d), F32),
        input_output_aliases={0: 0},
        compiler_params=_cparams(("arbitrary", "arbitrary")),
        name="merge",
    )(xs, mods, gdn_o[0], gdn_o[1], u, y_mla, y_diff, ssd_y[0], ssd_y[1], u,
      u, u, gng, sng, wb, wo)


def _mlp_kernel(x_ref, mod_ref, g_ref, w1_ref, w2_ref, o_ref):
    x = x_ref[0]
    h = _rms(x, g_ref[...]) * (1.0 + mod_ref[4:5, :]) + mod_ref[3:4, :]
    a = jnp.maximum(_dot(h.astype(BF16), w1_ref[...]), 0.0)
    o_ref[0] = x + mod_ref[5:6, :] * _dot((a * a).astype(BF16), w2_ref[...])


def _mlp(xs, mods, g, w1, w2, *, need_ctx):
    b, n_tok, d = xs.shape
    t = n_tok // TILE
    first = 0 if need_ctx else 1
    const = lambda shape: pl.BlockSpec(shape, lambda i, j: (0,) * len(shape))
    n_out = n_tok - first * TILE
    return pl.pallas_call(
        _mlp_kernel,
        grid=(b, t - first),
        in_specs=[pl.BlockSpec((1, TILE, d), lambda i, j: (i, j + first, 0)),
                  pl.BlockSpec((None, None, 6, d), lambda i, j: (i, jnp.minimum(j + first, 1), 0, 0)),
                  const((1, d)), const((d, D_FF)), const((D_FF, d))],
        out_specs=pl.BlockSpec((1, TILE, d), lambda i, j: (i, j, 0)),
        out_shape=jax.ShapeDtypeStruct((b, n_out, d), F32),
        compiler_params=_cparams(("arbitrary", "arbitrary")),
        name="mlp",
    )(xs, mods, g, w1, w2)


def _repack_w_in(w):
    d = w.shape[0]
    z = lambda n: jnp.zeros((d, n), w.dtype)
    o_mla = GDN_COLS
    o_diff = o_mla + MLA_COLS
    o_ssm = o_diff + DIFF_COLS
    o_gate = MIX_COLS
    gdn_ab = w[:, 2 * GDN_QK + 2 * GDN_VW:GDN_COLS]
    ssm_dt = w[:, o_ssm + 2 * SSM_INNER + 2 * SSM_BC:o_ssm + SSM_COLS]
    kpe = w[:, o_mla + MLA_Q_LORA + MLA_KV_LORA:o_mla + MLA_COLS]
    parts = [
        w[:, 0:2 * GDN_QK + 2 * GDN_VW],
        w[:, o_ssm + SSM_INNER:o_ssm + 2 * SSM_INNER + 2 * SSM_BC],
        w[:, o_diff:o_diff + DIFF_COLS],
        w[:, o_mla:o_mla + MLA_Q_LORA + MLA_KV_LORA], z(MLA_NOPE), kpe, z(LANES - MLA_QK),
        gdn_ab, ssm_dt, z(LANES - 4 * GDN_HEADS - 2 * SSM_HEADS),
        z(C_SSMZ - C_SMALL - LANES),
        w[:, o_ssm:o_ssm + SSM_INNER],
        w[:, o_gate:o_gate + GATE_COLS],
    ]
    out = jnp.concatenate(parts, axis=1)
    assert out.shape[1] == N_IN
    return out.astype(BF16)


def _lane_row(vals, offset):
    row = jnp.zeros((1, LANES), F32)
    return row.at[0, offset:offset + vals.shape[0]].set(vals.astype(F32))


def _rope_tables(n_lat, rot_dim, lane_offset, period):
    rows = n_lat // GRID_W
    row = jnp.repeat(jnp.arange(rows, dtype=F32), GRID_W)
    col = jnp.tile(jnp.arange(GRID_W, dtype=F32), rows)
    quarter = rot_dim // 4
    inv = ROPE_THETA ** (-jnp.arange(quarter, dtype=F32) / quarter)
    ar = row[:, None] * inv
    ac = col[:, None] * inv
    ang = jnp.concatenate([ar, ar, ac, ac], axis=-1)
    cos, sin = jnp.cos(ang), jnp.sin(ang)
    first = (np.arange(rot_dim) % (2 * quarter)) < quarter
    sin_a = jnp.where(first, -sin, 0.0)
    sin_b = jnp.where(first, 0.0, sin)
    reps = LANES // period
    def place(t, fill):
        blk = jnp.full((n_lat, period), fill, F32).at[:, lane_offset:lane_offset + rot_dim].set(t)
        blk = jnp.tile(blk, (1, reps))
        ctx = jnp.full((CTX_LEN, LANES), fill, F32)
        return jnp.concatenate([ctx, blk], axis=0)
    return jnp.stack([place(cos, 1.0), place(sin_a, 0.0), place(sin_b, 0.0)])


def kernel(x, c, ctx, c_ctx, ada_w, ada_b, norm1_g, norm2_g, w_in, gdn_conv, gdn_a_log, gdn_dt_bias, gdn_norm_g,
           mla_q_lora_g, mla_kv_lora_g, mla_w_uq, mla_w_ukv, mla_qn_g, mla_kn_g, diff_qn_g, diff_kn_g,
           diff_lambda, diff_sub_g, ssm_conv, ssm_conv_b, ssm_a_log, ssm_dt_bias, ssm_d, ssm_norm_g, w_branch,
           w_out, mlp_w1, mlp_w2):
    b, n_lat, d = x.shape
    depth = w_in.shape[0]
    assert ctx.shape[1] == CTX_LEN == TILE and n_lat % TILE == 0 and d == D_MODEL
    xs = jnp.concatenate([ctx, x], axis=1)
    tab_mla = _rope_tables(n_lat, MLA_ROPE, MLA_NOPE, LANES)
    tab_diff = _rope_tables(n_lat, DIFF_HD, 0, DIFF_HD)
    rows = b + 8
    cc = jnp.zeros((rows, d), F32).at[:b].set(c).at[b].set(c_ctx)
    row2 = lambda v: v.reshape(1, -1).astype(F32)

    for l in range(depth):
        need_ctx = l < depth - 1
        lam_init = LAMBDA_BASE - LAMBDA_AMP * math.exp(-LAMBDA_RATE * l)
        mod = _ada(cc, ada_w[l], ada_b[l].reshape(1, -1))
        mod_lat = mod[:b].reshape(b, 6, d)
        mod_ctx = jnp.broadcast_to(mod[b].reshape(1, 6, d), (b, 6, d))
        mods = jnp.stack([mod_ctx, mod_lat], axis=1)

        u = _inproj(xs, mods, row2(norm1_g[l]), _repack_w_in(w_in[l]))

        gdn_o = _gdn_scan(_gdn_prep(u, gdn_conv[l].astype(F32),
                                    _lane_row(gdn_a_log[l].reshape(-1), SM_A),
                                    _lane_row(gdn_dt_bias[l].reshape(-1), SM_A)))

        wq = mla_w_uq[l].reshape(MLA_Q_LORA, MLA_HEADS, MLA_QK)
        wq = jnp.pad(wq, ((0, 0), (0, 0), (0, LANES - MLA_QK))).reshape(MLA_Q_LORA, MLA_HEADS * LANES)
        wkv = mla_w_ukv[l].reshape(MLA_KV_LORA, MLA_HEADS, MLA_NOPE + MLA_V)
        wk = jnp.pad(wkv[:, :, :MLA_NOPE], ((0, 0), (0, 0), (0, LANES - MLA_NOPE))).reshape(MLA_KV_LORA, MLA_HEADS * LANES)
        wvt = wkv[:, :, MLA_NOPE:].reshape(MLA_KV_LORA, MLA_HEADS * MLA_V).T
        pad_g = lambda g: jnp.pad(g.astype(F32), (0, LANES - MLA_QK)).reshape(1, LANES)
        q_m, k_m, vt_m = _mla_prep(u, tab_mla, row2(mla_q_lora_g[l]), row2(mla_kv_lora_g[l]),
                                   wq.astype(BF16), wk.astype(BF16), wvt.astype(BF16),
                                   pad_g(mla_qn_g[l]), pad_g(mla_kn_g[l]))
        y_mla = _attention(q_m, k_m, vt_m, n_heads=MLA_HEADS, dv=MLA_V, need_ctx=need_ctx)

        rep_g = lambda g: jnp.tile(g.astype(F32), LANES // DIFF_HD).reshape(1, LANES)
        q_d, k_d, vt_d = _diff_prep(u, tab_diff, rep_g(diff_qn_g[l]), rep_g(diff_kn_g[l]))
        y_diff = _attention(q_d, k_d, vt_d, n_heads=DIFF_HEADS, dv=2 * DIFF_HD, need_ctx=need_ctx, diff=True,
                            lam_p=diff_lambda[l].astype(F32), sub_g=diff_sub_g[l].astype(F32).reshape(-1, 1),
                            lam_init=lam_init)

        ssd_y = _ssd(u, ssm_conv[l].astype(F32), row2(ssm_conv_b[l]),
                     _lane_row(ssm_a_log[l].reshape(-1), SM_DT), _lane_row(ssm_dt_bias[l].reshape(-1), SM_DT),
                     jnp.repeat(ssm_d[l].astype(F32), SSM_HEAD_DIM).reshape(1, -1))

        xs = _merge(xs, mods, gdn_o, u, y_mla, y_diff, ssd_y, row2(gdn_norm_g[l]), row2(ssm_norm_g[l]),
                    w_branch[l].astype(BF16), w_out[l].astype(BF16), need_ctx=need_ctx)
        xs = _mlp(xs, mods, row2(norm2_g[l]), mlp_w1[l].astype(BF16), mlp_w2[l].astype(BF16), need_ctx=need_ctx)
    return xs
```

```python
import functools
import math

import jax
import jax.numpy as jnp
import numpy as np
from jax import lax
from jax.experimental import pallas as pl
from jax.experimental.pallas import tpu as pltpu

F32 = jnp.float32
BF16 = jnp.bfloat16

D_MODEL = 1024
CTX_LEN = 256
GRID_W = 64
ROPE_THETA = 10000.0
NORM_EPS = 1e-6
CONV_K = 5
N_BRANCH = 4
D_FF = 4 * D_MODEL
GDN_HEADS = 4
GDN_DK = 128
GDN_DV = 128
MLA_HEADS = 8
MLA_NOPE = 64
MLA_ROPE = 32
MLA_V = 64
MLA_QK = MLA_NOPE + MLA_ROPE
MLA_Q_LORA = 384
MLA_KV_LORA = 256
DIFF_HEADS = 4
DIFF_HD = 64
LAMBDA_BASE = 0.8
LAMBDA_AMP = 0.6
LAMBDA_RATE = 0.3
SSM_HEADS = 8
SSM_HEAD_DIM = 64
SSM_GROUPS = 2
SSM_STATE = 128
GDN_QK = GDN_HEADS * GDN_DK
GDN_VW = GDN_HEADS * GDN_DV
DIFF_QK = DIFF_HEADS * 2 * DIFF_HD
DIFF_VW = DIFF_HEADS * 2 * DIFF_HD
SSM_INNER = SSM_HEADS * SSM_HEAD_DIM
SSM_BC = SSM_GROUPS * SSM_STATE
BRANCH_W = 512
GDN_COLS = 2 * GDN_QK + 2 * GDN_VW + 4 * GDN_HEADS
MLA_COLS = MLA_Q_LORA + MLA_KV_LORA + MLA_ROPE
DIFF_COLS = 2 * DIFF_QK + DIFF_VW
SSM_COLS = 2 * SSM_INNER + 2 * SSM_BC + 2 * SSM_HEADS
MIX_COLS = GDN_COLS + MLA_COLS + DIFF_COLS + SSM_COLS
GATE_COLS = N_BRANCH * D_MODEL

LANES = 128
TILE = 256
CHUNK = 64
CPT = TILE // CHUNK
HALO = 8
VMEM_LIMIT = 56 * 1024 * 1024
V_ONES = 16
KEY_BLOCK = 256
LOG2E = 1.4426950408889634

C_GDN = 0
C_XBC = 2048
C_DIFF = 3072
C_MLA = 4608
C_SMALL = 5376
C_SSMZ = 5632
C_GATE = 6144
N_IN = 10240
IN_TN = 2560

SM_A = 0
SM_B = 8
SM_DT = 16


def _cparams(sem):
    return pltpu.CompilerParams(dimension_semantics=sem, vmem_limit_bytes=VMEM_LIMIT)


def _dot(a, b):
    return jnp.dot(a, b, preferred_element_type=F32)


def _dot_nt(a, b):
    return lax.dot_general(a, b, (((1,), (1,)), ((), ())), preferred_element_type=F32)


def _dot_tn(a, b):
    return lax.dot_general(a, b, (((0,), (0,)), ((), ())), preferred_element_type=F32)


def _split3(x):
    x1 = x.astype(BF16)
    r1 = x - x1.astype(F32)
    x2 = r1.astype(BF16)
    r2 = r1 - x2.astype(F32)
    return x1, x2, r2.astype(BF16)


def _dot_exact_lhs(a_bf, x):
    x1, x2, x3 = _split3(x)
    return _dot(a_bf, x1) + _dot(a_bf, x2) + _dot(a_bf, x3)


def _dot_exact_rhs(x, b_bf):
    x1, x2, x3 = _split3(x)
    return _dot(x1, b_bf) + _dot(x2, b_bf) + _dot(x3, b_bf)


def _sigmoid(x):
    return 1.0 / (1.0 + jnp.exp(-x))


def _silu(x):
    return x * _sigmoid(x)


def _softplus(x):
    return jnp.maximum(x, 0.0) + jnp.log1p(jnp.exp(-jnp.abs(x)))


def _iota(shape, dim):
    return lax.broadcasted_iota(jnp.int32, shape, dim)


def _chunk_tri(n, upper):
    r = _iota((n, n), 0)
    c = _iota((n, n), 1)
    same = (r // CHUNK) == (c // CHUNK)
    tri = (r <= c) if upper else (r >= c)
    return jnp.where(same & tri, 1.0, 0.0).astype(BF16)


def _expander(n_rows, first_row, n_groups, width):
    r = _iota((n_rows, n_groups * width), 0)
    c = _iota((n_rows, n_groups * width), 1)
    return jnp.where(r == first_row + c // width, 1.0, 0.0).astype(BF16)


def _ada_kernel(c_ref, w_ref, b_ref, o_ref):
    a = _silu(c_ref[...])
    a1, a2, a3 = _split3(a)
    w1, w2, w3 = _split3(w_ref[...])
    acc = _dot(a1, w1) + (_dot(a1, w2) + _dot(a2, w1)) + (_dot(a1, w3) + _dot(a2, w2) + _dot(a3, w1))
    o_ref[...] = acc + b_ref[...]


def _ada(cc, w, b):
    rows, d = cc.shape
    n = w.shape[1]
    tn = 1536
    return pl.pallas_call(
        _ada_kernel,
        grid=(n // tn,),
        in_specs=[pl.BlockSpec((rows, d), lambda j: (0, 0)),
                  pl.BlockSpec((d, tn), lambda j: (0, j)),
                  pl.BlockSpec((1, tn), lambda j: (0, j))],
        out_specs=pl.BlockSpec((rows, tn), lambda j: (0, j)),
        out_shape=jax.ShapeDtypeStruct((rows, n), F32),
        compiler_params=_cparams(("arbitrary",)),
        name="ada",
    )(cc, w, b)


def _rms(x, g):
    ms = jnp.mean(x * x, axis=-1, keepdims=True)
    return x * lax.rsqrt(ms + NORM_EPS) * g


def _inproj_kernel(x_ref, mod_ref, g_ref, w_ref, o_ref):
    h = _rms(x_ref[0], g_ref[...])
    hb = (h * (1.0 + mod_ref[1:2, :]) + mod_ref[0:1, :]).astype(BF16)
    for c in range(0, N_IN, IN_TN):
        o_ref[0, :, c:c + IN_TN] = _dot(hb, w_ref[:, c:c + IN_TN])


def _inproj(xs, mods, g, w):
    b, n_tok, d = xs.shape
    t = n_tok // TILE
    return pl.pallas_call(
        _inproj_kernel,
        grid=(b, t),
        in_specs=[pl.BlockSpec((1, TILE, d), lambda i, j: (i, j, 0)),
                  pl.BlockSpec((None, None, 6, d), lambda i, j: (i, jnp.minimum(j, 1), 0, 0)),
                  pl.BlockSpec((1, d), lambda i, j: (0, 0)),
                  pl.BlockSpec((d, N_IN), lambda i, j: (0, 0), pipeline_mode=pl.Buffered(1))],
        out_specs=pl.BlockSpec((1, TILE, N_IN), lambda i, j: (i, j, 0)),
        out_shape=jax.ShapeDtypeStruct((b, n_tok, N_IN), F32),
        compiler_params=_cparams(("arbitrary", "arbitrary")),
        name="inproj",
    )(xs, mods, g, w)


def _halo_specs(width, col_block, n_tiles, tile_of):
    rpt = TILE // HALO
    last = n_tiles * rpt - 1
    main = pl.BlockSpec((1, TILE, width), lambda *ids: (ids[0], tile_of(*ids), col_block))
    prev = pl.BlockSpec((1, HALO, width),
                        lambda *ids: (ids[0], jnp.maximum(tile_of(*ids) * rpt - 1, 0), col_block))
    nxt = pl.BlockSpec((1, HALO, width),
                       lambda *ids: (ids[0], jnp.minimum(tile_of(*ids) * rpt + rpt, last), col_block))
    return main, prev, nxt


def _conv_tile(main_ref, prev_ref, next_ref, w_ref, ext_ref, tile, n_tiles):
    prev_ok = (tile >= 2).astype(F32)
    next_ok = jnp.logical_and(tile >= 1, tile < n_tiles - 1).astype(F32)
    ext_ref[0:HALO, :] = prev_ref[0] * prev_ok
    ext_ref[HALO:HALO + TILE, :] = main_ref[0]
    ext_ref[HALO + TILE:2 * HALO + TILE, :] = next_ref[0] * next_ok
    half = CONV_K // 2
    acc = None
    for k in range(CONV_K):
        term = w_ref[k:k + 1, :] * ext_ref[HALO - half + k:HALO - half + k + TILE, :]
        acc = term if acc is None else acc + term
    return acc


def _block_inverse_many(mats):
    n = mats[0].shape[0]
    ri = _iota((n, n), 0)
    ci = _iota((n, n), 1)
    eye = jnp.where(ri == ci, 1.0, 0.0).astype(F32)
    ts = [eye - jnp.where((ri // 2) == (ci // 2), a, 0.0) for a in mats]
    abs_ = [a.astype(BF16) for a in mats]
    zero = jnp.zeros((n, n), BF16)
    size = 2
    while size < CHUNK:
        off = ((ri // (2 * size)) == (ci // (2 * size))) & ((ri // size) != (ci // size))
        tbs = [t.astype(BF16) for t in ts]
        xs = [_dot(jnp.where(off, ab, zero), tb).astype(BF16) for ab, tb in zip(abs_, tbs)]
        ts = [t - _dot(tb, x) for t, tb, x in zip(ts, tbs, xs)]
        size *= 2
    return ts


def _gdn_prep_kernel(main_ref, prev_ref, next_ref, sm_ref, cw_ref, alog_ref, dtb_ref,
                     u0_ref, wq0_ref, kd0_ref, at0_ref, gl0_ref,
                     u1_ref, wq1_ref, kd1_ref, at1_ref, gl1_ref,
                     ext_ref, *, n_tiles):
    tile = pl.program_id(1)
    qkv = _silu(_conv_tile(main_ref, prev_ref, next_ref, cw_ref, ext_ref, tile, n_tiles))
    sm = sm_ref[0]
    g_all = -jnp.exp(alog_ref[...]) * _softplus(sm + dtb_ref[...])
    beta_all = _sigmoid(sm)
    nh = GDN_HEADS
    exp_g = _expander(LANES, SM_A, 2 * nh, LANES)
    exp_b = _expander(LANES, SM_B, 2 * nh, LANES)
    beta_x = _dot_exact_rhs(beta_all, exp_b)
    outs = ((u0_ref, wq0_ref, kd0_ref, at0_ref, gl0_ref), (u1_ref, wq1_ref, kd1_ref, at1_ref, gl1_ref))
    ri = _iota((TILE, TILE), 0)
    ci = _iota((TILE, TILE), 1)
    same = (ri // CHUNK) == (ci // CHUNK)
    cums, masks = [], []
    for d in range(2):
        cum = _dot_exact_lhs(_chunk_tri(TILE, upper=(d == 1)), g_all)
        cums.append((_dot_exact_rhs(cum, exp_g), cum.T))
        masks.append((same & ((ri >= ci) if d == 0 else (ri <= ci)), same & ((ri > ci) if d == 0 else (ri < ci))))

    chains, a_mats = [], []
    for h in range(nh):
        q = qkv[:, h * GDN_DK:(h + 1) * GDN_DK]
        k = qkv[:, GDN_QK + h * GDN_DK:GDN_QK + (h + 1) * GDN_DK]
        v = qkv[:, 2 * GDN_QK + h * GDN_DV:2 * GDN_QK + (h + 1) * GDN_DV]
        q = q * lax.rsqrt(jnp.sum(q * q, axis=-1, keepdims=True) + 1e-6) * (GDN_DK ** -0.5)
        k = k * lax.rsqrt(jnp.sum(k * k, axis=-1, keepdims=True) + 1e-6)
        kb = k.astype(BF16)
        kq = _dot_nt(jnp.concatenate([kb, q.astype(BF16)], axis=0), kb)
        for d in range(2):
            col = d * nh + h
            lanes = slice(col * LANES, (col + 1) * LANES)
            cum_x, cum_t = cums[d]
            incl, strict = masks[d]
            g_col = cum_x[:, lanes]
            b_col = beta_x[:, lanes]
            diff = jnp.concatenate([g_col, g_col], axis=1) - cum_t[SM_A + col:SM_A + col + 1, :]
            decay = jnp.where(incl, jnp.exp(jnp.where(incl, diff, 0.0)), 0.0)
            a_mats.append(jnp.where(strict, kq[:TILE] * jnp.concatenate([b_col, b_col], axis=1) * decay, 0.0))
            e_g = jnp.exp(g_col)
            rhs = jnp.concatenate([v * b_col, k * b_col * e_g], axis=1).astype(BF16)
            chains.append((h, d, q, k, g_col, e_g, rhs, kq[TILE:] * decay))

    t_mats = _block_inverse_many(a_mats)
    for (h, d, q, k, g_col, e_g, rhs, attn), t_mat in zip(chains, t_mats):
        uw = _dot(t_mat.astype(BF16), rhs)
        qg = (q * e_g).astype(BF16)
        u_ref, wq_ref, kd_ref, at_ref, gl_ref = outs[d]
        hl = slice(h * LANES, (h + 1) * LANES)
        for c in range(CPT):
            rows = slice(c * CHUNK, (c + 1) * CHUNK)
            last = c * CHUNK + (CHUNK - 1 if d == 0 else 0)
            g_last = g_col[last:last + 1, :]
            slot = c if d == 0 else CPT - 1 - c
            u_ref[0, 0, slot, :, hl] = uw[rows, :GDN_DV]
            wq_ref[0, 0, slot, 0:CHUNK, hl] = uw[rows, GDN_DV:].astype(BF16)
            wq_ref[0, 0, slot, CHUNK:2 * CHUNK, hl] = qg[rows]
            kd_ref[0, 0, slot, :, hl] = (k[rows] * jnp.exp(g_last - g_col[rows])).astype(BF16)
            at_ref[0, 0, slot, :, h * CHUNK:(h + 1) * CHUNK] = attn[rows, rows].astype(BF16)
            gl_ref[0, 0, slot, :, hl] = jnp.exp(g_last)


def _bwd_block(j, n_tiles):
    return jnp.where(j == 0, 0, n_tiles - j)


def _gdn_prep(u, conv_w, alog_row, dtb_row):
    b, n_tok, _ = u.shape
    t = n_tok // TILE
    w = GDN_VW
    cw = 2 * GDN_QK + GDN_VW
    main, prev, nxt = _halo_specs(cw, 0, t, lambda i, j: j)
    fwd = lambda i, j: (i, j, 0, 0, 0)
    bwd = lambda i, j: (i, _bwd_block(j, t), 0, 0, 0)

    def outs(imap):
        return [pl.BlockSpec((1, 1, CPT, CHUNK, w), imap),
                pl.BlockSpec((1, 1, CPT, 2 * CHUNK, w), imap),
                pl.BlockSpec((1, 1, CPT, CHUNK, w), imap),
                pl.BlockSpec((1, 1, CPT, CHUNK, GDN_HEADS * CHUNK), imap),
                pl.BlockSpec((1, 1, CPT, 1, w), imap)]

    shapes = [jax.ShapeDtypeStruct((b, t, CPT, CHUNK, w), F32),
              jax.ShapeDtypeStruct((b, t, CPT, 2 * CHUNK, w), BF16),
              jax.ShapeDtypeStruct((b, t, CPT, CHUNK, w), BF16),
              jax.ShapeDtypeStruct((b, t, CPT, CHUNK, GDN_HEADS * CHUNK), BF16),
              jax.ShapeDtypeStruct((b, t, CPT, 1, w), F32)]
    return pl.pallas_call(
        functools.partial(_gdn_prep_kernel, n_tiles=t),
        grid=(b, t),
        in_specs=[main, prev, nxt,
                  pl.BlockSpec((1, TILE, LANES), lambda i, j: (i, j, C_SMALL // LANES)),
                  pl.BlockSpec((CONV_K, cw), lambda i, j: (0, 0)),
                  pl.BlockSpec((1, LANES), lambda i, j: (0, 0)),
                  pl.BlockSpec((1, LANES), lambda i, j: (0, 0))],
        out_specs=outs(fwd) + outs(bwd),
        out_shape=shapes + shapes,
        scratch_shapes=[pltpu.VMEM((TILE + 2 * HALO, cw), F32)],
        compiler_params=_cparams(("arbitrary", "arbitrary")),
        name="gdn_prep",
    )(u, u, u, u, conv_w, alog_row, dtb_row)


def _gdn_scan_kernel(u0_ref, wq0_ref, kd0_ref, at0_ref, gl0_ref,
                     u1_ref, wq1_ref, kd1_ref, at1_ref, gl1_ref,
                     of_ref, ob_ref, s_ref):
    @pl.when(pl.program_id(1) == 0)
    def _():
        s_ref[...] = jnp.zeros_like(s_ref)

    ins = ((u0_ref, wq0_ref, kd0_ref, at0_ref, gl0_ref), (u1_ref, wq1_ref, kd1_ref, at1_ref, gl1_ref))
    chains = [(d, h) for d in range(2) for h in range(GDN_HEADS)]
    hls = [slice(h * LANES, (h + 1) * LANES) for _, h in chains]
    states = [s_ref[d, h] for d, h in chains]
    for slot in range(CPT):
        rs = [_dot(ins[d][1][0, 0, slot, :, hl], s.astype(BF16)) for (d, _), hl, s in zip(chains, hls, states)]
        vbs = [(ins[d][0][0, 0, slot, :, hl] - r[0:CHUNK]).astype(BF16) for (d, _), hl, r in zip(chains, hls, rs)]
        outs = [r[CHUNK:2 * CHUNK] + _dot(ins[d][3][0, 0, slot, :, h * CHUNK:(h + 1) * CHUNK], vb)
                for (d, h), r, vb in zip(chains, rs, vbs)]
        states = [s * ins[d][4][0, 0, slot, :, hl] + _dot_tn(ins[d][2][0, 0, slot, :, hl], vb)
                  for (d, _), hl, s, vb in zip(chains, hls, states, vbs)]
        for (d, _), hl, o in zip(chains, hls, outs):
            c = slot if d == 0 else CPT - 1 - slot
            (of_ref if d == 0 else ob_ref)[0, c * CHUNK:(c + 1) * CHUNK, hl] = o
    for (d, h), s in zip(chains, states):
        s_ref[d, h] = s


def _gdn_scan(prep):
    b, t = prep[0].shape[:2]
    w = GDN_VW
    imap = lambda i, j: (i, j, 0, 0, 0)
    specs = [pl.BlockSpec((1, 1, CPT, CHUNK, w), imap),
             pl.BlockSpec((1, 1, CPT, 2 * CHUNK, w), imap),
             pl.BlockSpec((1, 1, CPT, CHUNK, w), imap),
             pl.BlockSpec((1, 1, CPT, CHUNK, GDN_HEADS * CHUNK), imap),
             pl.BlockSpec((1, 1, CPT, 1, w), imap)]
    return pl.pallas_call(
        _gdn_scan_kernel,
        grid=(b, t),
        in_specs=specs + specs,
        out_specs=[pl.BlockSpec((1, TILE, w), lambda i, j: (i, j, 0)),
                   pl.BlockSpec((1, TILE, w), lambda i, j: (i, _bwd_block(j, t), 0))],
        out_shape=[jax.ShapeDtypeStruct((b, t * TILE, w), F32)] * 2,
        scratch_shapes=[pltpu.VMEM((2, GDN_HEADS, GDN_DK, GDN_DV), F32)],
        compiler_params=_cparams(("arbitrary", "arbitrary")),
        name="gdn_scan",
    )(*prep)


def _ssd_kernel(mf_ref, pf_ref, nf_ref, smf_ref, mb_ref, pb_ref, nb_ref, smb_ref,
                cw_ref, cb_ref, alog_ref, dtb_ref, dskip_ref,
                yf_ref, yb_ref, ext_ref, st_ref, *, n_tiles):
    j = pl.program_id(1)

    @pl.when(j == 0)
    def _():
        st_ref[...] = jnp.zeros_like(st_ref)

    nh, hd, ng = SSM_HEADS, SSM_HEAD_DIM, SSM_GROUPS
    hpg = nh // ng
    gw = hpg * hd
    ri = _iota((CHUNK, CHUNK), 0)
    ci = _iota((CHUNK, CHUNK), 1)
    dirs = ((mf_ref, pf_ref, nf_ref, smf_ref, yf_ref, j), (mb_ref, pb_ref, nb_ref, smb_ref, yb_ref, _bwd_block(j, n_tiles)))
    for d in range(2):
        m_ref, p_ref, n_ref, sm_ref, y_ref, tile = dirs[d]
        xbc = _silu(_conv_tile(m_ref, p_ref, n_ref, cw_ref, ext_ref, tile, n_tiles) + cb_ref[...])
        xs = xbc[:, :SSM_INNER]
        dt_all = _softplus(sm_ref[0] + dtb_ref[...])
        da_all = dt_all * (-jnp.exp(alog_ref[...]))
        cum = _dot_exact_lhs(_chunk_tri(TILE, upper=(d == 1)), da_all)
        cum_t = cum.T
        expand = _expander(LANES, SM_DT + d * nh, nh, hd)
        dt_x = _dot_exact_rhs(dt_all, expand)
        cum_x = _dot_exact_rhs(cum, expand)
        xdt = xs * dt_x
        incl = (ri >= ci) if d == 0 else (ri <= ci)
        for step in range(CPT):
            c = step if d == 0 else CPT - 1 - step
            rows = slice(c * CHUNK, (c + 1) * CHUNK)
            last = c * CHUNK + (CHUNK - 1 if d == 0 else 0)
            cum_c = cum_x[rows]
            cum_last = cum_x[last:last + 1]
            decay_states = jnp.exp(cum_last - cum_c)
            in_decay = jnp.exp(cum_c)
            chunk_decay = jnp.exp(cum_last)
            for g in range(ng):
                gl = slice(g * gw, (g + 1) * gw)
                bm = xbc[rows, SSM_INNER + g * SSM_STATE:SSM_INNER + (g + 1) * SSM_STATE].astype(BF16)
                cm = xbc[rows, SSM_INNER + SSM_BC + g * SSM_STATE:SSM_INNER + SSM_BC + (g + 1) * SSM_STATE].astype(BF16)
                scores = _dot_nt(cm, bm)
                state = st_ref[d, g]
                y_off = _dot(cm, state.astype(BF16)) * in_decay[:, gl]
                new_states = _dot_tn(bm, (xdt[rows, gl] * decay_states[:, gl]).astype(BF16))
                st_ref[d, g] = state * chunk_decay[:, gl] + new_states
                for hh in range(hpg):
                    h = g * hpg + hh
                    hl = slice(h * hd, (h + 1) * hd)
                    col = SM_DT + d * nh + h
                    seg = cum_c[:, hl] - cum_t[col:col + 1, rows]
                    lmat = jnp.where(incl, jnp.exp(jnp.where(incl, seg, 0.0)), 0.0)
                    y = _dot((scores * lmat).astype(BF16), xdt[rows, hl].astype(BF16)) + y_off[:, hh * hd:(hh + 1) * hd]
                    if d == 0:
                        y = y + dskip_ref[:, hl] * xs[rows, hl]
                    y_ref[0, rows, hl] = y


def _ssd(u, conv_w, conv_b, alog_row, dtb_row, dskip_row):
    b, n_tok, _ = u.shape
    t = n_tok // TILE
    cw = SSM_INNER + 2 * SSM_BC
    cb = C_XBC // cw
    mf, pf, nf = _halo_specs(cw, cb, t, lambda i, j: j)
    mb, pb, nb = _halo_specs(cw, cb, t, lambda i, j: _bwd_block(j, t))
    smf = pl.BlockSpec((1, TILE, LANES), lambda i, j: (i, j, C_SMALL // LANES))
    smb = pl.BlockSpec((1, TILE, LANES), lambda i, j: (i, _bwd_block(j, t), C_SMALL // LANES))
    const = lambda r, c: pl.BlockSpec((r, c), lambda i, j: (0, 0))
    return pl.pallas_call(
        functools.partial(_ssd_kernel, n_tiles=t),
        grid=(b, t),
        in_specs=[mf, pf, nf, smf, mb, pb, nb, smb,
                  const(CONV_K, cw), const(1, cw), const(1, LANES), const(1, LANES), const(1, SSM_INNER)],
        out_specs=[pl.BlockSpec((1, TILE, SSM_INNER), lambda i, j: (i, j, 0)),
                   pl.BlockSpec((1, TILE, SSM_INNER), lambda i, j: (i, _bwd_block(j, t), 0))],
        out_shape=[jax.ShapeDtypeStruct((b, n_tok, SSM_INNER), F32)] * 2,
        scratch_shapes=[pltpu.VMEM((TILE + 2 * HALO, cw), F32),
                        pltpu.VMEM((2, SSM_GROUPS, SSM_STATE, SSM_INNER // SSM_GROUPS), F32)],
        compiler_params=_cparams(("arbitrary", "arbitrary")),
        name="ssd",
    )(u, u, u, u, u, u, u, u, conv_w, conv_b, alog_row, dtb_row, dskip_row)


PAIR = 2 * LANES


def _group_ones(group):
    r = _iota((PAIR, PAIR), 0)
    c = _iota((PAIR, PAIR), 1)
    return jnp.where((r // group) == (c // group), 1.0, 0.0).astype(BF16)


def _rope_perm(period, lane_offset, rot_dim):
    r = _iota((PAIR, PAIR), 0)
    c = _iota((PAIR, PAIR), 1)
    q = rot_dim // 4
    cl = c % period - lane_offset
    src = jnp.where((cl % (2 * q)) < q, c + q, c - q)
    return jnp.where((cl >= 0) & (cl < rot_dim) & (r == src), 1.0, 0.0).astype(BF16)


def _norm_rope(x, gain, ones_bd, perm, tab_ref, n_real):
    ss = _dot((x * x).astype(BF16), ones_bd)
    xn = x * lax.rsqrt(ss / n_real + NORM_EPS) * gain
    hi = xn.astype(BF16)
    lo = (xn - hi.astype(F32)).astype(BF16)
    rot = _dot(hi, perm) + _dot(lo, perm)
    cos = jnp.concatenate([tab_ref[0]] * 2, axis=1)
    sin = jnp.concatenate([tab_ref[1] + tab_ref[2]] * 2, axis=1)
    return xn * cos + rot * sin


def _store_vt(vt_ref, vt, n_heads, dv):
    ones = jnp.ones((V_ONES, vt.shape[1]), BF16)
    for h in range(n_heads):
        base = h * (dv + V_ONES)
        vt_ref[0, base:base + dv, :] = vt[h * dv:(h + 1) * dv].astype(BF16)
        vt_ref[0, base + dv:base + dv + V_ONES, :] = ones


def _mla_prep_kernel(u_ref, tab_ref, qlg_ref, kvlg_ref, wq_ref, wk_ref, wvt_ref, qng_ref, kng_ref,
                     q_ref, k_ref, vt_ref):
    u = u_ref[0]
    cq = _rms(u[:, :MLA_Q_LORA], qlg_ref[...]).astype(BF16)
    ckv = _rms(u[:, MLA_Q_LORA:MLA_Q_LORA + MLA_KV_LORA], kvlg_ref[...]).astype(BF16)
    kpe = u[:, MLA_Q_LORA + MLA_KV_LORA:]
    q_all = _dot(cq, wq_ref[...])
    k_all = _dot(ckv, wk_ref[...])
    _store_vt(vt_ref, _dot_nt(wvt_ref[...], ckv), MLA_HEADS, MLA_V)
    scale = MLA_QK ** -0.5 * LOG2E
    ones_bd = _group_ones(LANES)
    perm = _rope_perm(LANES, MLA_NOPE, MLA_ROPE)
    qg = jnp.concatenate([qng_ref[...]] * 2, axis=1)
    kg = jnp.concatenate([kng_ref[...]] * 2, axis=1)
    kpe2 = jnp.concatenate([kpe, kpe], axis=1)
    for h in range(0, MLA_HEADS, 2):
        hl = slice(h * LANES, (h + 2) * LANES)
        q_ref[0, :, hl] = (_norm_rope(q_all[:, hl], qg, ones_bd, perm, tab_ref, MLA_QK) * scale).astype(BF16)
        k_ref[0, :, hl] = _norm_rope(k_all[:, hl] + kpe2, kg, ones_bd, perm, tab_ref, MLA_QK).astype(BF16)


def _mla_prep(u, tab, qlg, kvlg, wq, wk, wvt, qng, kng):
    b, n_tok, _ = u.shape
    t = n_tok // TILE
    w = MLA_HEADS * LANES
    cw = MLA_Q_LORA + MLA_KV_LORA + LANES
    const = lambda r, c: pl.BlockSpec((r, c), lambda i, j: (0, 0))
    return pl.pallas_call(
        _mla_prep_kernel,
        grid=(b, t),
        in_specs=[pl.BlockSpec((1, TILE, cw), lambda i, j: (i, j, C_MLA // cw)),
                  pl.BlockSpec((3, TILE, LANES), lambda i, j: (0, j, 0)),
                  const(1, MLA_Q_LORA), const(1, MLA_KV_LORA), const(MLA_Q_LORA, w), const(MLA_KV_LORA, w),
                  const(MLA_HEADS * MLA_V, MLA_KV_LORA), const(1, LANES), const(1, LANES)],
        out_specs=[pl.BlockSpec((1, TILE, w), lambda i, j: (i, j, 0)),
                   pl.BlockSpec((1, TILE, w), lambda i, j: (i, j, 0)),
                   pl.BlockSpec((1, MLA_HEADS * (MLA_V + V_ONES), TILE), lambda i, j: (i, 0, j))],
        out_shape=[jax.ShapeDtypeStruct((b, n_tok, w), BF16),
                   jax.ShapeDtypeStruct((b, n_tok, w), BF16),
                   jax.ShapeDtypeStruct((b, MLA_HEADS * (MLA_V + V_ONES), n_tok), BF16)],
        compiler_params=_cparams(("arbitrary", "arbitrary")),
        name="mla_prep",
    )(u, tab, qlg, kvlg, wq, wk, wvt, qng, kng)


def _diff_prep_kernel(u_ref, tab_ref, qng_ref, kng_ref, q_ref, k_ref, vt_ref):
    u = u_ref[0]
    ones_bd = _group_ones(DIFF_HD)
    perm = _rope_perm(DIFF_HD, 0, DIFF_HD)
    qg = jnp.concatenate([qng_ref[...]] * 2, axis=1)
    kg = jnp.concatenate([kng_ref[...]] * 2, axis=1)
    scale = DIFF_HD ** -0.5 * LOG2E
    for c in range(0, DIFF_QK, PAIR):
        q_ref[0, :, c:c + PAIR] = (_norm_rope(u[:, c:c + PAIR], qg, ones_bd, perm, tab_ref, DIFF_HD) * scale).astype(BF16)
        k_ref[0, :, c:c + PAIR] = _norm_rope(u[:, DIFF_QK + c:DIFF_QK + c + PAIR], kg, ones_bd, perm, tab_ref,
                                             DIFF_HD).astype(BF16)
    _store_vt(vt_ref, u[:, 2 * DIFF_QK:].T, DIFF_HEADS, 2 * DIFF_HD)


def _diff_prep(u, tab, qng, kng):
    b, n_tok, _ = u.shape
    t = n_tok // TILE
    const = lambda r, c: pl.BlockSpec((r, c), lambda i, j: (0, 0))
    return pl.pallas_call(
        _diff_prep_kernel,
        grid=(b, t),
        in_specs=[pl.BlockSpec((1, TILE, DIFF_COLS), lambda i, j: (i, j, C_DIFF // DIFF_COLS)),
                  pl.BlockSpec((3, TILE, LANES), lambda i, j: (0, j, 0)),
                  const(1, LANES), const(1, LANES)],
        out_specs=[pl.BlockSpec((1, TILE, DIFF_QK), lambda i, j: (i, j, 0)),
                   pl.BlockSpec((1, TILE, DIFF_QK), lambda i, j: (i, j, 0)),
                   pl.BlockSpec((1, DIFF_HEADS * (2 * DIFF_HD + V_ONES), TILE), lambda i, j: (i, 0, j))],
        out_shape=[jax.ShapeDtypeStruct((b, n_tok, DIFF_QK), BF16),
                   jax.ShapeDtypeStruct((b, n_tok, DIFF_QK), BF16),
                   jax.ShapeDtypeStruct((b, DIFF_HEADS * (2 * DIFF_HD + V_ONES), n_tok), BF16)],
        compiler_params=_cparams(("arbitrary", "arbitrary")),
        name="diff_prep",
    )(u, tab, qng, kng)


def _attn_kernel(q_ref, k_ref, vt_ref, *rest, n_heads, dv, diff, lam_init, first_tile):
    if diff:
        lamp_ref, subg_ref, o_ref, ot_ref, s_ref = rest
        lp = lamp_ref[...]
        lam = (jnp.exp(jnp.sum(lp[0:1] * lp[1:2], axis=-1, keepdims=True))
               - jnp.exp(jnp.sum(lp[2:3] * lp[3:4], axis=-1, keepdims=True)) + lam_init)
        lane = _iota((TILE, LANES), 1)
    else:
        o_ref, ot_ref, s_ref = rest
    tile = pl.program_id(1) + first_tile
    dva = dv + V_ONES
    n_iter = n_heads if diff else n_heads // 2

    def streams(i):
        if diff:
            hl = slice(i * LANES, (i + 1) * LANES)
            q = q_ref[0, :, hl]
            zero = jnp.zeros_like(q)
            vr = slice(i * dva, (i + 1) * dva)
            return [(hl, vr, jnp.where(lane < DIFF_HD, q, zero)), (hl, vr, jnp.where(lane < DIFF_HD, zero, q))]
        out = []
        for h in (2 * i, 2 * i + 1):
            hl = slice(h * LANES, (h + 1) * LANES)
            out.append((hl, slice(h * dva, (h + 1) * dva), q_ref[0, :, hl]))
        return out

    def finish(i, accs):
        a0, a1 = accs
        if diff:
            o = a0[:dv] / a0[dv:dv + 1] - lam * (a1[:dv] / a1[dv:dv + 1])
            ms = jnp.mean(o * o, axis=0, keepdims=True)
            o = o * lax.rsqrt(ms + NORM_EPS) * subg_ref[...] * (1.0 - lam_init)
            ot_ref[i * dv:(i + 1) * dv, :] = o
        else:
            o = jnp.concatenate([a0[:dv] / a0[dv:dv + 1], a1[:dv] / a1[dv:dv + 1]], axis=0)
            ot_ref[2 * i * dv:2 * (i + 1) * dv, :] = o

    def run(n_keys):
        blocks = range(0, n_keys, KEY_BLOCK)
        prev = prev_max = None
        for i in range(n_iter + 1):
            par = i % 2
            cur = streams(i) if i < n_iter else None
            cur_max = [None, None]
            accs = [None, None]
            for start in blocks:
                rows = slice(start, start + KEY_BLOCK)
                if cur is not None:
                    for st, (hl, _, q) in enumerate(cur):
                        s = _dot_nt(k_ref[0, rows, hl], q)
                        s_ref[par, st, rows, :] = s
                        part = jnp.max(s, axis=0, keepdims=True)
                        cur_max[st] = part if cur_max[st] is None else jnp.maximum(cur_max[st], part)
                if prev is not None:
                    for st, (_, vr, _) in enumerate(prev):
                        p = jnp.exp2(s_ref[1 - par, st, rows, :] - prev_max[st]).astype(BF16)
                        pv = _dot(vt_ref[0, vr, rows], p)
                        accs[st] = pv if accs[st] is None else accs[st] + pv
            if prev is not None:
                finish(i - 1, accs)
            prev, prev_max = cur, cur_max
        o_ref[0] = ot_ref[...].T.astype(o_ref.dtype)

    pl.when(tile == 0)(lambda: run(TILE))
    pl.when(tile > 0)(lambda: run(k_ref.shape[1]))


def _attention(q, k, vt, *, n_heads, dv, need_ctx, diff=False, lam_p=None, sub_g=None, lam_init=0.0):
    b, n_tok, w = q.shape
    t = n_tok // TILE
    first = 0 if need_ctx else 1
    ow = n_heads * dv
    in_specs = [pl.BlockSpec((1, TILE, w), lambda i, j: (i, j + first, 0)),
                pl.BlockSpec((1, n_tok, w), lambda i, j: (i, 0, 0)),
                pl.BlockSpec((1, n_heads * (dv + V_ONES), n_tok), lambda i, j: (i, 0, 0))]
    args = [q, k, vt]
    if diff:
        in_specs += [pl.BlockSpec(lam_p.shape, lambda i, j: (0, 0)), pl.BlockSpec(sub_g.shape, lambda i, j: (0, 0))]
        args += [lam_p, sub_g]
    return pl.pallas_call(
        functools.partial(_attn_kernel, n_heads=n_heads, dv=dv, diff=diff, lam_init=lam_init, first_tile=first),
        grid=(b, t - first),
        in_specs=in_specs,
        out_specs=pl.BlockSpec((1, TILE, ow), lambda i, j: (i, j + first, 0)),
        out_shape=jax.ShapeDtypeStruct((b, n_tok, ow), BF16),
        scratch_shapes=[pltpu.VMEM((ow, TILE), F32), pltpu.VMEM((2, 2, n_tok, TILE), F32)],
        compiler_params=_cparams(("arbitrary", "arbitrary")),
        name="diff_attn" if diff else "mla_attn",
    )(*args)


def _merge_kernel(x_ref, mod_ref, gof_ref, gob_ref, gz_ref, mla_ref, dif_ref, syf_ref, syb_ref, sz_ref,
                  gates01_ref, gates23_ref, gng_ref, sng_ref, wb_ref, wo_ref, o_ref):
    o = gof_ref[0] + gob_ref[0]
    z = gz_ref[0]
    ya = []
    for h in range(GDN_HEADS):
        hl = slice(h * GDN_DV, (h + 1) * GDN_DV)
        ya.append(_rms(o[:, hl], gng_ref[...]) * _silu(z[:, hl]))
    ya = jnp.concatenate(ya, axis=-1)
    y = (syf_ref[0] + syb_ref[0]) * _silu(sz_ref[0])
    gsz = SSM_INNER // SSM_GROUPS
    yd = jnp.concatenate([_rms(y[:, g * gsz:(g + 1) * gsz], sng_ref[:, g * gsz:(g + 1) * gsz])
                          for g in range(SSM_GROUPS)], axis=-1)
    ys = (ya, mla_ref[0], dif_ref[0], yd)
    m = None
    for i in range(N_BRANCH):
        gates_ref = gates01_ref if i < 2 else gates23_ref
        gate = _sigmoid(gates_ref[0, :, (i % 2) * D_MODEL:(i % 2 + 1) * D_MODEL])
        term = gate * _dot(ys[i].astype(BF16), wb_ref[i])
        m = term if m is None else m + term
    o_ref[0] = x_ref[0] + mod_ref[2:3, :] * _dot(m.astype(BF16), wo_ref[...])


def _merge(xs, mods, gdn_o, u, y_mla, y_diff, ssd_y, gng, sng, wb, wo, *, need_ctx):
    b, n_tok, d = xs.shape
    t = n_tok // TILE
    first = 0 if need_ctx else 1
    row = lambda w, cb=0: pl.BlockSpec((1, TILE, w), lambda i, j: (i, j + first, cb))
    const = lambda shape: pl.BlockSpec(shape, lambda i, j: (0,) * len(shape))
    return pl.pallas_call(
        _merge_kernel,
        grid=(b, t - first),
        in_specs=[row(d),
                  pl.BlockSpec((None, None, 6, d), lambda i, j: (i, jnp.minimum(j + first, 1), 0, 0)),
                  row(BRANCH_W), row(BRANCH_W), row(BRANCH_W, (C_GDN + 3 * BRANCH_W) // BRANCH_W),
                  row(BRANCH_W), row(BRANCH_W), row(BRANCH_W), row(BRANCH_W), row(BRANCH_W, C_SSMZ // BRANCH_W),
                  row(GATE_COLS // 2, C_GATE // (GATE_COLS // 2)), row(GATE_COLS // 2, C_GATE // (GATE_COLS // 2) + 1),
                  const((1, GDN_DV)), const((1, SSM_INNER)), const((N_BRANCH, BRANCH_W, d)), const((d, d))],
        out_specs=row(d),
        out_shape=jax.ShapeDtypeStruct((b, n_tok, d), F32),
        input_output_aliases={0: 0},
        compiler_params=_cparams(("arbitrary", "arbitrary")),
        name="merge",
    )(xs, mods, gdn_o[0], gdn_o[1], u, y_mla, y_diff, ssd_y[0], ssd_y[1], u,
      u, u, gng, sng, wb, wo)


def _mlp_kernel(x_ref, mod_ref, g_ref, w1_ref, w2_ref, o_ref):
    x = x_ref[0]
    h = _rms(x, g_ref[...]) * (1.0 + mod_ref[4:5, :]) + mod_ref[3:4, :]
    a = jnp.maximum(_dot(h.astype(BF16), w1_ref[...]), 0.0)
    o_ref[0] = x + mod_ref[5:6, :] * _dot((a * a).astype(BF16), w2_ref[...])


def _mlp(xs, mods, g, w1, w2, *, need_ctx):
    b, n_tok, d = xs.shape
    t = n_tok // TILE
    first = 0 if need_ctx else 1
    const = lambda shape: pl.BlockSpec(shape, lambda i, j: (0,) * len(shape))
    n_out = n_tok - first * TILE
    return pl.pallas_call(
        _mlp_kernel,
        grid=(b, t - first),
        in_specs=[pl.BlockSpec((1, TILE, d), lambda i, j: (i, j + first, 0)),
                  pl.BlockSpec((None, None, 6, d), lambda i, j: (i, jnp.minimum(j + first, 1), 0, 0)),
                  const((1, d)), const((d, D_FF)), const((D_FF, d))],
        out_specs=pl.BlockSpec((1, TILE, d), lambda i, j: (i, j, 0)),
        out_shape=jax.ShapeDtypeStruct((b, n_out, d), F32),
        compiler_params=_cparams(("arbitrary", "arbitrary")),
        name="mlp",
    )(xs, mods, g, w1, w2)


def _repack_w_in(w):
    d = w.shape[0]
    z = lambda n: jnp.zeros((d, n), w.dtype)
    o_mla = GDN_COLS
    o_diff = o_mla + MLA_COLS
    o_ssm = o_diff + DIFF_COLS
    o_gate = MIX_COLS
    gdn_ab = w[:, 2 * GDN_QK + 2 * GDN_VW:GDN_COLS]
    ssm_dt = w[:, o_ssm + 2 * SSM_INNER + 2 * SSM_BC:o_ssm + SSM_COLS]
    kpe = w[:, o_mla + MLA_Q_LORA + MLA_KV_LORA:o_mla + MLA_COLS]
    parts = [
        w[:, 0:2 * GDN_QK + 2 * GDN_VW],
        w[:, o_ssm + SSM_INNER:o_ssm + 2 * SSM_INNER + 2 * SSM_BC],
        w[:, o_diff:o_diff + DIFF_COLS],
        w[:, o_mla:o_mla + MLA_Q_LORA + MLA_KV_LORA], z(MLA_NOPE), kpe, z(LANES - MLA_QK),
        gdn_ab, ssm_dt, z(LANES - 4 * GDN_HEADS - 2 * SSM_HEADS),
        z(C_SSMZ - C_SMALL - LANES),
        w[:, o_ssm:o_ssm + SSM_INNER],
        w[:, o_gate:o_gate + GATE_COLS],
    ]
    out = jnp.concatenate(parts, axis=1)
    assert out.shape[1] == N_IN
    return out.astype(BF16)


def _lane_row(vals, offset):
    row = jnp.zeros((1, LANES), F32)
    return row.at[0, offset:offset + vals.shape[0]].set(vals.astype(F32))


def _rope_tables(n_lat, rot_dim, lane_offset, period):
    rows = n_lat // GRID_W
    row = jnp.repeat(jnp.arange(rows, dtype=F32), GRID_W)
    col = jnp.tile(jnp.arange(GRID_W, dtype=F32), rows)
    quarter = rot_dim // 4
    inv = ROPE_THETA ** (-jnp.arange(quarter, dtype=F32) / quarter)
    ar = row[:, None] * inv
    ac = col[:, None] * inv
    ang = jnp.concatenate([ar, ar, ac, ac], axis=-1)
    cos, sin = jnp.cos(ang), jnp.sin(ang)
    first = (np.arange(rot_dim) % (2 * quarter)) < quarter
    sin_a = jnp.where(first, -sin, 0.0)
    sin_b = jnp.where(first, 0.0, sin)
    reps = LANES // period
    def place(t, fill):
        blk = jnp.full((n_lat, period), fill, F32).at[:, lane_offset:lane_offset + rot_dim].set(t)
        blk = jnp.tile(blk, (1, reps))
        ctx = jnp.full((CTX_LEN, LANES), fill, F32)
        return jnp.concatenate([ctx, blk], axis=0)
    return jnp.stack([place(cos, 1.0), place(sin_a, 0.0), place(sin_b, 0.0)])


def kernel(x, c, ctx, c_ctx, ada_w, ada_b, norm1_g, norm2_g, w_in, gdn_conv, gdn_a_log, gdn_dt_bias, gdn_norm_g,
           mla_q_lora_g, mla_kv_lora_g, mla_w_uq, mla_w_ukv, mla_qn_g, mla_kn_g, diff_qn_g, diff_kn_g,
           diff_lambda, diff_sub_g, ssm_conv, ssm_conv_b, ssm_a_log, ssm_dt_bias, ssm_d, ssm_norm_g, w_branch,
           w_out, mlp_w1, mlp_w2):
    b, n_lat, d = x.shape
    depth = w_in.shape[0]
    assert ctx.shape[1] == CTX_LEN == TILE and n_lat % TILE == 0 and d == D_MODEL
    xs = jnp.concatenate([ctx, x], axis=1)
    tab_mla = _rope_tables(n_lat, MLA_ROPE, MLA_NOPE, LANES)
    tab_diff = _rope_tables(n_lat, DIFF_HD, 0, DIFF_HD)
    rows = b + 8
    cc = jnp.zeros((rows, d), F32).at[:b].set(c).at[b].set(c_ctx)
    row2 = lambda v: v.reshape(1, -1).astype(F32)

    for l in range(depth):
        need_ctx = l < depth - 1
        lam_init = LAMBDA_BASE - LAMBDA_AMP * math.exp(-LAMBDA_RATE * l)
        mod = _ada(cc, ada_w[l], ada_b[l].reshape(1, -1))
        mod_lat = mod[:b].reshape(b, 6, d)
        mod_ctx = jnp.broadcast_to(mod[b].reshape(1, 6, d), (b, 6, d))
        mods = jnp.stack([mod_ctx, mod_lat], axis=1)

        u = _inproj(xs, mods, row2(norm1_g[l]), _repack_w_in(w_in[l]))

        gdn_o = _gdn_scan(_gdn_prep(u, gdn_conv[l].astype(F32),
                                    _lane_row(gdn_a_log[l].reshape(-1), SM_A),
                                    _lane_row(gdn_dt_bias[l].reshape(-1), SM_A)))

        wq = mla_w_uq[l].reshape(MLA_Q_LORA, MLA_HEADS, MLA_QK)
        wq = jnp.pad(wq, ((0, 0), (0, 0), (0, LANES - MLA_QK))).reshape(MLA_Q_LORA, MLA_HEADS * LANES)
        wkv = mla_w_ukv[l].reshape(MLA_KV_LORA, MLA_HEADS, MLA_NOPE + MLA_V)
        wk = jnp.pad(wkv[:, :, :MLA_NOPE], ((0, 0), (0, 0), (0, LANES - MLA_NOPE))).reshape(MLA_KV_LORA, MLA_HEADS * LANES)
        wvt = wkv[:, :, MLA_NOPE:].reshape(MLA_KV_LORA, MLA_HEADS * MLA_V).T
        pad_g = lambda g: jnp.pad(g.astype(F32), (0, LANES - MLA_QK)).reshape(1, LANES)
        q_m, k_m, vt_m = _mla_prep(u, tab_mla, row2(mla_q_lora_g[l]), row2(mla_kv_lora_g[l]),
                                   wq.astype(BF16), wk.astype(BF16), wvt.astype(BF16),
                                   pad_g(mla_qn_g[l]), pad_g(mla_kn_g[l]))
        y_mla = _attention(q_m, k_m, vt_m, n_heads=MLA_HEADS, dv=MLA_V, need_ctx=need_ctx)

        rep_g = lambda g: jnp.tile(g.astype(F32), LANES // DIFF_HD).reshape(1, LANES)
        q_d, k_d, vt_d = _diff_prep(u, tab_diff, rep_g(diff_qn_g[l]), rep_g(diff_kn_g[l]))
        y_diff = _attention(q_d, k_d, vt_d, n_heads=DIFF_HEADS, dv=2 * DIFF_HD, need_ctx=need_ctx, diff=True,
                            lam_p=diff_lambda[l].astype(F32), sub_g=diff_sub_g[l].astype(F32).reshape(-1, 1),
                            lam_init=lam_init)

        ssd_y = _ssd(u, ssm_conv[l].astype(F32), row2(ssm_conv_b[l]),
                     _lane_row(ssm_a_log[l].reshape(-1), SM_DT), _lane_row(ssm_dt_bias[l].reshape(-1), SM_DT),
                     jnp.repeat(ssm_d[l].astype(F32), SSM_HEAD_DIM).reshape(1, -1))

        xs = _merge(xs, mods, gdn_o, u, y_mla, y_diff, ssd_y, row2(gdn_norm_g[l]), row2(ssm_norm_g[l]),
                    w_branch[l].astype(BF16), w_out[l].astype(BF16), need_ctx=need_ctx)
        xs = _mlp(xs, mods, row2(norm2_g[l]), mlp_w1[l].astype(BF16), mlp_w2[l].astype(BF16), need_ctx=need_ctx)
    return xs
```

```python
import functools
import math

import jax
import jax.numpy as jnp
import numpy as np
from jax import lax
from jax.experimental import pallas as pl
from jax.experimental.pallas import tpu as pltpu

F32 = jnp.float32
BF16 = jnp.bfloat16

D_MODEL = 1024
CTX_LEN = 256
GRID_W = 64
ROPE_THETA = 10000.0
NORM_EPS = 1e-6
CONV_K = 5
N_BRANCH = 4
D_FF = 4 * D_MODEL
GDN_HEADS = 4
GDN_DK = 128
GDN_DV = 128
MLA_HEADS = 8
MLA_NOPE = 64
MLA_ROPE = 32
MLA_V = 64
MLA_QK = MLA_NOPE + MLA_ROPE
MLA_Q_LORA = 384
MLA_KV_LORA = 256
DIFF_HEADS = 4
DIFF_HD = 64
LAMBDA_BASE = 0.8
LAMBDA_AMP = 0.6
LAMBDA_RATE = 0.3
SSM_HEADS = 8
SSM_HEAD_DIM = 64
SSM_GROUPS = 2
SSM_STATE = 128
GDN_QK = GDN_HEADS * GDN_DK
GDN_VW = GDN_HEADS * GDN_DV
DIFF_QK = DIFF_HEADS * 2 * DIFF_HD
DIFF_VW = DIFF_HEADS * 2 * DIFF_HD
SSM_INNER = SSM_HEADS * SSM_HEAD_DIM
SSM_BC = SSM_GROUPS * SSM_STATE
BRANCH_W = 512
GDN_COLS = 2 * GDN_QK + 2 * GDN_VW + 4 * GDN_HEADS
MLA_COLS = MLA_Q_LORA + MLA_KV_LORA + MLA_ROPE
DIFF_COLS = 2 * DIFF_QK + DIFF_VW
SSM_COLS = 2 * SSM_INNER + 2 * SSM_BC + 2 * SSM_HEADS
MIX_COLS = GDN_COLS + MLA_COLS + DIFF_COLS + SSM_COLS
GATE_COLS = N_BRANCH * D_MODEL

LANES = 128
TILE = 256
CHUNK = 64
CPT = TILE // CHUNK
HALO = 8
VMEM_LIMIT = 56 * 1024 * 1024
V_ONES = 16
KEY_BLOCK = 256
LOG2E = 1.4426950408889634

C_GDN = 0
C_XBC = 2048
C_DIFF = 3072
C_MLA = 4608
C_SMALL = 5376
C_SSMZ = 5632
C_GATE = 6144
N_IN = 10240
IN_TN = 2560

SM_A = 0
SM_B = 8
SM_DT = 16


def _cparams(sem):
    return pltpu.CompilerParams(dimension_semantics=sem, vmem_limit_bytes=VMEM_LIMIT)


def _dot(a, b):
    return jnp.dot(a, b, preferred_element_type=F32)


def _dot_nt(a, b):
    return lax.dot_general(a, b, (((1,), (1,)), ((), ())), preferred_element_type=F32)


def _dot_tn(a, b):
    return lax.dot_general(a, b, (((0,), (0,)), ((), ())), preferred_element_type=F32)


def _split3(x):
    x1 = x.astype(BF16)
    r1 = x - x1.astype(F32)
    x2 = r1.astype(BF16)
    r2 = r1 - x2.astype(F32)
    return x1, x2, r2.astype(BF16)


def _dot_exact_lhs(a_bf, x):
    x1, x2, x3 = _split3(x)
    return _dot(a_bf, x1) + _dot(a_bf, x2) + _dot(a_bf, x3)


def _dot_exact_rhs(x, b_bf):
    x1, x2, x3 = _split3(x)
    return _dot(x1, b_bf) + _dot(x2, b_bf) + _dot(x3, b_bf)


def _sigmoid(x):
    return 1.0 / (1.0 + jnp.exp(-x))


def _silu(x):
    return x * _sigmoid(x)


def _softplus(x):
    return jnp.maximum(x, 0.0) + jnp.log1p(jnp.exp(-jnp.abs(x)))


def _iota(shape, dim):
    return lax.broadcasted_iota(jnp.int32, shape, dim)


def _chunk_tri(n, upper):
    r = _iota((n, n), 0)
    c = _iota((n, n), 1)
    same = (r // CHUNK) == (c // CHUNK)
    tri = (r <= c) if upper else (r >= c)
    return jnp.where(same & tri, 1.0, 0.0).astype(BF16)


def _expander(n_rows, first_row, n_groups, width):
    r = _iota((n_rows, n_groups * width), 0)
    c = _iota((n_rows, n_groups * width), 1)
    return jnp.where(r == first_row + c // width, 1.0, 0.0).astype(BF16)


def _ada_kernel(c_ref, w_ref, b_ref, o_ref):
    a = _silu(c_ref[...])
    a1, a2, a3 = _split3(a)
    w1, w2, w3 = _split3(w_ref[...])
    acc = _dot(a1, w1) + (_dot(a1, w2) + _dot(a2, w1)) + (_dot(a1, w3) + _dot(a2, w2) + _dot(a3, w1))
    o_ref[...] = acc + b_ref[...]


def _ada(cc, w, b):
    rows, d = cc.shape
    n = w.shape[1]
    tn = 1536
    return pl.pallas_call(
        _ada_kernel,
        grid=(n // tn,),
        in_specs=[pl.BlockSpec((rows, d), lambda j: (0, 0)),
                  pl.BlockSpec((d, tn), lambda j: (0, j)),
                  pl.BlockSpec((1, tn), lambda j: (0, j))],
        out_specs=pl.BlockSpec((rows, tn), lambda j: (0, j)),
        out_shape=jax.ShapeDtypeStruct((rows, n), F32),
        compiler_params=_cparams(("arbitrary",)),
        name="ada",
    )(cc, w, b)


def _rms(x, g):
    ms = jnp.mean(x * x, axis=-1, keepdims=True)
    return x * lax.rsqrt(ms + NORM_EPS) * g


def _inproj_kernel(x_ref, mod_ref, g_ref, w_ref, o_ref):
    h = _rms(x_ref[0], g_ref[...])
    hb = (h * (1.0 + mod_ref[1:2, :]) + mod_ref[0:1, :]).astype(BF16)
    for c in range(0, N_IN, IN_TN):
        o_ref[0, :, c:c + IN_TN] = _dot(hb, w_ref[:, c:c + IN_TN])


def _inproj(xs, mods, g, w):
    b, n_tok, d = xs.shape
    t = n_tok // TILE
    return pl.pallas_call(
        _inproj_kernel,
        grid=(b, t),
        in_specs=[pl.BlockSpec((1, TILE, d), lambda i, j: (i, j, 0)),
                  pl.BlockSpec((None, None, 6, d), lambda i, j: (i, jnp.minimum(j, 1), 0, 0)),
                  pl.BlockSpec((1, d), lambda i, j: (0, 0)),
                  pl.BlockSpec((d, N_IN), lambda i, j: (0, 0), pipeline_mode=pl.Buffered(1))],
        out_specs=pl.BlockSpec((1, TILE, N_IN), lambda i, j: (i, j, 0)),
        out_shape=jax.ShapeDtypeStruct((b, n_tok, N_IN), F32),
        compiler_params=_cparams(("arbitrary", "arbitrary")),
        name="inproj",
    )(xs, mods, g, w)


def _halo_specs(width, col_block, n_tiles, tile_of):
    rpt = TILE // HALO
    last = n_tiles * rpt - 1
    main = pl.BlockSpec((1, TILE, width), lambda *ids: (ids[0], tile_of(*ids), col_block))
    prev = pl.BlockSpec((1, HALO, width),
                        lambda *ids: (ids[0], jnp.maximum(tile_of(*ids) * rpt - 1, 0), col_block))
    nxt = pl.BlockSpec((1, HALO, width),
                       lambda *ids: (ids[0], jnp.minimum(tile_of(*ids) * rpt + rpt, last), col_block))
    return main, prev, nxt


def _conv_tile(main_ref, prev_ref, next_ref, w_ref, ext_ref, tile, n_tiles):
    prev_ok = (tile >= 2).astype(F32)
    next_ok = jnp.logical_and(tile >= 1, tile < n_tiles - 1).astype(F32)
    ext_ref[0:HALO, :] = prev_ref[0] * prev_ok
    ext_ref[HALO:HALO + TILE, :] = main_ref[0]
    ext_ref[HALO + TILE:2 * HALO + TILE, :] = next_ref[0] * next_ok
    half = CONV_K // 2
    acc = None
    for k in range(CONV_K):
        term = w_ref[k:k + 1, :] * ext_ref[HALO - half + k:HALO - half + k + TILE, :]
        acc = term if acc is None else acc + term
    return acc


def _block_inverse_many(mats):
    n = mats[0].shape[0]
    ri = _iota((n, n), 0)
    ci = _iota((n, n), 1)
    eye = jnp.where(ri == ci, 1.0, 0.0).astype(F32)
    pair = (ri // 2) == (ci // 2)
    ts = [eye - jnp.where(pair, a, 0.0) for a in mats]
    abs_ = [a.astype(BF16) for a in mats]
    zero = jnp.zeros((n, n), BF16)
    size = 2
    while size < CHUNK:
        off = ((ri // (2 * size)) == (ci // (2 * size))) & ((ri // size) != (ci // size))
        tbs = [t.astype(BF16) for t in ts]
        xs = [_dot(jnp.where(off, ab, zero), tb).astype(BF16) for ab, tb in zip(abs_, tbs)]
        ts = [t - _dot(tb, x) for t, tb, x in zip(ts, tbs, xs)]
        size *= 2
    return ts


def _gdn_prep_kernel(main_ref, prev_ref, next_ref, sm_ref, cw_ref, alog_ref, dtb_ref,
                     u0_ref, wq0_ref, kd0_ref, at0_ref, gl0_ref,
                     u1_ref, wq1_ref, kd1_ref, at1_ref, gl1_ref,
                     ext_ref, *, n_tiles):
    tile = pl.program_id(1)
    qkv = _silu(_conv_tile(main_ref, prev_ref, next_ref, cw_ref, ext_ref, tile, n_tiles))
    sm = sm_ref[0]
    g_all = -jnp.exp(alog_ref[...]) * _softplus(sm + dtb_ref[...])
    beta_all = _sigmoid(sm)
    nh = GDN_HEADS
    exp_g = _expander(LANES, SM_A, 2 * nh, LANES)
    exp_b = _expander(LANES, SM_B, 2 * nh, LANES)
    beta_x = _dot_exact_rhs(beta_all, exp_b)
    outs = ((u0_ref, wq0_ref, kd0_ref, at0_ref, gl0_ref), (u1_ref, wq1_ref, kd1_ref, at1_ref, gl1_ref))
    ri = _iota((TILE, TILE), 0)
    ci = _iota((TILE, TILE), 1)
    same = (ri // CHUNK) == (ci // CHUNK)
    cums, masks = [], []
    for d in range(2):
        cum = _dot_exact_lhs(_chunk_tri(TILE, upper=(d == 1)), g_all)
        cums.append((_dot_exact_rhs(cum, exp_g), cum.T))
        masks.append((same & ((ri >= ci) if d == 0 else (ri <= ci)), same & ((ri > ci) if d == 0 else (ri < ci))))

    chains, a_mats = [], []
    for h in range(nh):
        q = qkv[:, h * GDN_DK:(h + 1) * GDN_DK]
        k = qkv[:, GDN_QK + h * GDN_DK:GDN_QK + (h + 1) * GDN_DK]
        v = qkv[:, 2 * GDN_QK + h * GDN_DV:2 * GDN_QK + (h + 1) * GDN_DV]
        q = q * lax.rsqrt(jnp.sum(q * q, axis=-1, keepdims=True) + 1e-6) * (GDN_DK ** -0.5)
        k = k * lax.rsqrt(jnp.sum(k * k, axis=-1, keepdims=True) + 1e-6)
        kb = k.astype(BF16)
        kq = _dot_nt(jnp.concatenate([kb, q.astype(BF16)], axis=0), kb)
        for d in range(2):
            col = d * nh + h
            lanes = slice(col * LANES, (col + 1) * LANES)
            cum_x, cum_t = cums[d]
            incl, strict = masks[d]
            g_col = cum_x[:, lanes]
            b_col = beta_x[:, lanes]
            diff = jnp.concatenate([g_col, g_col], axis=1) - cum_t[SM_A + col:SM_A + col + 1, :]
            decay = jnp.where(incl, jnp.exp(jnp.where(incl, diff, 0.0)), 0.0)
            a_mats.append(jnp.where(strict, kq[:TILE] * jnp.concatenate([b_col, b_col], axis=1) * decay, 0.0))
            e_g = jnp.exp(g_col)
            rhs = jnp.concatenate([v * b_col, k * b_col * e_g], axis=1).astype(BF16)
            chains.append((h, d, q, k, g_col, e_g, rhs, kq[TILE:] * decay))

    t_mats = _block_inverse_many(a_mats)
    for (h, d, q, k, g_col, e_g, rhs, attn), t_mat in zip(chains, t_mats):
        uw = _dot(t_mat.astype(BF16), rhs)
        qg = (q * e_g).astype(BF16)
        u_ref, wq_ref, kd_ref, at_ref, gl_ref = outs[d]
        hl = slice(h * LANES, (h + 1) * LANES)
        for c in range(CPT):
            rows = slice(c * CHUNK, (c + 1) * CHUNK)
            last = c * CHUNK + (CHUNK - 1 if d == 0 else 0)
            g_last = g_col[last:last + 1, :]
            slot = c if d == 0 else CPT - 1 - c
            u_ref[0, 0, slot, :, hl] = uw[rows, :GDN_DV]
            wq_ref[0, 0, slot, 0:CHUNK, hl] = uw[rows, GDN_DV:].astype(BF16)
            wq_ref[0, 0, slot, CHUNK:2 * CHUNK, hl] = qg[rows]
            kd_ref[0, 0, slot, :, hl] = (k[rows] * jnp.exp(g_last - g_col[rows])).astype(BF16)
            at_ref[0, 0, slot, :, h * CHUNK:(h + 1) * CHUNK] = attn[rows, rows].astype(BF16)
            gl_ref[0, 0, slot, :, hl] = jnp.exp(g_last)


def _bwd_block(j, n_tiles):
    return jnp.where(j == 0, 0, n_tiles - j)


def _gdn_prep(u, conv_w, alog_row, dtb_row):
    b, n_tok, _ = u.shape
    t = n_tok // TILE
    w = GDN_VW
    cw = 2 * GDN_QK + GDN_VW
    main, prev, nxt = _halo_specs(cw, 0, t, lambda i, j: j)
    fwd = lambda i, j: (i, j, 0, 0, 0)
    bwd = lambda i, j: (i, _bwd_block(j, t), 0, 0, 0)

    def outs(imap):
        return [pl.BlockSpec((1, 1, CPT, CHUNK, w), imap),
                pl.BlockSpec((1, 1, CPT, 2 * CHUNK, w), imap),
                pl.BlockSpec((1, 1, CPT, CHUNK, w), imap),
                pl.BlockSpec((1, 1, CPT, CHUNK, GDN_HEADS * CHUNK), imap),
                pl.BlockSpec((1, 1, CPT, 1, w), imap)]

    shapes = [jax.ShapeDtypeStruct((b, t, CPT, CHUNK, w), F32),
              jax.ShapeDtypeStruct((b, t, CPT, 2 * CHUNK, w), BF16),
              jax.ShapeDtypeStruct((b, t, CPT, CHUNK, w), BF16),
              jax.ShapeDtypeStruct((b, t, CPT, CHUNK, GDN_HEADS * CHUNK), BF16),
              jax.ShapeDtypeStruct((b, t, CPT, 1, w), F32)]
    return pl.pallas_call(
        functools.partial(_gdn_prep_kernel, n_tiles=t),
        grid=(b, t),
        in_specs=[main, prev, nxt,
                  pl.BlockSpec((1, TILE, LANES), lambda i, j: (i, j, C_SMALL // LANES)),
                  pl.BlockSpec((CONV_K, cw), lambda i, j: (0, 0)),
                  pl.BlockSpec((1, LANES), lambda i, j: (0, 0)),
                  pl.BlockSpec((1, LANES), lambda i, j: (0, 0))],
        out_specs=outs(fwd) + outs(bwd),
        out_shape=shapes + shapes,
        scratch_shapes=[pltpu.VMEM((TILE + 2 * HALO, cw), F32)],
        compiler_params=_cparams(("arbitrary", "arbitrary")),
        name="gdn_prep",
    )(u, u, u, u, conv_w, alog_row, dtb_row)


def _gdn_scan_kernel(u0_ref, wq0_ref, kd0_ref, at0_ref, gl0_ref,
                     u1_ref, wq1_ref, kd1_ref, at1_ref, gl1_ref,
                     of_ref, ob_ref, s_ref):
    @pl.when(pl.program_id(1) == 0)
    def _():
        s_ref[...] = jnp.zeros_like(s_ref)

    ins = ((u0_ref, wq0_ref, kd0_ref, at0_ref, gl0_ref), (u1_ref, wq1_ref, kd1_ref, at1_ref, gl1_ref))
    chains = [(d, h) for d in range(2) for h in range(GDN_HEADS)]
    hls = [slice(h * LANES, (h + 1) * LANES) for _, h in chains]
    states = [s_ref[d, h] for d, h in chains]
    for slot in range(CPT):
        rs = [_dot(ins[d][1][0, 0, slot, :, hl], s.astype(BF16)) for (d, _), hl, s in zip(chains, hls, states)]
        vbs = [(ins[d][0][0, 0, slot, :, hl] - r[0:CHUNK]).astype(BF16) for (d, _), hl, r in zip(chains, hls, rs)]
        outs = [r[CHUNK:2 * CHUNK] + _dot(ins[d][3][0, 0, slot, :, h * CHUNK:(h + 1) * CHUNK], vb)
                for (d, h), r, vb in zip(chains, rs, vbs)]
        states = [s * ins[d][4][0, 0, slot, :, hl] + _dot_tn(ins[d][2][0, 0, slot, :, hl], vb)
                  for (d, _), hl, s, vb in zip(chains, hls, states, vbs)]
        for (d, _), hl, o in zip(chains, hls, outs):
            c = slot if d == 0 else CPT - 1 - slot
            (of_ref if d == 0 else ob_ref)[0, c * CHUNK:(c + 1) * CHUNK, hl] = o
    for (d, h), s in zip(chains, states):
        s_ref[d, h] = s


def _gdn_scan(prep):
    b, t = prep[0].shape[:2]
    w = GDN_VW
    imap = lambda i, j: (i, j, 0, 0, 0)
    specs = [pl.BlockSpec((1, 1, CPT, CHUNK, w), imap),
             pl.BlockSpec((1, 1, CPT, 2 * CHUNK, w), imap),
             pl.BlockSpec((1, 1, CPT, CHUNK, w), imap),
             pl.BlockSpec((1, 1, CPT, CHUNK, GDN_HEADS * CHUNK), imap),
             pl.BlockSpec((1, 1, CPT, 1, w), imap)]
    return pl.pallas_call(
        _gdn_scan_kernel,
        grid=(b, t),
        in_specs=specs + specs,
        out_specs=[pl.BlockSpec((1, TILE, w), lambda i, j: (i, j, 0)),
                   pl.BlockSpec((1, TILE, w), lambda i, j: (i, _bwd_block(j, t), 0))],
        out_shape=[jax.ShapeDtypeStruct((b, t * TILE, w), F32)] * 2,
        scratch_shapes=[pltpu.VMEM((2, GDN_HEADS, GDN_DK, GDN_DV), F32)],
        compiler_params=_cparams(("arbitrary", "arbitrary")),
        name="gdn_scan",
    )(*prep)


def _ssd_direction(d, xbc, sm, y_ref, st_ref, alog_ref, dtb_ref, dskip_ref):
    nh, hd, ng = SSM_HEADS, SSM_HEAD_DIM, SSM_GROUPS
    hpg = nh // ng
    gw = hpg * hd
    xs = xbc[:, :SSM_INNER]
    dt_all = _softplus(sm + dtb_ref[...])
    da_all = dt_all * (-jnp.exp(alog_ref[...]))
    cum = _dot_exact_lhs(_chunk_tri(TILE, upper=(d == 1)), da_all)
    cum_t = cum.T
    expand = _expander(LANES, SM_DT + d * nh, nh, hd)
    dt_x = _dot_exact_rhs(dt_all, expand)
    cum_x = _dot_exact_rhs(cum, expand)
    xdt = xs * dt_x
    ri = _iota((CHUNK, gw), 0)
    ci = _iota((CHUNK, gw), 1) % CHUNK
    incl = (ri >= ci) if d == 0 else (ri <= ci)
    head_diag = (_iota((gw, gw), 0) // hd) == (_iota((gw, gw), 1) // hd)
    zero_bf = jnp.zeros((gw, gw), BF16)
    for step in range(CPT):
        c = step if d == 0 else CPT - 1 - step
        rows = slice(c * CHUNK, (c + 1) * CHUNK)
        last = c * CHUNK + (CHUNK - 1 if d == 0 else 0)
        cum_c = cum_x[rows]
        cum_last = cum_x[last:last + 1]
        decay_states = jnp.exp(cum_last - cum_c)
        in_decay = jnp.exp(cum_c)
        chunk_decay = jnp.exp(cum_last)
        for g in range(ng):
            gl = slice(g * gw, (g + 1) * gw)
            bm = xbc[rows, SSM_INNER + g * SSM_STATE:SSM_INNER + (g + 1) * SSM_STATE].astype(BF16)
            cm = xbc[rows, SSM_INNER + SSM_BC + g * SSM_STATE:SSM_INNER + SSM_BC + (g + 1) * SSM_STATE].astype(BF16)
            state = st_ref[d, g]
            y_off = _dot(cm, state.astype(BF16)) * in_decay[:, gl]
            xdt_c = xdt[rows, gl]
            st_ref[d, g] = state * chunk_decay[:, gl] + _dot_tn(bm, (xdt_c * decay_states[:, gl]).astype(BF16))
            scores = _dot_nt(cm, jnp.concatenate([bm] * hpg, axis=0))
            col0 = SM_DT + d * nh + g * hpg
            row_terms = jnp.concatenate([cum_t[col0 + hh:col0 + hh + 1, rows] for hh in range(hpg)], axis=1)
            seg = cum_c[:, gl] - row_terms
            lmat = jnp.where(incl, jnp.exp(jnp.where(incl, seg, 0.0)), 0.0)
            x_bd = jnp.where(head_diag, jnp.concatenate([xdt_c.astype(BF16)] * hpg, axis=0), zero_bf)
            y = _dot((scores * lmat).astype(BF16), x_bd) + y_off
            if d == 0:
                y = y + dskip_ref[:, gl] * xs[rows, gl]
            y_ref[0, rows, gl] = y


def _ssd_kernel(m_ref, p_ref, n_ref, sm_ref, cw_ref, cb_ref, alog_ref, dtb_ref, dskip_ref,
                yf_ref, yb_ref, ext_ref, st_ref, cache_ref, *, n_tiles):
    s = pl.program_id(1)

    @pl.when(s == 0)
    def _():
        st_ref[...] = jnp.zeros_like(st_ref)

    @pl.when(s < n_tiles)
    def _():
        xbc = _silu(_conv_tile(m_ref, p_ref, n_ref, cw_ref, ext_ref, s, n_tiles) + cb_ref[...])
        cache_ref[s] = xbc
        _ssd_direction(0, xbc, sm_ref[0], yf_ref, st_ref, alog_ref, dtb_ref, dskip_ref)

    @pl.when(s >= n_tiles)
    def _():
        tile = _bwd_block(s - n_tiles, n_tiles)
        _ssd_direction(1, cache_ref[tile], sm_ref[0], yb_ref, st_ref, alog_ref, dtb_ref, dskip_ref)


def _ssd(u, conv_w, conv_b, alog_row, dtb_row, dskip_row):
    b, n_tok, _ = u.shape
    t = n_tok // TILE
    cw = SSM_INNER + 2 * SSM_BC
    main, prev, nxt = _halo_specs(cw, C_XBC // cw, t, lambda i, s: jnp.minimum(s, t - 1))
    tile_of = lambda s: jnp.where(s < t, s, _bwd_block(s - t, t))
    const = lambda r, c: pl.BlockSpec((r, c), lambda i, s: (0, 0))
    return pl.pallas_call(
        functools.partial(_ssd_kernel, n_tiles=t),
        grid=(b, 2 * t),
        in_specs=[main, prev, nxt,
                  pl.BlockSpec((1, TILE, LANES), lambda i, s: (i, tile_of(s), C_SMALL // LANES)),
                  const(CONV_K, cw), const(1, cw), const(1, LANES), const(1, LANES), const(1, SSM_INNER)],
        out_specs=[pl.BlockSpec((1, TILE, SSM_INNER), lambda i, s: (i, jnp.minimum(s, t - 1), 0)),
                   pl.BlockSpec((1, TILE, SSM_INNER), lambda i, s: (i, jnp.where(s < t, 0, _bwd_block(s - t, t)), 0))],
        out_shape=[jax.ShapeDtypeStruct((b, n_tok, SSM_INNER), F32)] * 2,
        scratch_shapes=[pltpu.VMEM((TILE + 2 * HALO, cw), F32),
                        pltpu.VMEM((2, SSM_GROUPS, SSM_STATE, SSM_INNER // SSM_GROUPS), F32),
                        pltpu.VMEM((t, TILE, cw), F32)],
        compiler_params=_cparams(("arbitrary", "arbitrary")),
        name="ssd",
    )(u, u, u, u, conv_w, conv_b, alog_row, dtb_row, dskip_row)


PAIR = 2 * LANES


def _group_ones(group):
    r = _iota((PAIR, PAIR), 0)
    c = _iota((PAIR, PAIR), 1)
    return jnp.where((r // group) == (c // group), 1.0, 0.0).astype(BF16)


def _rope_perm(period, lane_offset, rot_dim):
    r = _iota((PAIR, PAIR), 0)
    c = _iota((PAIR, PAIR), 1)
    q = rot_dim // 4
    cl = c % period - lane_offset
    src = jnp.where((cl % (2 * q)) < q, c + q, c - q)
    return jnp.where((cl >= 0) & (cl < rot_dim) & (r == src), 1.0, 0.0).astype(BF16)


def _norm_rope(x, gain, ones_bd, perm, tab_ref, n_real):
    ss = _dot((x * x).astype(BF16), ones_bd)
    xn = x * lax.rsqrt(ss / n_real + NORM_EPS) * gain
    hi = xn.astype(BF16)
    lo = (xn - hi.astype(F32)).astype(BF16)
    rot = _dot(hi, perm) + _dot(lo, perm)
    cos = jnp.concatenate([tab_ref[0]] * 2, axis=1)
    sin = jnp.concatenate([tab_ref[1] + tab_ref[2]] * 2, axis=1)
    return xn * cos + rot * sin


def _store_vt(vt_ref, vt, n_heads, dv):
    ones = jnp.ones((V_ONES, vt.shape[1]), BF16)
    for h in range(n_heads):
        base = h * (dv + V_ONES)
        vt_ref[0, base:base + dv, :] = vt[h * dv:(h + 1) * dv].astype(BF16)
        vt_ref[0, base + dv:base + dv + V_ONES, :] = ones


def _mla_prep_kernel(u_ref, tab_ref, qlg_ref, kvlg_ref, wq_ref, wk_ref, wvt_ref, qng_ref, kng_ref,
                     q_ref, k_ref, vt_ref):
    u = u_ref[0]
    cq = _rms(u[:, :MLA_Q_LORA], qlg_ref[...]).astype(BF16)
    ckv = _rms(u[:, MLA_Q_LORA:MLA_Q_LORA + MLA_KV_LORA], kvlg_ref[...]).astype(BF16)
    kpe = u[:, MLA_Q_LORA + MLA_KV_LORA:]
    q_all = _dot(cq, wq_ref[...])
    k_all = _dot(ckv, wk_ref[...])
    _store_vt(vt_ref, _dot_nt(wvt_ref[...], ckv), MLA_HEADS, MLA_V)
    scale = MLA_QK ** -0.5 * LOG2E
    ones_bd = _group_ones(LANES)
    perm = _rope_perm(LANES, MLA_NOPE, MLA_ROPE)
    qg = jnp.concatenate([qng_ref[...]] * 2, axis=1)
    kg = jnp.concatenate([kng_ref[...]] * 2, axis=1)
    kpe2 = jnp.concatenate([kpe, kpe], axis=1)
    for h in range(0, MLA_HEADS, 2):
        hl = slice(h * LANES, (h + 2) * LANES)
        q_ref[0, :, hl] = (_norm_rope(q_all[:, hl], qg, ones_bd, perm, tab_ref, MLA_QK) * scale).astype(BF16)
        k_ref[0, :, hl] = _norm_rope(k_all[:, hl] + kpe2, kg, ones_bd, perm, tab_ref, MLA_QK).astype(BF16)


def _mla_prep(u, tab, qlg, kvlg, wq, wk, wvt, qng, kng):
    b, n_tok, _ = u.shape
    t = n_tok // TILE
    w = MLA_HEADS * LANES
    cw = MLA_Q_LORA + MLA_KV_LORA + LANES
    const = lambda r, c: pl.BlockSpec((r, c), lambda i, j: (0, 0))
    return pl.pallas_call(
        _mla_prep_kernel,
        grid=(b, t),
        in_specs=[pl.BlockSpec((1, TILE, cw), lambda i, j: (i, j, C_MLA // cw)),
                  pl.BlockSpec((3, TILE, LANES), lambda i, j: (0, j, 0)),
                  const(1, MLA_Q_LORA), const(1, MLA_KV_LORA), const(MLA_Q_LORA, w), const(MLA_KV_LORA, w),
                  const(MLA_HEADS * MLA_V, MLA_KV_LORA), const(1, LANES), const(1, LANES)],
        out_specs=[pl.BlockSpec((1, TILE, w), lambda i, j: (i, j, 0)),
                   pl.BlockSpec((1, TILE, w), lambda i, j: (i, j, 0)),
                   pl.BlockSpec((1, MLA_HEADS * (MLA_V + V_ONES), TILE), lambda i, j: (i, 0, j))],
        out_shape=[jax.ShapeDtypeStruct((b, n_tok, w), BF16),
                   jax.ShapeDtypeStruct((b, n_tok, w), BF16),
                   jax.ShapeDtypeStruct((b, MLA_HEADS * (MLA_V + V_ONES), n_tok), BF16)],
        compiler_params=_cparams(("arbitrary", "arbitrary")),
        name="mla_prep",
    )(u, tab, qlg, kvlg, wq, wk, wvt, qng, kng)


def _diff_prep_kernel(u_ref, tab_ref, qng_ref, kng_ref, q_ref, k_ref, vt_ref):
    u = u_ref[0]
    ones_bd = _group_ones(DIFF_HD)
    perm = _rope_perm(DIFF_HD, 0, DIFF_HD)
    qg = jnp.concatenate([qng_ref[...]] * 2, axis=1)
    kg = jnp.concatenate([kng_ref[...]] * 2, axis=1)
    scale = DIFF_HD ** -0.5 * LOG2E
    for c in range(0, DIFF_QK, PAIR):
        q_ref[0, :, c:c + PAIR] = (_norm_rope(u[:, c:c + PAIR], qg, ones_bd, perm, tab_ref, DIFF_HD) * scale).astype(BF16)
        k_ref[0, :, c:c + PAIR] = _norm_rope(u[:, DIFF_QK + c:DIFF_QK + c + PAIR], kg, ones_bd, perm, tab_ref,
                                             DIFF_HD).astype(BF16)
    _store_vt(vt_ref, u[:, 2 * DIFF_QK:].T, DIFF_HEADS, 2 * DIFF_HD)


def _diff_prep(u, tab, qng, kng):
    b, n_tok, _ = u.shape
    t = n_tok // TILE
    const = lambda r, c: pl.BlockSpec((r, c), lambda i, j: (0, 0))
    return pl.pallas_call(
        _diff_prep_kernel,
        grid=(b, t),
        in_specs=[pl.BlockSpec((1, TILE, DIFF_COLS), lambda i, j: (i, j, C_DIFF // DIFF_COLS)),
                  pl.BlockSpec((3, TILE, LANES), lambda i, j: (0, j, 0)),
                  const(1, LANES), const(1, LANES)],
        out_specs=[pl.BlockSpec((1, TILE, DIFF_QK), lambda i, j: (i, j, 0)),
                   pl.BlockSpec((1, TILE, DIFF_QK), lambda i, j: (i, j, 0)),
                   pl.BlockSpec((1, DIFF_HEADS * (2 * DIFF_HD + V_ONES), TILE), lambda i, j: (i, 0, j))],
        out_shape=[jax.ShapeDtypeStruct((b, n_tok, DIFF_QK), BF16),
                   jax.ShapeDtypeStruct((b, n_tok, DIFF_QK), BF16),
                   jax.ShapeDtypeStruct((b, DIFF_HEADS * (2 * DIFF_HD + V_ONES), n_tok), BF16)],
        compiler_params=_cparams(("arbitrary", "arbitrary")),
        name="diff_prep",
    )(u, tab, qng, kng)


def _attn_kernel(q_ref, k_ref, vt_ref, *rest, n_heads, dv, diff, lam_init, first_tile):
    if diff:
        lamp_ref, subg_ref, o_ref, ot_ref, s_ref = rest
        lp = lamp_ref[...]
        lam = (jnp.exp(jnp.sum(lp[0:1] * lp[1:2], axis=-1, keepdims=True))
               - jnp.exp(jnp.sum(lp[2:3] * lp[3:4], axis=-1, keepdims=True)) + lam_init)
        lane = _iota((TILE, LANES), 1)
    else:
        o_ref, ot_ref, s_ref = rest
    tile = pl.program_id(1) + first_tile
    dva = dv + V_ONES
    n_iter = n_heads if diff else n_heads // 2

    def streams(i):
        if diff:
            hl = slice(i * LANES, (i + 1) * LANES)
            q = q_ref[0, :, hl]
            zero = jnp.zeros_like(q)
            vr = slice(i * dva, (i + 1) * dva)
            return [(hl, vr, jnp.where(lane < DIFF_HD, q, zero)), (hl, vr, jnp.where(lane < DIFF_HD, zero, q))]
        out = []
        for h in (2 * i, 2 * i + 1):
            hl = slice(h * LANES, (h + 1) * LANES)
            out.append((hl, slice(h * dva, (h + 1) * dva), q_ref[0, :, hl]))
        return out

    def finish(i, accs):
        a0, a1 = accs
        if diff:
            o = a0[:dv] / a0[dv:dv + 1] - lam * (a1[:dv] / a1[dv:dv + 1])
            ms = jnp.mean(o * o, axis=0, keepdims=True)
            o = o * lax.rsqrt(ms + NORM_EPS) * subg_ref[...] * (1.0 - lam_init)
            ot_ref[i * dv:(i + 1) * dv, :] = o
        else:
            o = jnp.concatenate([a0[:dv] / a0[dv:dv + 1], a1[:dv] / a1[dv:dv + 1]], axis=0)
            ot_ref[2 * i * dv:2 * (i + 1) * dv, :] = o

    def run(n_keys):
        blocks = range(0, n_keys, KEY_BLOCK)
        prev = prev_max = None
        for i in range(n_iter + 1):
            par = i % 2
            cur = streams(i) if i < n_iter else None
            cur_max = [None, None]
            accs = [None, None]
            for start in blocks:
                rows = slice(start, start + KEY_BLOCK)
                if cur is not None:
                    for st, (hl, _, q) in enumerate(cur):
                        s = _dot_nt(k_ref[0, rows, hl], q)
                        s_ref[par, st, rows, :] = s
                        part = jnp.max(s, axis=0, keepdims=True)
                        cur_max[st] = part if cur_max[st] is None else jnp.maximum(cur_max[st], part)
                if prev is not None:
                    for st, (_, vr, _) in enumerate(prev):
                        p = jnp.exp2(s_ref[1 - par, st, rows, :] - prev_max[st]).astype(BF16)
                        pv = _dot(vt_ref[0, vr, rows], p)
                        accs[st] = pv if accs[st] is None else accs[st] + pv
            if prev is not None:
                finish(i - 1, accs)
            prev, prev_max = cur, cur_max
        o_ref[0] = ot_ref[...].T.astype(o_ref.dtype)

    pl.when(tile == 0)(lambda: run(TILE))
    pl.when(tile > 0)(lambda: run(k_ref.shape[1]))


def _attention(q, k, vt, *, n_heads, dv, need_ctx, diff=False, lam_p=None, sub_g=None, lam_init=0.0):
    b, n_tok, w = q.shape
    t = n_tok // TILE
    first = 0 if need_ctx else 1
    ow = n_heads * dv
    in_specs = [pl.BlockSpec((1, TILE, w), lambda i, j: (i, j + first, 0)),
                pl.BlockSpec((1, n_tok, w), lambda i, j: (i, 0, 0)),
                pl.BlockSpec((1, n_heads * (dv + V_ONES), n_tok), lambda i, j: (i, 0, 0))]
    args = [q, k, vt]
    if diff:
        in_specs += [pl.BlockSpec(lam_p.shape, lambda i, j: (0, 0)), pl.BlockSpec(sub_g.shape, lambda i, j: (0, 0))]
        args += [lam_p, sub_g]
    return pl.pallas_call(
        functools.partial(_attn_kernel, n_heads=n_heads, dv=dv, diff=diff, lam_init=lam_init, first_tile=first),
        grid=(b, t - first),
        in_specs=in_specs,
        out_specs=pl.BlockSpec((1, TILE, ow), lambda i, j: (i, j + first, 0)),
        out_shape=jax.ShapeDtypeStruct((b, n_tok, ow), BF16),
        scratch_shapes=[pltpu.VMEM((ow, TILE), F32), pltpu.VMEM((2, 2, n_tok, TILE), F32)],
        compiler_params=_cparams(("arbitrary", "arbitrary")),
        name="diff_attn" if diff else "mla_attn",
    )(*args)


def _merge_kernel(x_ref, mod_ref, gof_ref, gob_ref, gz_ref, mla_ref, dif_ref, syf_ref, syb_ref, sz_ref,
                  gates01_ref, gates23_ref, gng_ref, sng_ref, wb_ref, wo_ref, o_ref):
    o = gof_ref[0] + gob_ref[0]
    z = gz_ref[0]
    ya = []
    for h in range(GDN_HEADS):
        hl = slice(h * GDN_DV, (h + 1) * GDN_DV)
        ya.append(_rms(o[:, hl], gng_ref[...]) * _silu(z[:, hl]))
    ya = jnp.concatenate(ya, axis=-1)
    y = (syf_ref[0] + syb_ref[0]) * _silu(sz_ref[0])
    gsz = SSM_INNER // SSM_GROUPS
    yd = jnp.concatenate([_rms(y[:, g * gsz:(g + 1) * gsz], sng_ref[:, g * gsz:(g + 1) * gsz])
                          for g in range(SSM_GROUPS)], axis=-1)
    ys = (ya, mla_ref[0], dif_ref[0], yd)
    m = None
    for i in range(N_BRANCH):
        gates_ref = gates01_ref if i < 2 else gates23_ref
        gate = _sigmoid(gates_ref[0, :, (i % 2) * D_MODEL:(i % 2 + 1) * D_MODEL])
        term = gate * _dot(ys[i].astype(BF16), wb_ref[i])
        m = term if m is None else m + term
    o_ref[0] = x_ref[0] + mod_ref[2:3, :] * _dot(m.astype(BF16), wo_ref[...])


def _merge(xs, mods, gdn_o, u, y_mla, y_diff, ssd_y, gng, sng, wb, wo, *, need_ctx):
    b, n_tok, d = xs.shape
    t = n_tok // TILE
    first = 0 if need_ctx else 1
    row = lambda w, cb=0: pl.BlockSpec((1, TILE, w), lambda i, j: (i, j + first, cb))
    const = lambda shape: pl.BlockSpec(shape, lambda i, j: (0,) * len(shape))
    return pl.pallas_call(
        _merge_kernel,
        grid=(b, t - first),
        in_specs=[row(d),
                  pl.BlockSpec((None, None, 6, d), lambda i, j: (i, jnp.minimum(j + first, 1), 0, 0)),
                  row(BRANCH_W), row(BRANCH_W), row(BRANCH_W, (C_GDN + 3 * BRANCH_W) // BRANCH_W),
                  row(BRANCH_W), row(BRANCH_W), row(BRANCH_W), row(BRANCH_W), row(BRANCH_W, C_SSMZ // BRANCH_W),
                  row(GATE_COLS // 2, C_GATE // (GATE_COLS // 2)), row(GATE_COLS // 2, C_GATE // (GATE_COLS // 2) + 1),
                  const((1, GDN_DV)), const((1, SSM_INNER)), const((N_BRANCH, BRANCH_W, d)), const((d, d))],
        out_specs=row(d),
        out_shape=jax.ShapeDtypeStruct((b, n_tok, d), F32),
        input_output_aliases={0: 0},
        compiler_params=_cparams(("arbitrary", "arbitrary")),
        name="merge",
    )(xs, mods, gdn_o[0], gdn_o[1], u, y_mla, y_diff, ssd_y[0], ssd_y[1], u,
      u, u, gng, sng, wb, wo)


def _mlp_kernel(x_ref, mod_ref, g_ref, w1_ref, w2_ref, o_ref):
    x = x_ref[0]
    h = _rms(x, g_ref[...]) * (1.0 + mod_ref[4:5, :]) + mod_ref[3:4, :]
    a = jnp.maximum(_dot(h.astype(BF16), w1_ref[...]), 0.0)
    o_ref[0] = x + mod_ref[5:6, :] * _dot((a * a).astype(BF16), w2_ref[...])


def _mlp(xs, mods, g, w1, w2, *, need_ctx):
    b, n_tok, d = xs.shape
    t = n_tok // TILE
    first = 0 if need_ctx else 1
    const = lambda shape: pl.BlockSpec(shape, lambda i, j: (0,) * len(shape))
    n_out = n_tok - first * TILE
    return pl.pallas_call(
        _mlp_kernel,
        grid=(b, t - first),
        in_specs=[pl.BlockSpec((1, TILE, d), lambda i, j: (i, j + first, 0)),
                  pl.BlockSpec((None, None, 6, d), lambda i, j: (i, jnp.minimum(j + first, 1), 0, 0)),
                  const((1, d)), const((d, D_FF)), const((D_FF, d))],
        out_specs=pl.BlockSpec((1, TILE, d), lambda i, j: (i, j, 0)),
        out_shape=jax.ShapeDtypeStruct((b, n_out, d), F32),
        compiler_params=_cparams(("arbitrary", "arbitrary")),
        name="mlp",
    )(xs, mods, g, w1, w2)


def _repack_w_in(w):
    d = w.shape[0]
    z = lambda n: jnp.zeros((d, n), w.dtype)
    o_mla = GDN_COLS
    o_diff = o_mla + MLA_COLS
    o_ssm = o_diff + DIFF_COLS
    o_gate = MIX_COLS
    gdn_ab = w[:, 2 * GDN_QK + 2 * GDN_VW:GDN_COLS]
    ssm_dt = w[:, o_ssm + 2 * SSM_INNER + 2 * SSM_BC:o_ssm + SSM_COLS]
    kpe = w[:, o_mla + MLA_Q_LORA + MLA_KV_LORA:o_mla + MLA_COLS]
    parts = [
        w[:, 0:2 * GDN_QK + 2 * GDN_VW],
        w[:, o_ssm + SSM_INNER:o_ssm + 2 * SSM_INNER + 2 * SSM_BC],
        w[:, o_diff:o_diff + DIFF_COLS],
        w[:, o_mla:o_mla + MLA_Q_LORA + MLA_KV_LORA], z(MLA_NOPE), kpe, z(LANES - MLA_QK),
        gdn_ab, ssm_dt, z(LANES - 4 * GDN_HEADS - 2 * SSM_HEADS),
        z(C_SSMZ - C_SMALL - LANES),
        w[:, o_ssm:o_ssm + SSM_INNER],
        w[:, o_gate:o_gate + GATE_COLS],
    ]
    out = jnp.concatenate(parts, axis=1)
    assert out.shape[1] == N_IN
    return out.astype(BF16)


def _lane_row(vals, offset):
    row = jnp.zeros((1, LANES), F32)
    return row.at[0, offset:offset + vals.shape[0]].set(vals.astype(F32))


def _rope_tables(n_lat, rot_dim, lane_offset, period):
    rows = n_lat // GRID_W
    row = jnp.repeat(jnp.arange(rows, dtype=F32), GRID_W)
    col = jnp.tile(jnp.arange(GRID_W, dtype=F32), rows)
    quarter = rot_dim // 4
    inv = ROPE_THETA ** (-jnp.arange(quarter, dtype=F32) / quarter)
    ar = row[:, None] * inv
    ac = col[:, None] * inv
    ang = jnp.concatenate([ar, ar, ac, ac], axis=-1)
    cos, sin = jnp.cos(ang), jnp.sin(ang)
    first = (np.arange(rot_dim) % (2 * quarter)) < quarter
    sin_a = jnp.where(first, -sin, 0.0)
    sin_b = jnp.where(first, 0.0, sin)
    reps = LANES // period
    def place(t, fill):
        blk = jnp.full((n_lat, period), fill, F32).at[:, lane_offset:lane_offset + rot_dim].set(t)
        blk = jnp.tile(blk, (1, reps))
        ctx = jnp.full((CTX_LEN, LANES), fill, F32)
        return jnp.concatenate([ctx, blk], axis=0)
    return jnp.stack([place(cos, 1.0), place(sin_a, 0.0), place(sin_b, 0.0)])


def kernel(x, c, ctx, c_ctx, ada_w, ada_b, norm1_g, norm2_g, w_in, gdn_conv, gdn_a_log, gdn_dt_bias, gdn_norm_g,
           mla_q_lora_g, mla_kv_lora_g, mla_w_uq, mla_w_ukv, mla_qn_g, mla_kn_g, diff_qn_g, diff_kn_g,
           diff_lambda, diff_sub_g, ssm_conv, ssm_conv_b, ssm_a_log, ssm_dt_bias, ssm_d, ssm_norm_g, w_branch,
           w_out, mlp_w1, mlp_w2):
    b, n_lat, d = x.shape
    depth = w_in.shape[0]
    assert ctx.shape[1] == CTX_LEN == TILE and n_lat % TILE == 0 and d == D_MODEL
    xs = jnp.concatenate([ctx, x], axis=1)
    tab_mla = _rope_tables(n_lat, MLA_ROPE, MLA_NOPE, LANES)
    tab_diff = _rope_tables(n_lat, DIFF_HD, 0, DIFF_HD)
    rows = b + 8
    cc = jnp.zeros((rows, d), F32).at[:b].set(c).at[b].set(c_ctx)
    row2 = lambda v: v.reshape(1, -1).astype(F32)

    for l in range(depth):
        need_ctx = l < depth - 1
        lam_init = LAMBDA_BASE - LAMBDA_AMP * math.exp(-LAMBDA_RATE * l)
        mod = _ada(cc, ada_w[l], ada_b[l].reshape(1, -1))
        mod_lat = mod[:b].reshape(b, 6, d)
        mod_ctx = jnp.broadcast_to(mod[b].reshape(1, 6, d), (b, 6, d))
        mods = jnp.stack([mod_ctx, mod_lat], axis=1)

        u = _inproj(xs, mods, row2(norm1_g[l]), _repack_w_in(w_in[l]))

        gdn_o = _gdn_scan(_gdn_prep(u, gdn_conv[l].astype(F32),
                                    _lane_row(gdn_a_log[l].reshape(-1), SM_A),
                                    _lane_row(gdn_dt_bias[l].reshape(-1), SM_A)))

        wq = mla_w_uq[l].reshape(MLA_Q_LORA, MLA_HEADS, MLA_QK)
        wq = jnp.pad(wq, ((0, 0), (0, 0), (0, LANES - MLA_QK))).reshape(MLA_Q_LORA, MLA_HEADS * LANES)
        wkv = mla_w_ukv[l].reshape(MLA_KV_LORA, MLA_HEADS, MLA_NOPE + MLA_V)
        wk = jnp.pad(wkv[:, :, :MLA_NOPE], ((0, 0), (0, 0), (0, LANES - MLA_NOPE))).reshape(MLA_KV_LORA, MLA_HEADS * LANES)
        wvt = wkv[:, :, MLA_NOPE:].reshape(MLA_KV_LORA, MLA_HEADS * MLA_V).T
        pad_g = lambda g: jnp.pad(g.astype(F32), (0, LANES - MLA_QK)).reshape(1, LANES)
        q_m, k_m, vt_m = _mla_prep(u, tab_mla, row2(mla_q_lora_g[l]), row2(mla_kv_lora_g[l]),
                                   wq.astype(BF16), wk.astype(BF16), wvt.astype(BF16),
                                   pad_g(mla_qn_g[l]), pad_g(mla_kn_g[l]))
        y_mla = _attention(q_m, k_m, vt_m, n_heads=MLA_HEADS, dv=MLA_V, need_ctx=need_ctx)

        rep_g = lambda g: jnp.tile(g.astype(F32), LANES // DIFF_HD).reshape(1, LANES)
        q_d, k_d, vt_d = _diff_prep(u, tab_diff, rep_g(diff_qn_g[l]), rep_g(diff_kn_g[l]))
        y_diff = _attention(q_d, k_d, vt_d, n_heads=DIFF_HEADS, dv=2 * DIFF_HD, need_ctx=need_ctx, diff=True,
                            lam_p=diff_lambda[l].astype(F32), sub_g=diff_sub_g[l].astype(F32).reshape(-1, 1),
                            lam_init=lam_init)

        ssd_y = _ssd(u, ssm_conv[l].astype(F32), row2(ssm_conv_b[l]),
                     _lane_row(ssm_a_log[l].reshape(-1), SM_DT), _lane_row(ssm_dt_bias[l].reshape(-1), SM_DT),
                     jnp.repeat(ssm_d[l].astype(F32), SSM_HEAD_DIM).reshape(1, -1))

        xs = _merge(xs, mods, gdn_o, u, y_mla, y_diff, ssd_y, row2(gdn_norm_g[l]), row2(ssm_norm_g[l]),
                    w_branch[l].astype(BF16), w_out[l].astype(BF16), need_ctx=need_ctx)
        xs = _mlp(xs, mods, row2(norm2_g[l]), mlp_w1[l].astype(BF16), mlp_w2[l].astype(BF16), need_ctx=need_ctx)
    return xs
```

```python
import functools
import math

import jax
import jax.numpy as jnp
import numpy as np
from jax import lax
from jax.experimental import pallas as pl
from jax.experimental.pallas import tpu as pltpu

F32 = jnp.float32
BF16 = jnp.bfloat16

D_MODEL = 1024
CTX_LEN = 256
GRID_W = 64
ROPE_THETA = 10000.0
NORM_EPS = 1e-6
CONV_K = 5
N_BRANCH = 4
D_FF = 4 * D_MODEL
GDN_HEADS = 4
GDN_DK = 128
GDN_DV = 128
MLA_HEADS = 8
MLA_NOPE = 64
MLA_ROPE = 32
MLA_V = 64
MLA_QK = MLA_NOPE + MLA_ROPE
MLA_Q_LORA = 384
MLA_KV_LORA = 256
DIFF_HEADS = 4
DIFF_HD = 64
LAMBDA_BASE = 0.8
LAMBDA_AMP = 0.6
LAMBDA_RATE = 0.3
SSM_HEADS = 8
SSM_HEAD_DIM = 64
SSM_GROUPS = 2
SSM_STATE = 128
GDN_QK = GDN_HEADS * GDN_DK
GDN_VW = GDN_HEADS * GDN_DV
DIFF_QK = DIFF_HEADS * 2 * DIFF_HD
DIFF_VW = DIFF_HEADS * 2 * DIFF_HD
SSM_INNER = SSM_HEADS * SSM_HEAD_DIM
SSM_BC = SSM_GROUPS * SSM_STATE
BRANCH_W = 512
GDN_COLS = 2 * GDN_QK + 2 * GDN_VW + 4 * GDN_HEADS
MLA_COLS = MLA_Q_LORA + MLA_KV_LORA + MLA_ROPE
DIFF_COLS = 2 * DIFF_QK + DIFF_VW
SSM_COLS = 2 * SSM_INNER + 2 * SSM_BC + 2 * SSM_HEADS
MIX_COLS = GDN_COLS + MLA_COLS + DIFF_COLS + SSM_COLS
GATE_COLS = N_BRANCH * D_MODEL

LANES = 128
TILE = 256
CHUNK = 64
CPT = TILE // CHUNK
HALO = 8
VMEM_LIMIT = 56 * 1024 * 1024
V_ONES = 16
KEY_BLOCK = 256
LOG2E = 1.4426950408889634

C_GDN = 0
C_XBC = 2048
C_DIFF = 3072
C_MLA = 4608
C_SMALL = 5376
C_SSMZ = 5632
C_GATE = 6144
N_IN = 10240
IN_TN = 2560

SM_A = 0
SM_B = 8
SM_DT = 16


def _cparams(sem):
    return pltpu.CompilerParams(dimension_semantics=sem, vmem_limit_bytes=VMEM_LIMIT)


def _dot(a, b):
    return jnp.dot(a, b, preferred_element_type=F32)


def _dot_nt(a, b):
    return lax.dot_general(a, b, (((1,), (1,)), ((), ())), preferred_element_type=F32)


def _dot_tn(a, b):
    return lax.dot_general(a, b, (((0,), (0,)), ((), ())), preferred_element_type=F32)


def _split3(x):
    x1 = x.astype(BF16)
    r1 = x - x1.astype(F32)
    x2 = r1.astype(BF16)
    r2 = r1 - x2.astype(F32)
    return x1, x2, r2.astype(BF16)


def _dot_exact_lhs(a_bf, x):
    x1, x2, x3 = _split3(x)
    return _dot(a_bf, x1) + _dot(a_bf, x2) + _dot(a_bf, x3)


def _dot_exact_rhs(x, b_bf):
    x1, x2, x3 = _split3(x)
    return _dot(x1, b_bf) + _dot(x2, b_bf) + _dot(x3, b_bf)


def _sigmoid(x):
    return 1.0 / (1.0 + jnp.exp(-x))


def _silu(x):
    return x * _sigmoid(x)


def _softplus(x):
    return jnp.maximum(x, 0.0) + jnp.log1p(jnp.exp(-jnp.abs(x)))


def _iota(shape, dim):
    return lax.broadcasted_iota(jnp.int32, shape, dim)


def _chunk_tri(n, upper):
    r = _iota((n, n), 0)
    c = _iota((n, n), 1)
    same = (r // CHUNK) == (c // CHUNK)
    tri = (r <= c) if upper else (r >= c)
    return jnp.where(same & tri, 1.0, 0.0).astype(BF16)


def _expander(n_rows, first_row, n_groups, width):
    r = _iota((n_rows, n_groups * width), 0)
    c = _iota((n_rows, n_groups * width), 1)
    return jnp.where(r == first_row + c // width, 1.0, 0.0).astype(BF16)


def _ada_kernel(c_ref, w_ref, b_ref, o_ref):
    a = _silu(c_ref[...])
    a1, a2, a3 = _split3(a)
    w1, w2, w3 = _split3(w_ref[...])
    acc = _dot(a1, w1) + (_dot(a1, w2) + _dot(a2, w1)) + (_dot(a1, w3) + _dot(a2, w2) + _dot(a3, w1))
    o_ref[...] = acc + b_ref[...]


def _ada(cc, w, b):
    rows, d = cc.shape
    n = w.shape[1]
    tn = 1536
    return pl.pallas_call(
        _ada_kernel,
        grid=(n // tn,),
        in_specs=[pl.BlockSpec((rows, d), lambda j: (0, 0)),
                  pl.BlockSpec((d, tn), lambda j: (0, j)),
                  pl.BlockSpec((1, tn), lambda j: (0, j))],
        out_specs=pl.BlockSpec((rows, tn), lambda j: (0, j)),
        out_shape=jax.ShapeDtypeStruct((rows, n), F32),
        compiler_params=_cparams(("arbitrary",)),
        name="ada",
    )(cc, w, b)


def _rms(x, g):
    ms = jnp.mean(x * x, axis=-1, keepdims=True)
    return x * lax.rsqrt(ms + NORM_EPS) * g


def _inproj_kernel(x_ref, mod_ref, g_ref, w_ref, o_ref):
    h = _rms(x_ref[0], g_ref[...])
    hb = (h * (1.0 + mod_ref[1:2, :]) + mod_ref[0:1, :]).astype(BF16)
    for c in range(0, N_IN, IN_TN):
        o_ref[0, :, c:c + IN_TN] = _dot(hb, w_ref[:, c:c + IN_TN])


def _inproj(xs, mods, g, w):
    b, n_tok, d = xs.shape
    t = n_tok // TILE
    return pl.pallas_call(
        _inproj_kernel,
        grid=(b, t),
        in_specs=[pl.BlockSpec((1, TILE, d), lambda i, j: (i, j, 0)),
                  pl.BlockSpec((None, None, 6, d), lambda i, j: (i, jnp.minimum(j, 1), 0, 0)),
                  pl.BlockSpec((1, d), lambda i, j: (0, 0)),
                  pl.BlockSpec((d, N_IN), lambda i, j: (0, 0), pipeline_mode=pl.Buffered(1))],
        out_specs=pl.BlockSpec((1, TILE, N_IN), lambda i, j: (i, j, 0)),
        out_shape=jax.ShapeDtypeStruct((b, n_tok, N_IN), F32),
        compiler_params=_cparams(("arbitrary", "arbitrary")),
        name="inproj",
    )(xs, mods, g, w)


def _halo_specs(width, col_block, n_tiles, tile_of):
    rpt = TILE // HALO
    last = n_tiles * rpt - 1
    main = pl.BlockSpec((1, TILE, width), lambda *ids: (ids[0], tile_of(*ids), col_block))
    prev = pl.BlockSpec((1, HALO, width),
                        lambda *ids: (ids[0], jnp.maximum(tile_of(*ids) * rpt - 1, 0), col_block))
    nxt = pl.BlockSpec((1, HALO, width),
                       lambda *ids: (ids[0], jnp.minimum(tile_of(*ids) * rpt + rpt, last), col_block))
    return main, prev, nxt


def _conv_tile(main_ref, prev_ref, next_ref, w_ref, ext_ref, tile, n_tiles):
    prev_ok = (tile >= 2).astype(F32)
    next_ok = jnp.logical_and(tile >= 1, tile < n_tiles - 1).astype(F32)
    ext_ref[0:HALO, :] = prev_ref[0] * prev_ok
    ext_ref[HALO:HALO + TILE, :] = main_ref[0]
    ext_ref[HALO + TILE:2 * HALO + TILE, :] = next_ref[0] * next_ok
    half = CONV_K // 2
    acc = None
    for k in range(CONV_K):
        term = w_ref[k:k + 1, :] * ext_ref[HALO - half + k:HALO - half + k + TILE, :]
        acc = term if acc is None else acc + term
    return acc


def _block_inverse_many(mats):
    n = mats[0].shape[0]
    ri = _iota((n, n), 0)
    ci = _iota((n, n), 1)
    eye = jnp.where(ri == ci, 1.0, 0.0).astype(F32)
    pair = (ri // 2) == (ci // 2)
    ts = [eye - jnp.where(pair, a, 0.0) for a in mats]
    abs_ = [a.astype(BF16) for a in mats]
    zero = jnp.zeros((n, n), BF16)
    size = 2
    while size < CHUNK:
        off = ((ri // (2 * size)) == (ci // (2 * size))) & ((ri // size) != (ci // size))
        tbs = [t.astype(BF16) for t in ts]
        xs = [_dot(jnp.where(off, ab, zero), tb).astype(BF16) for ab, tb in zip(abs_, tbs)]
        ts = [t - _dot(tb, x) for t, tb, x in zip(ts, tbs, xs)]
        size *= 2
    return ts


def _gdn_prep_kernel(main_ref, prev_ref, next_ref, sm_ref, cw_ref, alog_ref, dtb_ref,
                     u0_ref, wq0_ref, kd0_ref, at0_ref, gl0_ref,
                     u1_ref, wq1_ref, kd1_ref, at1_ref, gl1_ref,
                     ext_ref, *, n_tiles):
    tile = pl.program_id(1)
    qkv = _silu(_conv_tile(main_ref, prev_ref, next_ref, cw_ref, ext_ref, tile, n_tiles))
    sm = sm_ref[0]
    g_all = -jnp.exp(alog_ref[...]) * _softplus(sm + dtb_ref[...])
    beta_all = _sigmoid(sm)
    nh = GDN_HEADS
    exp_g = _expander(LANES, SM_A, 2 * nh, LANES)
    exp_b = _expander(LANES, SM_B, 2 * nh, LANES)
    beta_x = _dot_exact_rhs(beta_all, exp_b)
    outs = ((u0_ref, wq0_ref, kd0_ref, at0_ref, gl0_ref), (u1_ref, wq1_ref, kd1_ref, at1_ref, gl1_ref))
    ri = _iota((TILE, TILE), 0)
    ci = _iota((TILE, TILE), 1)
    same = (ri // CHUNK) == (ci // CHUNK)
    cums, masks = [], []
    for d in range(2):
        cum = _dot_exact_lhs(_chunk_tri(TILE, upper=(d == 1)), g_all)
        cums.append((_dot_exact_rhs(cum, exp_g), cum.T))
        masks.append((same & ((ri >= ci) if d == 0 else (ri <= ci)), same & ((ri > ci) if d == 0 else (ri < ci))))

    chains, a_mats = [], []
    for h in range(nh):
        q = qkv[:, h * GDN_DK:(h + 1) * GDN_DK]
        k = qkv[:, GDN_QK + h * GDN_DK:GDN_QK + (h + 1) * GDN_DK]
        v = qkv[:, 2 * GDN_QK + h * GDN_DV:2 * GDN_QK + (h + 1) * GDN_DV]
        q = q * lax.rsqrt(jnp.sum(q * q, axis=-1, keepdims=True) + 1e-6) * (GDN_DK ** -0.5)
        k = k * lax.rsqrt(jnp.sum(k * k, axis=-1, keepdims=True) + 1e-6)
        kb = k.astype(BF16)
        kq = _dot_nt(jnp.concatenate([kb, q.astype(BF16)], axis=0), kb)
        for d in range(2):
            col = d * nh + h
            lanes = slice(col * LANES, (col + 1) * LANES)
            cum_x, cum_t = cums[d]
            incl, strict = masks[d]
            g_col = cum_x[:, lanes]
            b_col = beta_x[:, lanes]
            diff = jnp.concatenate([g_col, g_col], axis=1) - cum_t[SM_A + col:SM_A + col + 1, :]
            decay = jnp.where(incl, jnp.exp(jnp.where(incl, diff, 0.0)), 0.0)
            a_mats.append(jnp.where(strict, kq[:TILE] * jnp.concatenate([b_col, b_col], axis=1) * decay, 0.0))
            e_g = jnp.exp(g_col)
            rhs = jnp.concatenate([v * b_col, k * b_col * e_g], axis=1).astype(BF16)
            chains.append((h, d, q, k, g_col, e_g, rhs, kq[TILE:] * decay))

    t_mats = _block_inverse_many(a_mats)
    for (h, d, q, k, g_col, e_g, rhs, attn), t_mat in zip(chains, t_mats):
        uw = _dot(t_mat.astype(BF16), rhs)
        qg = (q * e_g).astype(BF16)
        u_ref, wq_ref, kd_ref, at_ref, gl_ref = outs[d]
        hl = slice(h * LANES, (h + 1) * LANES)
        for c in range(CPT):
            rows = slice(c * CHUNK, (c + 1) * CHUNK)
            last = c * CHUNK + (CHUNK - 1 if d == 0 else 0)
            g_last = g_col[last:last + 1, :]
            slot = c if d == 0 else CPT - 1 - c
            u_ref[0, 0, slot, :, hl] = uw[rows, :GDN_DV]
            wq_ref[0, 0, slot, 0:CHUNK, hl] = uw[rows, GDN_DV:].astype(BF16)
            wq_ref[0, 0, slot, CHUNK:2 * CHUNK, hl] = qg[rows]
            kd_ref[0, 0, slot, :, hl] = (k[rows] * jnp.exp(g_last - g_col[rows])).astype(BF16)
            at_ref[0, 0, slot, :, h * CHUNK:(h + 1) * CHUNK] = attn[rows, rows].astype(BF16)
            gl_ref[0, 0, slot, :, hl] = jnp.exp(g_last)


def _bwd_block(j, n_tiles):
    return jnp.where(j == 0, 0, n_tiles - j)


def _gdn_prep(u, conv_w, alog_row, dtb_row):
    b, n_tok, _ = u.shape
    t = n_tok // TILE
    w = GDN_VW
    cw = 2 * GDN_QK + GDN_VW
    main, prev, nxt = _halo_specs(cw, 0, t, lambda i, j: j)
    fwd = lambda i, j: (i, j, 0, 0, 0)
    bwd = lambda i, j: (i, _bwd_block(j, t), 0, 0, 0)

    def outs(imap):
        return [pl.BlockSpec((1, 1, CPT, CHUNK, w), imap),
                pl.BlockSpec((1, 1, CPT, 2 * CHUNK, w), imap),
                pl.BlockSpec((1, 1, CPT, CHUNK, w), imap),
                pl.BlockSpec((1, 1, CPT, CHUNK, GDN_HEADS * CHUNK), imap),
                pl.BlockSpec((1, 1, CPT, 1, w), imap)]

    shapes = [jax.ShapeDtypeStruct((b, t, CPT, CHUNK, w), F32),
              jax.ShapeDtypeStruct((b, t, CPT, 2 * CHUNK, w), BF16),
              jax.ShapeDtypeStruct((b, t, CPT, CHUNK, w), BF16),
              jax.ShapeDtypeStruct((b, t, CPT, CHUNK, GDN_HEADS * CHUNK), BF16),
              jax.ShapeDtypeStruct((b, t, CPT, 1, w), F32)]
    return pl.pallas_call(
        functools.partial(_gdn_prep_kernel, n_tiles=t),
        grid=(b, t),
        in_specs=[main, prev, nxt,
                  pl.BlockSpec((1, TILE, LANES), lambda i, j: (i, j, C_SMALL // LANES)),
                  pl.BlockSpec((CONV_K, cw), lambda i, j: (0, 0)),
                  pl.BlockSpec((1, LANES), lambda i, j: (0, 0)),
                  pl.BlockSpec((1, LANES), lambda i, j: (0, 0))],
        out_specs=outs(fwd) + outs(bwd),
        out_shape=shapes + shapes,
        scratch_shapes=[pltpu.VMEM((TILE + 2 * HALO, cw), F32)],
        compiler_params=_cparams(("arbitrary", "arbitrary")),
        name="gdn_prep",
    )(u, u, u, u, conv_w, alog_row, dtb_row)


def _gdn_scan_kernel(u0_ref, wq0_ref, kd0_ref, at0_ref, gl0_ref,
                     u1_ref, wq1_ref, kd1_ref, at1_ref, gl1_ref,
                     of_ref, ob_ref, s_ref):
    @pl.when(pl.program_id(1) == 0)
    def _():
        s_ref[...] = jnp.zeros_like(s_ref)

    ins = ((u0_ref, wq0_ref, kd0_ref, at0_ref, gl0_ref), (u1_ref, wq1_ref, kd1_ref, at1_ref, gl1_ref))
    chains = [(d, h) for d in range(2) for h in range(GDN_HEADS)]
    hls = [slice(h * LANES, (h + 1) * LANES) for _, h in chains]
    states = [s_ref[d, h] for d, h in chains]
    for slot in range(CPT):
        rs = [_dot(ins[d][1][0, 0, slot, :, hl], s.astype(BF16)) for (d, _), hl, s in zip(chains, hls, states)]
        vbs = [(ins[d][0][0, 0, slot, :, hl] - r[0:CHUNK]).astype(BF16) for (d, _), hl, r in zip(chains, hls, rs)]
        outs = [r[CHUNK:2 * CHUNK] + _dot(ins[d][3][0, 0, slot, :, h * CHUNK:(h + 1) * CHUNK], vb)
                for (d, h), r, vb in zip(chains, rs, vbs)]
        states = [s * ins[d][4][0, 0, slot, :, hl] + _dot_tn(ins[d][2][0, 0, slot, :, hl], vb)
                  for (d, _), hl, s, vb in zip(chains, hls, states, vbs)]
        for (d, _), hl, o in zip(chains, hls, outs):
            c = slot if d == 0 else CPT - 1 - slot
            (of_ref if d == 0 else ob_ref)[0, c * CHUNK:(c + 1) * CHUNK, hl] = o
    for (d, h), s in zip(chains, states):
        s_ref[d, h] = s


def _gdn_scan(prep):
    b, t = prep[0].shape[:2]
    w = GDN_VW
    imap = lambda i, j: (i, j, 0, 0, 0)
    specs = [pl.BlockSpec((1, 1, CPT, CHUNK, w), imap),
             pl.BlockSpec((1, 1, CPT, 2 * CHUNK, w), imap),
             pl.BlockSpec((1, 1, CPT, CHUNK, w), imap),
             pl.BlockSpec((1, 1, CPT, CHUNK, GDN_HEADS * CHUNK), imap),
             pl.BlockSpec((1, 1, CPT, 1, w), imap)]
    return pl.pallas_call(
        _gdn_scan_kernel,
        grid=(b, t),
        in_specs=specs + specs,
        out_specs=[pl.BlockSpec((1, TILE, w), lambda i, j: (i, j, 0)),
                   pl.BlockSpec((1, TILE, w), lambda i, j: (i, _bwd_block(j, t), 0))],
        out_shape=[jax.ShapeDtypeStruct((b, t * TILE, w), F32)] * 2,
        scratch_shapes=[pltpu.VMEM((2, GDN_HEADS, GDN_DK, GDN_DV), F32)],
        compiler_params=_cparams(("arbitrary", "arbitrary")),
        name="gdn_scan",
    )(*prep)


def _ssd_direction(d, xbc, sm, y_ref, st_ref, alog_ref, dtb_ref, dskip_ref):
    nh, hd, ng = SSM_HEADS, SSM_HEAD_DIM, SSM_GROUPS
    hpg = nh // ng
    gw = hpg * hd
    xs = xbc[:, :SSM_INNER]
    dt_all = _softplus(sm + dtb_ref[...])
    da_all = dt_all * (-jnp.exp(alog_ref[...]))
    cum = _dot_exact_lhs(_chunk_tri(TILE, upper=(d == 1)), da_all)
    cum_t = cum.T
    expand = _expander(LANES, SM_DT + d * nh, nh, hd)
    dt_x = _dot_exact_rhs(dt_all, expand)
    cum_x = _dot_exact_rhs(cum, expand)
    xdt = xs * dt_x
    ri = _iota((CHUNK, gw), 0)
    ci = _iota((CHUNK, gw), 1) % CHUNK
    incl = (ri >= ci) if d == 0 else (ri <= ci)
    head_diag = (_iota((gw, gw), 0) // hd) == (_iota((gw, gw), 1) // hd)
    zero_bf = jnp.zeros((gw, gw), BF16)
    for step in range(CPT):
        c = step if d == 0 else CPT - 1 - step
        rows = slice(c * CHUNK, (c + 1) * CHUNK)
        last = c * CHUNK + (CHUNK - 1 if d == 0 else 0)
        cum_c = cum_x[rows]
        cum_last = cum_x[last:last + 1]
        decay_states = jnp.exp(cum_last - cum_c)
        in_decay = jnp.exp(cum_c)
        chunk_decay = jnp.exp(cum_last)
        for g in range(ng):
            gl = slice(g * gw, (g + 1) * gw)
            bm = xbc[rows, SSM_INNER + g * SSM_STATE:SSM_INNER + (g + 1) * SSM_STATE].astype(BF16)
            cm = xbc[rows, SSM_INNER + SSM_BC + g * SSM_STATE:SSM_INNER + SSM_BC + (g + 1) * SSM_STATE].astype(BF16)
            state = st_ref[d, g]
            y_off = _dot(cm, state.astype(BF16)) * in_decay[:, gl]
            xdt_c = xdt[rows, gl]
            st_ref[d, g] = state * chunk_decay[:, gl] + _dot_tn(bm, (xdt_c * decay_states[:, gl]).astype(BF16))
            scores = _dot_nt(cm, jnp.concatenate([bm] * hpg, axis=0))
            col0 = SM_DT + d * nh + g * hpg
            row_terms = jnp.concatenate([cum_t[col0 + hh:col0 + hh + 1, rows] for hh in range(hpg)], axis=1)
            seg = cum_c[:, gl] - row_terms
            lmat = jnp.where(incl, jnp.exp(jnp.where(incl, seg, 0.0)), 0.0)
            x_bd = jnp.where(head_diag, jnp.concatenate([xdt_c.astype(BF16)] * hpg, axis=0), zero_bf)
            y = _dot((scores * lmat).astype(BF16), x_bd) + y_off
            if d == 0:
                y = y + dskip_ref[:, gl] * xs[rows, gl]
            y_ref[0, rows, gl] = y


def _ssd_kernel(m_ref, p_ref, n_ref, sm_ref, cw_ref, cb_ref, alog_ref, dtb_ref, dskip_ref,
                yf_ref, yb_ref, ext_ref, st_ref, cache_ref, *, n_tiles):
    s = pl.program_id(1)

    @pl.when(s == 0)
    def _():
        st_ref[...] = jnp.zeros_like(st_ref)

    @pl.when(s < n_tiles)
    def _():
        xbc = _silu(_conv_tile(m_ref, p_ref, n_ref, cw_ref, ext_ref, s, n_tiles) + cb_ref[...])
        cache_ref[s] = xbc
        _ssd_direction(0, xbc, sm_ref[0], yf_ref, st_ref, alog_ref, dtb_ref, dskip_ref)

    @pl.when(s >= n_tiles)
    def _():
        tile = _bwd_block(s - n_tiles, n_tiles)
        _ssd_direction(1, cache_ref[tile], sm_ref[0], yb_ref, st_ref, alog_ref, dtb_ref, dskip_ref)


def _ssd(u, conv_w, conv_b, alog_row, dtb_row, dskip_row):
    b, n_tok, _ = u.shape
    t = n_tok // TILE
    cw = SSM_INNER + 2 * SSM_BC
    main, prev, nxt = _halo_specs(cw, C_XBC // cw, t, lambda i, s: jnp.minimum(s, t - 1))
    tile_of = lambda s: jnp.where(s < t, s, _bwd_block(s - t, t))
    const = lambda r, c: pl.BlockSpec((r, c), lambda i, s: (0, 0))
    return pl.pallas_call(
        functools.partial(_ssd_kernel, n_tiles=t),
        grid=(b, 2 * t),
        in_specs=[main, prev, nxt,
                  pl.BlockSpec((1, TILE, LANES), lambda i, s: (i, tile_of(s), C_SMALL // LANES)),
                  const(CONV_K, cw), const(1, cw), const(1, LANES), const(1, LANES), const(1, SSM_INNER)],
        out_specs=[pl.BlockSpec((1, TILE, SSM_INNER), lambda i, s: (i, jnp.minimum(s, t - 1), 0)),
                   pl.BlockSpec((1, TILE, SSM_INNER), lambda i, s: (i, jnp.where(s < t, 0, _bwd_block(s - t, t)), 0))],
        out_shape=[jax.ShapeDtypeStruct((b, n_tok, SSM_INNER), F32)] * 2,
        scratch_shapes=[pltpu.VMEM((TILE + 2 * HALO, cw), F32),
                        pltpu.VMEM((2, SSM_GROUPS, SSM_STATE, SSM_INNER // SSM_GROUPS), F32),
                        pltpu.VMEM((t, TILE, cw), F32)],
        compiler_params=_cparams(("arbitrary", "arbitrary")),
        name="ssd",
    )(u, u, u, u, conv_w, conv_b, alog_row, dtb_row, dskip_row)


PAIR = 2 * LANES


def _group_ones(group):
    r = _iota((PAIR, PAIR), 0)
    c = _iota((PAIR, PAIR), 1)
    return jnp.where((r // group) == (c // group), 1.0, 0.0).astype(BF16)


def _rope_perm(period, lane_offset, rot_dim):
    r = _iota((PAIR, PAIR), 0)
    c = _iota((PAIR, PAIR), 1)
    q = rot_dim // 4
    cl = c % period - lane_offset
    src = jnp.where((cl % (2 * q)) < q, c + q, c - q)
    return jnp.where((cl >= 0) & (cl < rot_dim) & (r == src), 1.0, 0.0).astype(BF16)


def _norm_rope(x, gain, ones_bd, perm, tab_ref, n_real):
    ss = _dot((x * x).astype(BF16), ones_bd)
    xn = x * lax.rsqrt(ss / n_real + NORM_EPS) * gain
    hi = xn.astype(BF16)
    lo = (xn - hi.astype(F32)).astype(BF16)
    rot = _dot(hi, perm) + _dot(lo, perm)
    cos = jnp.concatenate([tab_ref[0]] * 2, axis=1)
    sin = jnp.concatenate([tab_ref[1] + tab_ref[2]] * 2, axis=1)
    return xn * cos + rot * sin


def _store_vt(vt_ref, vt, n_heads, dv):
    ones = jnp.ones((V_ONES, vt.shape[1]), BF16)
    for h in range(n_heads):
        base = h * (dv + V_ONES)
        vt_ref[0, base:base + dv, :] = vt[h * dv:(h + 1) * dv].astype(BF16)
        vt_ref[0, base + dv:base + dv + V_ONES, :] = ones


def _mla_prep_kernel(u_ref, tab_ref, qlg_ref, kvlg_ref, wq_ref, wk_ref, wvt_ref, qng_ref, kng_ref,
                     q_ref, k_ref, vt_ref):
    u = u_ref[0]
    cq = _rms(u[:, :MLA_Q_LORA], qlg_ref[...]).astype(BF16)
    ckv = _rms(u[:, MLA_Q_LORA:MLA_Q_LORA + MLA_KV_LORA], kvlg_ref[...]).astype(BF16)
    kpe = u[:, MLA_Q_LORA + MLA_KV_LORA:]
    q_all = _dot(cq, wq_ref[...])
    k_all = _dot(ckv, wk_ref[...])
    _store_vt(vt_ref, _dot_nt(wvt_ref[...], ckv), MLA_HEADS, MLA_V)
    scale = MLA_QK ** -0.5 * LOG2E
    ones_bd = _group_ones(LANES)
    perm = _rope_perm(LANES, MLA_NOPE, MLA_ROPE)
    qg = jnp.concatenate([qng_ref[...]] * 2, axis=1)
    kg = jnp.concatenate([kng_ref[...]] * 2, axis=1)
    kpe2 = jnp.concatenate([kpe, kpe], axis=1)
    for h in range(0, MLA_HEADS, 2):
        hl = slice(h * LANES, (h + 2) * LANES)
        q_ref[0, :, hl] = (_norm_rope(q_all[:, hl], qg, ones_bd, perm, tab_ref, MLA_QK) * scale).astype(BF16)
        k_ref[0, :, hl] = _norm_rope(k_all[:, hl] + kpe2, kg, ones_bd, perm, tab_ref, MLA_QK).astype(BF16)


def _mla_prep(u, tab, qlg, kvlg, wq, wk, wvt, qng, kng):
    b, n_tok, _ = u.shape
    t = n_tok // TILE
    w = MLA_HEADS * LANES
    cw = MLA_Q_LORA + MLA_KV_LORA + LANES
    const = lambda r, c: pl.BlockSpec((r, c), lambda i, j: (0, 0))
    return pl.pallas_call(
        _mla_prep_kernel,
        grid=(b, t),
        in_specs=[pl.BlockSpec((1, TILE, cw), lambda i, j: (i, j, C_MLA // cw)),
                  pl.BlockSpec((3, TILE, LANES), lambda i, j: (0, j, 0)),
                  const(1, MLA_Q_LORA), const(1, MLA_KV_LORA), const(MLA_Q_LORA, w), const(MLA_KV_LORA, w),
                  const(MLA_HEADS * MLA_V, MLA_KV_LORA), const(1, LANES), const(1, LANES)],
        out_specs=[pl.BlockSpec((1, TILE, w), lambda i, j: (i, j, 0)),
                   pl.BlockSpec((1, TILE, w), lambda i, j: (i, j, 0)),
                   pl.BlockSpec((1, MLA_HEADS * (MLA_V + V_ONES), TILE), lambda i, j: (i, 0, j))],
        out_shape=[jax.ShapeDtypeStruct((b, n_tok, w), BF16),
                   jax.ShapeDtypeStruct((b, n_tok, w), BF16),
                   jax.ShapeDtypeStruct((b, MLA_HEADS * (MLA_V + V_ONES), n_tok), BF16)],
        compiler_params=_cparams(("arbitrary", "arbitrary")),
        name="mla_prep",
    )(u, tab, qlg, kvlg, wq, wk, wvt, qng, kng)


def _diff_prep_kernel(u_ref, tab_ref, qng_ref, kng_ref, q_ref, k_ref, vt_ref):
    u = u_ref[0]
    ones_bd = _group_ones(DIFF_HD)
    perm = _rope_perm(DIFF_HD, 0, DIFF_HD)
    qg = jnp.concatenate([qng_ref[...]] * 2, axis=1)
    kg = jnp.concatenate([kng_ref[...]] * 2, axis=1)
    scale = DIFF_HD ** -0.5 * LOG2E
    for c in range(0, DIFF_QK, PAIR):
        q_ref[0, :, c:c + PAIR] = (_norm_rope(u[:, c:c + PAIR], qg, ones_bd, perm, tab_ref, DIFF_HD) * scale).astype(BF16)
        k_ref[0, :, c:c + PAIR] = _norm_rope(u[:, DIFF_QK + c:DIFF_QK + c + PAIR], kg, ones_bd, perm, tab_ref,
                                             DIFF_HD).astype(BF16)
    _store_vt(vt_ref, u[:, 2 * DIFF_QK:].T, DIFF_HEADS, 2 * DIFF_HD)


def _diff_prep(u, tab, qng, kng):
    b, n_tok, _ = u.shape
    t = n_tok // TILE
    const = lambda r, c: pl.BlockSpec((r, c), lambda i, j: (0, 0))
    return pl.pallas_call(
        _diff_prep_kernel,
        grid=(b, t),
        in_specs=[pl.BlockSpec((1, TILE, DIFF_COLS), lambda i, j: (i, j, C_DIFF // DIFF_COLS)),
                  pl.BlockSpec((3, TILE, LANES), lambda i, j: (0, j, 0)),
                  const(1, LANES), const(1, LANES)],
        out_specs=[pl.BlockSpec((1, TILE, DIFF_QK), lambda i, j: (i, j, 0)),
                   pl.BlockSpec((1, TILE, DIFF_QK), lambda i, j: (i, j, 0)),
                   pl.BlockSpec((1, DIFF_HEADS * (2 * DIFF_HD + V_ONES), TILE), lambda i, j: (i, 0, j))],
        out_shape=[jax.ShapeDtypeStruct((b, n_tok, DIFF_QK), BF16),
                   jax.ShapeDtypeStruct((b, n_tok, DIFF_QK), BF16),
                   jax.ShapeDtypeStruct((b, DIFF_HEADS * (2 * DIFF_HD + V_ONES), n_tok), BF16)],
        compiler_params=_cparams(("arbitrary", "arbitrary")),
        name="diff_prep",
    )(u, tab, qng, kng)


def _attn_kernel(q_ref, k_ref, vt_ref, *rest, n_heads, dv, diff, lam_init, n_q_tiles):
    rest = list(rest)
    if n_q_tiles is None:
        rest.pop(0)
    if diff:
        lamp_ref, subg_ref = rest[:2]
        rest = rest[2:]
        lp = lamp_ref[...]
        lam = (jnp.exp(jnp.sum(lp[0:1] * lp[1:2], axis=-1, keepdims=True))
               - jnp.exp(jnp.sum(lp[2:3] * lp[3:4], axis=-1, keepdims=True)) + lam_init)
        lane = _iota((TILE, LANES), 1)
    o_ref, ot_ref, s_ref = rest[:3]
    dva = dv + V_ONES
    n_iter = n_heads if diff else n_heads // 2
    n_keys = k_ref.shape[1]
    blocks = [(0, TILE)] + [(b0, KEY_BLOCK) for b0 in range(TILE, n_keys, KEY_BLOCK)]

    def streams(i):
        if diff:
            hl = slice(i * LANES, (i + 1) * LANES)
            q = q_ref[0, :, hl]
            zero = jnp.zeros_like(q)
            vr = slice(i * dva, (i + 1) * dva)
            return [(hl, vr, jnp.where(lane < DIFF_HD, q, zero)), (hl, vr, jnp.where(lane < DIFF_HD, zero, q))]
        out = []
        for h in (2 * i, 2 * i + 1):
            hl = slice(h * LANES, (h + 1) * LANES)
            out.append((hl, slice(h * dva, (h + 1) * dva), q_ref[0, :, hl]))
        return out

    def v_rows(i):
        if diff:
            return [slice(i * dva, (i + 1) * dva)] * 2
        return [slice(h * dva, (h + 1) * dva) for h in (2 * i, 2 * i + 1)]

    def finish(i, accs):
        a0, a1 = accs
        if diff:
            o = a0[:dv] / a0[dv:dv + 1] - lam * (a1[:dv] / a1[dv:dv + 1])
            ms = jnp.mean(o * o, axis=0, keepdims=True)
            o = o * lax.rsqrt(ms + NORM_EPS) * subg_ref[...] * (1.0 - lam_init)
            ot_ref[i * dv:(i + 1) * dv, :] = o
        else:
            o = jnp.concatenate([a0[:dv] / a0[dv:dv + 1], a1[:dv] / a1[dv:dv + 1]], axis=0)
            ot_ref[2 * i * dv:2 * (i + 1) * dv, :] = o

    def phase(cur, cur_par, prev_rows, prev_par, prev_max):
        cur_max = [None, None]
        accs = [None, None]
        for start, size in blocks:
            rows = slice(start, start + size)
            if cur is not None:
                for st, (hl, _, q) in enumerate(cur):
                    s = _dot_nt(k_ref[0, rows, hl], q)
                    s_ref[cur_par, st, rows, :] = s
                    part = jnp.max(s, axis=0, keepdims=True)
                    cur_max[st] = part if cur_max[st] is None else jnp.maximum(cur_max[st], part)
            if prev_rows is not None:
                for st, vr in enumerate(prev_rows):
                    p = jnp.exp2(s_ref[prev_par, st, rows, :] - prev_max[st]).astype(BF16)
                    pv = _dot(vt_ref[0, vr, rows], p)
                    accs[st] = pv if accs[st] is None else accs[st] + pv
        return cur_max, accs

    def middle(first_max):
        prev_max = first_max
        for i in range(1, n_iter):
            cur_max, accs = phase(streams(i), i % 2, v_rows(i - 1), (i - 1) % 2, prev_max)
            finish(i - 1, accs)
            prev_max = cur_max
        return prev_max

    last_par = (n_iter - 1) % 2
    if n_q_tiles is None:
        last_max = middle(phase(streams(0), 0, None, 0, None)[0])
        finish(n_iter - 1, phase(None, 0, v_rows(n_iter - 1), last_par, last_max)[1])
        o_ref[0] = ot_ref[...].T.astype(o_ref.dtype)
        return

    mx_ref = rest[3]
    j = pl.program_id(1)

    @pl.when(j == 0)
    def _():
        s_ref[last_par] = jnp.zeros(s_ref.shape[1:], F32)
        mx_ref[...] = jnp.zeros_like(mx_ref)
        ot_ref[...] = jnp.zeros_like(ot_ref)

    first_max, accs = phase(streams(0), 0, v_rows(n_iter - 1), last_par, [mx_ref[0], mx_ref[1]])
    finish(n_iter - 1, accs)
    o_ref[0] = ot_ref[...].T.astype(o_ref.dtype)

    @pl.when(j < n_q_tiles)
    def _():
        last_max = middle(first_max)
        mx_ref[0] = last_max[0]
        mx_ref[1] = last_max[1]


def _attention(q, k, vt, *, n_heads, dv, need_ctx, diff=False, lam_p=None, sub_g=None, lam_init=0.0):
    b, n_tok, w = q.shape
    nq = n_tok // TILE - 1
    ow = n_heads * dv
    vrows = n_heads * (dv + V_ONES)
    n_iter = n_heads if diff else n_heads // 2
    assert n_iter % 2 == 0 and (n_tok - TILE) % KEY_BLOCK == 0
    extra_specs, extra_args = [], []
    if diff:
        extra_specs = [pl.BlockSpec(lam_p.shape, lambda i, j: (0, 0)), pl.BlockSpec(sub_g.shape, lambda i, j: (0, 0))]
        extra_args = [lam_p, sub_g]
    kern = functools.partial(_attn_kernel, n_heads=n_heads, dv=dv, diff=diff, lam_init=lam_init)
    name = "diff_attn" if diff else "mla_attn"
    y = pl.pallas_call(
        functools.partial(kern, n_q_tiles=nq),
        grid=(b, nq + 1),
        in_specs=[pl.BlockSpec((1, TILE, w), lambda i, j: (i, jnp.minimum(j, nq - 1) + 1, 0)),
                  pl.BlockSpec((1, n_tok, w), lambda i, j: (i, 0, 0)),
                  pl.BlockSpec((1, vrows, n_tok), lambda i, j: (i, 0, 0))] + extra_specs,
        out_specs=pl.BlockSpec((1, TILE, ow), lambda i, j: (i, jnp.maximum(j - 1, 0) + 1, 0)),
        out_shape=jax.ShapeDtypeStruct((b, n_tok, ow), BF16),
        scratch_shapes=[pltpu.VMEM((ow, TILE), F32), pltpu.VMEM((2, 2, n_tok, TILE), F32),
                        pltpu.VMEM((2, 1, TILE), F32)],
        compiler_params=_cparams(("arbitrary", "arbitrary")),
        name=name,
    )(q, k, vt, *extra_args)
    if not need_ctx:
        return y
    return pl.pallas_call(
        functools.partial(kern, n_q_tiles=None),
        grid=(b,),
        in_specs=[pl.BlockSpec((1, TILE, w), lambda i: (i, 0, 0)),
                  pl.BlockSpec((1, TILE, w), lambda i: (i, 0, 0)),
                  pl.BlockSpec((1, vrows, TILE), lambda i: (i, 0, 0)),
                  pl.BlockSpec(memory_space=pl.ANY)]
                 + [pl.BlockSpec(s.block_shape, lambda i: (0, 0)) for s in extra_specs],
        out_specs=pl.BlockSpec((1, TILE, ow), lambda i: (i, 0, 0)),
        out_shape=jax.ShapeDtypeStruct((b, n_tok, ow), BF16),
        input_output_aliases={3: 0},
        scratch_shapes=[pltpu.VMEM((ow, TILE), F32), pltpu.VMEM((2, 2, TILE, TILE), F32)],
        compiler_params=_cparams(("arbitrary",)),
        name=name + "_ctx",
    )(q, k, vt, y, *extra_args)


def _merge_kernel(x_ref, mod_ref, gof_ref, gob_ref, gz_ref, mla_ref, dif_ref, syf_ref, syb_ref, sz_ref,
                  gates01_ref, gates23_ref, gng_ref, sng_ref, wb_ref, wo_ref, o_ref):
    o = gof_ref[0] + gob_ref[0]
    z = gz_ref[0]
    ya = []
    for h in range(GDN_HEADS):
        hl = slice(h * GDN_DV, (h + 1) * GDN_DV)
        ya.append(_rms(o[:, hl], gng_ref[...]) * _silu(z[:, hl]))
    ya = jnp.concatenate(ya, axis=-1)
    y = (syf_ref[0] + syb_ref[0]) * _silu(sz_ref[0])
    gsz = SSM_INNER // SSM_GROUPS
    yd = jnp.concatenate([_rms(y[:, g * gsz:(g + 1) * gsz], sng_ref[:, g * gsz:(g + 1) * gsz])
                          for g in range(SSM_GROUPS)], axis=-1)
    ys = (ya, mla_ref[0], dif_ref[0], yd)
    m = None
    for i in range(N_BRANCH):
        gates_ref = gates01_ref if i < 2 else gates23_ref
        gate = _sigmoid(gates_ref[0, :, (i % 2) * D_MODEL:(i % 2 + 1) * D_MODEL])
        term = gate * _dot(ys[i].astype(BF16), wb_ref[i])
        m = term if m is None else m + term
    o_ref[0] = x_ref[0] + mod_ref[2:3, :] * _dot(m.astype(BF16), wo_ref[...])


def _merge(xs, mods, gdn_o, u, y_mla, y_diff, ssd_y, gng, sng, wb, wo, *, need_ctx):
    b, n_tok, d = xs.shape
    t = n_tok // TILE
    first = 0 if need_ctx else 1
    row = lambda w, cb=0: pl.BlockSpec((1, TILE, w), lambda i, j: (i, j + first, cb))
    const = lambda shape: pl.BlockSpec(shape, lambda i, j: (0,) * len(shape))
    return pl.pallas_call(
        _merge_kernel,
        grid=(b, t - first),
        in_specs=[row(d),
                  pl.BlockSpec((None, None, 6, d), lambda i, j: (i, jnp.minimum(j + first, 1), 0, 0)),
                  row(BRANCH_W), row(BRANCH_W), row(BRANCH_W, (C_GDN + 3 * BRANCH_W) // BRANCH_W),
                  row(BRANCH_W), row(BRANCH_W), row(BRANCH_W), row(BRANCH_W), row(BRANCH_W, C_SSMZ // BRANCH_W),
                  row(GATE_COLS // 2, C_GATE // (GATE_COLS // 2)), row(GATE_COLS // 2, C_GATE // (GATE_COLS // 2) + 1),
                  const((1, GDN_DV)), const((1, SSM_INNER)), const((N_BRANCH, BRANCH_W, d)), const((d, d))],
        out_specs=row(d),
        out_shape=jax.ShapeDtypeStruct((b, n_tok, d), F32),
        input_output_aliases={0: 0},
        compiler_params=_cparams(("arbitrary", "arbitrary")),
        name="merge",
    )(xs, mods, gdn_o[0], gdn_o[1], u, y_mla, y_diff, ssd_y[0], ssd_y[1], u,
      u, u, gng, sng, wb, wo)


def _mlp_kernel(x_ref, mod_ref, g_ref, w1_ref, w2_ref, o_ref):
    x = x_ref[0]
    h = _rms(x, g_ref[...]) * (1.0 + mod_ref[4:5, :]) + mod_ref[3:4, :]
    a = jnp.maximum(_dot(h.astype(BF16), w1_ref[...]), 0.0)
    o_ref[0] = x + mod_ref[5:6, :] * _dot((a * a).astype(BF16), w2_ref[...])


def _mlp(xs, mods, g, w1, w2, *, need_ctx):
    b, n_tok, d = xs.shape
    t = n_tok // TILE
    first = 0 if need_ctx else 1
    const = lambda shape: pl.BlockSpec(shape, lambda i, j: (0,) * len(shape))
    n_out = n_tok - first * TILE
    return pl.pallas_call(
        _mlp_kernel,
        grid=(b, t - first),
        in_specs=[pl.BlockSpec((1, TILE, d), lambda i, j: (i, j + first, 0)),
                  pl.BlockSpec((None, None, 6, d), lambda i, j: (i, jnp.minimum(j + first, 1), 0, 0)),
                  const((1, d)), const((d, D_FF)), const((D_FF, d))],
        out_specs=pl.BlockSpec((1, TILE, d), lambda i, j: (i, j, 0)),
        out_shape=jax.ShapeDtypeStruct((b, n_out, d), F32),
        compiler_params=_cparams(("arbitrary", "arbitrary")),
        name="mlp",
    )(xs, mods, g, w1, w2)


def _repack_w_in(w):
    d = w.shape[0]
    z = lambda n: jnp.zeros((d, n), w.dtype)
    o_mla = GDN_COLS
    o_diff = o_mla + MLA_COLS
    o_ssm = o_diff + DIFF_COLS
    o_gate = MIX_COLS
    gdn_ab = w[:, 2 * GDN_QK + 2 * GDN_VW:GDN_COLS]
    ssm_dt = w[:, o_ssm + 2 * SSM_INNER + 2 * SSM_BC:o_ssm + SSM_COLS]
    kpe = w[:, o_mla + MLA_Q_LORA + MLA_KV_LORA:o_mla + MLA_COLS]
    parts = [
        w[:, 0:2 * GDN_QK + 2 * GDN_VW],
        w[:, o_ssm + SSM_INNER:o_ssm + 2 * SSM_INNER + 2 * SSM_BC],
        w[:, o_diff:o_diff + DIFF_COLS],
        w[:, o_mla:o_mla + MLA_Q_LORA + MLA_KV_LORA], z(MLA_NOPE), kpe, z(LANES - MLA_QK),
        gdn_ab, ssm_dt, z(LANES - 4 * GDN_HEADS - 2 * SSM_HEADS),
        z(C_SSMZ - C_SMALL - LANES),
        w[:, o_ssm:o_ssm + SSM_INNER],
        w[:, o_gate:o_gate + GATE_COLS],
    ]
    out = jnp.concatenate(parts, axis=1)
    assert out.shape[1] == N_IN
    return out.astype(BF16)


def _lane_row(vals, offset):
    row = jnp.zeros((1, LANES), F32)
    return row.at[0, offset:offset + vals.shape[0]].set(vals.astype(F32))


def _rope_tables(n_lat, rot_dim, lane_offset, period):
    rows = n_lat // GRID_W
    row = jnp.repeat(jnp.arange(rows, dtype=F32), GRID_W)
    col = jnp.tile(jnp.arange(GRID_W, dtype=F32), rows)
    quarter = rot_dim // 4
    inv = ROPE_THETA ** (-jnp.arange(quarter, dtype=F32) / quarter)
    ar = row[:, None] * inv
    ac = col[:, None] * inv
    ang = jnp.concatenate([ar, ar, ac, ac], axis=-1)
    cos, sin = jnp.cos(ang), jnp.sin(ang)
    first = (np.arange(rot_dim) % (2 * quarter)) < quarter
    sin_a = jnp.where(first, -sin, 0.0)
    sin_b = jnp.where(first, 0.0, sin)
    reps = LANES // period
    def place(t, fill):
        blk = jnp.full((n_lat, period), fill, F32).at[:, lane_offset:lane_offset + rot_dim].set(t)
        blk = jnp.tile(blk, (1, reps))
        ctx = jnp.full((CTX_LEN, LANES), fill, F32)
        return jnp.concatenate([ctx, blk], axis=0)
    return jnp.stack([place(cos, 1.0), place(sin_a, 0.0), place(sin_b, 0.0)])


def kernel(x, c, ctx, c_ctx, ada_w, ada_b, norm1_g, norm2_g, w_in, gdn_conv, gdn_a_log, gdn_dt_bias, gdn_norm_g,
           mla_q_lora_g, mla_kv_lora_g, mla_w_uq, mla_w_ukv, mla_qn_g, mla_kn_g, diff_qn_g, diff_kn_g,
           diff_lambda, diff_sub_g, ssm_conv, ssm_conv_b, ssm_a_log, ssm_dt_bias, ssm_d, ssm_norm_g, w_branch,
           w_out, mlp_w1, mlp_w2):
    b, n_lat, d = x.shape
    depth = w_in.shape[0]
    assert ctx.shape[1] == CTX_LEN == TILE and n_lat % TILE == 0 and d == D_MODEL
    xs = jnp.concatenate([ctx, x], axis=1)
    tab_mla = _rope_tables(n_lat, MLA_ROPE, MLA_NOPE, LANES)
    tab_diff = _rope_tables(n_lat, DIFF_HD, 0, DIFF_HD)
    rows = b + 8
    cc = jnp.zeros((rows, d), F32).at[:b].set(c).at[b].set(c_ctx)
    row2 = lambda v: v.reshape(1, -1).astype(F32)

    for l in range(depth):
        need_ctx = l < depth - 1
        lam_init = LAMBDA_BASE - LAMBDA_AMP * math.exp(-LAMBDA_RATE * l)
        mod = _ada(cc, ada_w[l], ada_b[l].reshape(1, -1))
        mod_lat = mod[:b].reshape(b, 6, d)
        mod_ctx = jnp.broadcast_to(mod[b].reshape(1, 6, d), (b, 6, d))
        mods = jnp.stack([mod_ctx, mod_lat], axis=1)

        u = _inproj(xs, mods, row2(norm1_g[l]), _repack_w_in(w_in[l]))

        gdn_o = _gdn_scan(_gdn_prep(u, gdn_conv[l].astype(F32),
                                    _lane_row(gdn_a_log[l].reshape(-1), SM_A),
                                    _lane_row(gdn_dt_bias[l].reshape(-1), SM_A)))

        wq = mla_w_uq[l].reshape(MLA_Q_LORA, MLA_HEADS, MLA_QK)
        wq = jnp.pad(wq, ((0, 0), (0, 0), (0, LANES - MLA_QK))).reshape(MLA_Q_LORA, MLA_HEADS * LANES)
        wkv = mla_w_ukv[l].reshape(MLA_KV_LORA, MLA_HEADS, MLA_NOPE + MLA_V)
        wk = jnp.pad(wkv[:, :, :MLA_NOPE], ((0, 0), (0, 0), (0, LANES - MLA_NOPE))).reshape(MLA_KV_LORA, MLA_HEADS * LANES)
        wvt = wkv[:, :, MLA_NOPE:].reshape(MLA_KV_LORA, MLA_HEADS * MLA_V).T
        pad_g = lambda g: jnp.pad(g.astype(F32), (0, LANES - MLA_QK)).reshape(1, LANES)
        q_m, k_m, vt_m = _mla_prep(u, tab_mla, row2(mla_q_lora_g[l]), row2(mla_kv_lora_g[l]),
                                   wq.astype(BF16), wk.astype(BF16), wvt.astype(BF16),
                                   pad_g(mla_qn_g[l]), pad_g(mla_kn_g[l]))
        y_mla = _attention(q_m, k_m, vt_m, n_heads=MLA_HEADS, dv=MLA_V, need_ctx=need_ctx)

        rep_g = lambda g: jnp.tile(g.astype(F32), LANES // DIFF_HD).reshape(1, LANES)
        q_d, k_d, vt_d = _diff_prep(u, tab_diff, rep_g(diff_qn_g[l]), rep_g(diff_kn_g[l]))
        y_diff = _attention(q_d, k_d, vt_d, n_heads=DIFF_HEADS, dv=2 * DIFF_HD, need_ctx=need_ctx, diff=True,
                            lam_p=diff_lambda[l].astype(F32), sub_g=diff_sub_g[l].astype(F32).reshape(-1, 1),
                            lam_init=lam_init)

        ssd_y = _ssd(u, ssm_conv[l].astype(F32), row2(ssm_conv_b[l]),
                     _lane_row(ssm_a_log[l].reshape(-1), SM_DT), _lane_row(ssm_dt_bias[l].reshape(-1), SM_DT),
                     jnp.repeat(ssm_d[l].astype(F32), SSM_HEAD_DIM).reshape(1, -1))

        xs = _merge(xs, mods, gdn_o, u, y_mla, y_diff, ssd_y, row2(gdn_norm_g[l]), row2(ssm_norm_g[l]),
                    w_branch[l].astype(BF16), w_out[l].astype(BF16), need_ctx=need_ctx)
        xs = _mlp(xs, mods, row2(norm2_g[l]), mlp_w1[l].astype(BF16), mlp_w2[l].astype(BF16), need_ctx=need_ctx)
    return xs
```

```python
import functools
import math

import jax
import jax.numpy as jnp
import numpy as np
from jax import lax
from jax.experimental import pallas as pl
from jax.experimental.pallas import tpu as pltpu

F32 = jnp.float32
BF16 = jnp.bfloat16

D_MODEL = 1024
CTX_LEN = 256
GRID_W = 64
ROPE_THETA = 10000.0
NORM_EPS = 1e-6
CONV_K = 5
N_BRANCH = 4
D_FF = 4 * D_MODEL
GDN_HEADS = 4
GDN_DK = 128
GDN_DV = 128
MLA_HEADS = 8
MLA_NOPE = 64
MLA_ROPE = 32
MLA_V = 64
MLA_QK = MLA_NOPE + MLA_ROPE
MLA_Q_LORA = 384
MLA_KV_LORA = 256
DIFF_HEADS = 4
DIFF_HD = 64
LAMBDA_BASE = 0.8
LAMBDA_AMP = 0.6
LAMBDA_RATE = 0.3
SSM_HEADS = 8
SSM_HEAD_DIM = 64
SSM_GROUPS = 2
SSM_STATE = 128
GDN_QK = GDN_HEADS * GDN_DK
GDN_VW = GDN_HEADS * GDN_DV
DIFF_QK = DIFF_HEADS * 2 * DIFF_HD
DIFF_VW = DIFF_HEADS * 2 * DIFF_HD
SSM_INNER = SSM_HEADS * SSM_HEAD_DIM
SSM_BC = SSM_GROUPS * SSM_STATE
BRANCH_W = 512
GDN_COLS = 2 * GDN_QK + 2 * GDN_VW + 4 * GDN_HEADS
MLA_COLS = MLA_Q_LORA + MLA_KV_LORA + MLA_ROPE
DIFF_COLS = 2 * DIFF_QK + DIFF_VW
SSM_COLS = 2 * SSM_INNER + 2 * SSM_BC + 2 * SSM_HEADS
MIX_COLS = GDN_COLS + MLA_COLS + DIFF_COLS + SSM_COLS
GATE_COLS = N_BRANCH * D_MODEL

LANES = 128
TILE = 256
CHUNK = 64
CPT = TILE // CHUNK
HALO = 8
VMEM_LIMIT = 56 * 1024 * 1024
V_ONES = 16
KEY_BLOCK = 256
LOG2E = 1.4426950408889634

C_GDN = 0
C_XBC = 2048
C_DIFF = 3072
C_MLA = 4608
C_SMALL = 5376
C_SSMZ = 5632
C_GATE = 6144
N_IN = 10240
IN_TN = 2048

SM_A = 0
SM_B = 8
SM_DT = 16


def _cparams(sem):
    return pltpu.CompilerParams(dimension_semantics=sem, vmem_limit_bytes=VMEM_LIMIT)


def _dot(a, b):
    return jnp.dot(a, b, preferred_element_type=F32)


def _dot_nt(a, b):
    return lax.dot_general(a, b, (((1,), (1,)), ((), ())), preferred_element_type=F32)


def _dot_tn(a, b):
    return lax.dot_general(a, b, (((0,), (0,)), ((), ())), preferred_element_type=F32)


def _split3(x):
    x1 = x.astype(BF16)
    r1 = x - x1.astype(F32)
    x2 = r1.astype(BF16)
    r2 = r1 - x2.astype(F32)
    return x1, x2, r2.astype(BF16)


def _dot_exact_lhs(a_bf, x):
    x1, x2, x3 = _split3(x)
    return _dot(a_bf, x1) + _dot(a_bf, x2) + _dot(a_bf, x3)


def _dot_exact_rhs(x, b_bf):
    x1, x2, x3 = _split3(x)
    return _dot(x1, b_bf) + _dot(x2, b_bf) + _dot(x3, b_bf)


def _sigmoid(x):
    return 1.0 / (1.0 + jnp.exp(-x))


def _silu(x):
    return x * _sigmoid(x)


def _softplus(x):
    return jnp.maximum(x, 0.0) + jnp.log1p(jnp.exp(-jnp.abs(x)))


def _iota(shape, dim):
    return lax.broadcasted_iota(jnp.int32, shape, dim)


def _chunk_tri(n, upper):
    r = _iota((n, n), 0)
    c = _iota((n, n), 1)
    same = (r // CHUNK) == (c // CHUNK)
    tri = (r <= c) if upper else (r >= c)
    return jnp.where(same & tri, 1.0, 0.0).astype(BF16)


def _expander(n_rows, first_row, n_groups, width):
    r = _iota((n_rows, n_groups * width), 0)
    c = _iota((n_rows, n_groups * width), 1)
    return jnp.where(r == first_row + c // width, 1.0, 0.0).astype(BF16)


def _ada_kernel(c_ref, w_ref, b_ref, o_ref):
    a = _silu(c_ref[...])
    a1, a2, a3 = _split3(a)
    w1, w2, w3 = _split3(w_ref[...])
    acc = _dot(a1, w1) + (_dot(a1, w2) + _dot(a2, w1)) + (_dot(a1, w3) + _dot(a2, w2) + _dot(a3, w1))
    o_ref[...] = acc + b_ref[...]


def _ada(cc, w, b):
    rows, d = cc.shape
    n = w.shape[1]
    tn = 1536
    return pl.pallas_call(
        _ada_kernel,
        grid=(n // tn,),
        in_specs=[pl.BlockSpec((rows, d), lambda j: (0, 0)),
                  pl.BlockSpec((d, tn), lambda j: (0, j)),
                  pl.BlockSpec((1, tn), lambda j: (0, j))],
        out_specs=pl.BlockSpec((rows, tn), lambda j: (0, j)),
        out_shape=jax.ShapeDtypeStruct((rows, n), F32),
        compiler_params=_cparams(("arbitrary",)),
        name="ada",
    )(cc, w, b)


def _rms(x, g):
    ms = jnp.mean(x * x, axis=-1, keepdims=True)
    return x * lax.rsqrt(ms + NORM_EPS) * g


def _stream_specs(xs, first=0):
    if isinstance(xs, tuple):
        ctx, x = xs
        d = x.shape[-1]
        specs = [pl.BlockSpec((1, TILE, d), lambda i, j: (i, 0, 0)),
                 pl.BlockSpec((1, TILE, d), lambda i, j: (i, jnp.maximum(j + first - 1, 0), 0))]
        return specs, [ctx, x], ctx.shape[1] + x.shape[1]
    d = xs.shape[-1]
    return [pl.BlockSpec((1, TILE, d), lambda i, j: (i, j + first, 0))], [xs], xs.shape[1]


def _stream_tile(refs, first=0):
    if len(refs) == 2:
        return jnp.where(pl.program_id(1) + first == 0, refs[0][0], refs[1][0])
    return refs[0][0]


def _inproj_kernel(*refs, n_src):
    mod_ref, g_ref, w_ref, o_ref, gate_ref = refs[n_src:]
    h = _rms(_stream_tile(refs[:n_src]), g_ref[...])
    hb = (h * (1.0 + mod_ref[1:2, :]) + mod_ref[0:1, :]).astype(BF16)
    for c in range(0, C_GATE, IN_TN):
        o_ref[0, :, c:c + IN_TN] = _dot(hb, w_ref[:, c:c + IN_TN])
    for c in range(0, GATE_COLS, IN_TN):
        gate_ref[0, :, c:c + IN_TN] = _dot(hb, w_ref[:, C_GATE + c:C_GATE + c + IN_TN]).astype(BF16)


def _inproj(xs, mods, g, w):
    src_specs, src_args, n_tok = _stream_specs(xs)
    b, d = mods.shape[0], mods.shape[-1]
    t = n_tok // TILE
    return pl.pallas_call(
        functools.partial(_inproj_kernel, n_src=len(src_args)),
        grid=(b, t),
        in_specs=src_specs + [
                  pl.BlockSpec((None, None, 6, d), lambda i, j: (i, jnp.minimum(j, 1), 0, 0)),
                  pl.BlockSpec((1, d), lambda i, j: (0, 0)),
                  pl.BlockSpec((d, N_IN), lambda i, j: (0, 0), pipeline_mode=pl.Buffered(1))],
        out_specs=[pl.BlockSpec((1, TILE, C_GATE), lambda i, j: (i, j, 0)),
                   pl.BlockSpec((1, TILE, GATE_COLS), lambda i, j: (i, j, 0))],
        out_shape=[jax.ShapeDtypeStruct((b, n_tok, C_GATE), F32),
                   jax.ShapeDtypeStruct((b, n_tok, GATE_COLS), BF16)],
        compiler_params=_cparams(("arbitrary", "arbitrary")),
        name="inproj",
    )(*src_args, mods, g, w)


def _halo_specs(width, col_block, n_tiles, tile_of):
    rpt = TILE // HALO
    last = n_tiles * rpt - 1
    main = pl.BlockSpec((1, TILE, width), lambda *ids: (ids[0], tile_of(*ids), col_block))
    prev = pl.BlockSpec((1, HALO, width),
                        lambda *ids: (ids[0], jnp.maximum(tile_of(*ids) * rpt - 1, 0), col_block))
    nxt = pl.BlockSpec((1, HALO, width),
                       lambda *ids: (ids[0], jnp.minimum(tile_of(*ids) * rpt + rpt, last), col_block))
    return main, prev, nxt


def _conv_tile(main_ref, prev_ref, next_ref, w_ref, ext_ref, tile, n_tiles):
    prev_ok = (tile >= 2).astype(F32)
    next_ok = jnp.logical_and(tile >= 1, tile < n_tiles - 1).astype(F32)
    ext_ref[0:HALO, :] = prev_ref[0] * prev_ok
    ext_ref[HALO:HALO + TILE, :] = main_ref[0]
    ext_ref[HALO + TILE:2 * HALO + TILE, :] = next_ref[0] * next_ok
    half = CONV_K // 2
    acc = None
    for k in range(CONV_K):
        term = w_ref[k:k + 1, :] * ext_ref[HALO - half + k:HALO - half + k + TILE, :]
        acc = term if acc is None else acc + term
    return acc


def _block_inverse_many(mats):
    n = mats[0].shape[0]
    ri = _iota((n, n), 0)
    ci = _iota((n, n), 1)
    eye = jnp.where(ri == ci, 1.0, 0.0).astype(F32)
    pair = (ri // 2) == (ci // 2)
    ts = [eye - jnp.where(pair, a, 0.0) for a in mats]
    abs_ = [a.astype(BF16) for a in mats]
    zero = jnp.zeros((n, n), BF16)
    size = 2
    while size < CHUNK:
        off = ((ri // (2 * size)) == (ci // (2 * size))) & ((ri // size) != (ci // size))
        tbs = [t.astype(BF16) for t in ts]
        xs = [_dot(jnp.where(off, ab, zero), tb).astype(BF16) for ab, tb in zip(abs_, tbs)]
        ts = [t - _dot(tb, x) for t, tb, x in zip(ts, tbs, xs)]
        size *= 2
    return ts


def _gdn_prep_kernel(main_ref, prev_ref, next_ref, sm_ref, cw_ref, alog_ref, dtb_ref,
                     u0_ref, wq0_ref, kd0_ref, at0_ref, gl0_ref,
                     u1_ref, wq1_ref, kd1_ref, at1_ref, gl1_ref,
                     ext_ref, *, n_tiles):
    tile = pl.program_id(1)
    qkv = _silu(_conv_tile(main_ref, prev_ref, next_ref, cw_ref, ext_ref, tile, n_tiles))
    sm = sm_ref[0]
    g_all = -jnp.exp(alog_ref[...]) * _softplus(sm + dtb_ref[...])
    beta_all = _sigmoid(sm)
    nh = GDN_HEADS
    exp_g = _expander(LANES, SM_A, 2 * nh, LANES)
    exp_b = _expander(LANES, SM_B, 2 * nh, LANES)
    beta_x = _dot_exact_rhs(beta_all, exp_b)
    outs = ((u0_ref, wq0_ref, kd0_ref, at0_ref, gl0_ref), (u1_ref, wq1_ref, kd1_ref, at1_ref, gl1_ref))
    ri = _iota((TILE, TILE), 0)
    ci = _iota((TILE, TILE), 1)
    same = (ri // CHUNK) == (ci // CHUNK)
    cums, masks = [], []
    for d in range(2):
        cum = _dot_exact_lhs(_chunk_tri(TILE, upper=(d == 1)), g_all)
        cums.append((_dot_exact_rhs(cum, exp_g), cum.T))
        masks.append((same & ((ri >= ci) if d == 0 else (ri <= ci)), same & ((ri > ci) if d == 0 else (ri < ci))))

    chains, a_mats = [], []
    for h in range(nh):
        q = qkv[:, h * GDN_DK:(h + 1) * GDN_DK]
        k = qkv[:, GDN_QK + h * GDN_DK:GDN_QK + (h + 1) * GDN_DK]
        v = qkv[:, 2 * GDN_QK + h * GDN_DV:2 * GDN_QK + (h + 1) * GDN_DV]
        q = q * lax.rsqrt(jnp.sum(q * q, axis=-1, keepdims=True) + 1e-6) * (GDN_DK ** -0.5)
        k = k * lax.rsqrt(jnp.sum(k * k, axis=-1, keepdims=True) + 1e-6)
        kb = k.astype(BF16)
        kq = _dot_nt(jnp.concatenate([kb, q.astype(BF16)], axis=0), kb)
        for d in range(2):
            col = d * nh + h
            lanes = slice(col * LANES, (col + 1) * LANES)
            cum_x, cum_t = cums[d]
            incl, strict = masks[d]
            g_col = cum_x[:, lanes]
            b_col = beta_x[:, lanes]
            diff = jnp.concatenate([g_col, g_col], axis=1) - cum_t[SM_A + col:SM_A + col + 1, :]
            decay = jnp.where(incl, jnp.exp(jnp.where(incl, diff, 0.0)), 0.0)
            a_mats.append(jnp.where(strict, kq[:TILE] * jnp.concatenate([b_col, b_col], axis=1) * decay, 0.0))
            e_g = jnp.exp(g_col)
            rhs = jnp.concatenate([v * b_col, k * b_col * e_g], axis=1).astype(BF16)
            chains.append((h, d, q, k, g_col, e_g, rhs, kq[TILE:] * decay))

    t_mats = _block_inverse_many(a_mats)
    for (h, d, q, k, g_col, e_g, rhs, attn), t_mat in zip(chains, t_mats):
        uw = _dot(t_mat.astype(BF16), rhs)
        qg = (q * e_g).astype(BF16)
        u_ref, wq_ref, kd_ref, at_ref, gl_ref = outs[d]
        hl = slice(h * LANES, (h + 1) * LANES)
        for c in range(CPT):
            rows = slice(c * CHUNK, (c + 1) * CHUNK)
            last = c * CHUNK + (CHUNK - 1 if d == 0 else 0)
            g_last = g_col[last:last + 1, :]
            slot = c if d == 0 else CPT - 1 - c
            u_ref[0, 0, slot, :, hl] = uw[rows, :GDN_DV]
            wq_ref[0, 0, slot, 0:CHUNK, hl] = uw[rows, GDN_DV:].astype(BF16)
            wq_ref[0, 0, slot, CHUNK:2 * CHUNK, hl] = qg[rows]
            kd_ref[0, 0, slot, :, hl] = (k[rows] * jnp.exp(g_last - g_col[rows])).astype(BF16)
            at_ref[0, 0, slot, :, h * CHUNK:(h + 1) * CHUNK] = attn[rows, rows].astype(BF16)
            gl_ref[0, 0, slot, :, hl] = jnp.exp(g_last)


def _bwd_block(j, n_tiles):
    return jnp.where(j == 0, 0, n_tiles - j)


def _gdn_prep(u, conv_w, alog_row, dtb_row):
    b, n_tok, _ = u.shape
    t = n_tok // TILE
    w = GDN_VW
    cw = 2 * GDN_QK + GDN_VW
    main, prev, nxt = _halo_specs(cw, 0, t, lambda i, j: j)
    fwd = lambda i, j: (i, j, 0, 0, 0)
    bwd = lambda i, j: (i, _bwd_block(j, t), 0, 0, 0)

    def outs(imap):
        return [pl.BlockSpec((1, 1, CPT, CHUNK, w), imap),
                pl.BlockSpec((1, 1, CPT, 2 * CHUNK, w), imap),
                pl.BlockSpec((1, 1, CPT, CHUNK, w), imap),
                pl.BlockSpec((1, 1, CPT, CHUNK, GDN_HEADS * CHUNK), imap),
                pl.BlockSpec((1, 1, CPT, 1, w), imap)]

    shapes = [jax.ShapeDtypeStruct((b, t, CPT, CHUNK, w), F32),
              jax.ShapeDtypeStruct((b, t, CPT, 2 * CHUNK, w), BF16),
              jax.ShapeDtypeStruct((b, t, CPT, CHUNK, w), BF16),
              jax.ShapeDtypeStruct((b, t, CPT, CHUNK, GDN_HEADS * CHUNK), BF16),
              jax.ShapeDtypeStruct((b, t, CPT, 1, w), F32)]
    return pl.pallas_call(
        functools.partial(_gdn_prep_kernel, n_tiles=t),
        grid=(b, t),
        in_specs=[main, prev, nxt,
                  pl.BlockSpec((1, TILE, LANES), lambda i, j: (i, j, C_SMALL // LANES)),
                  pl.BlockSpec((CONV_K, cw), lambda i, j: (0, 0)),
                  pl.BlockSpec((1, LANES), lambda i, j: (0, 0)),
                  pl.BlockSpec((1, LANES), lambda i, j: (0, 0))],
        out_specs=outs(fwd) + outs(bwd),
        out_shape=shapes + shapes,
        scratch_shapes=[pltpu.VMEM((TILE + 2 * HALO, cw), F32)],
        compiler_params=_cparams(("arbitrary", "arbitrary")),
        name="gdn_prep",
    )(u, u, u, u, conv_w, alog_row, dtb_row)


def _gdn_scan_kernel(u0_ref, wq0_ref, kd0_ref, at0_ref, gl0_ref,
                     u1_ref, wq1_ref, kd1_ref, at1_ref, gl1_ref,
                     of_ref, ob_ref, s_ref):
    @pl.when(pl.program_id(1) == 0)
    def _():
        s_ref[...] = jnp.zeros_like(s_ref)

    ins = ((u0_ref, wq0_ref, kd0_ref, at0_ref, gl0_ref), (u1_ref, wq1_ref, kd1_ref, at1_ref, gl1_ref))
    chains = [(d, h) for d in range(2) for h in range(GDN_HEADS)]
    hls = [slice(h * LANES, (h + 1) * LANES) for _, h in chains]
    states = [s_ref[d, h] for d, h in chains]
    for slot in range(CPT):
        rs = [_dot(ins[d][1][0, 0, slot, :, hl], s.astype(BF16)) for (d, _), hl, s in zip(chains, hls, states)]
        vbs = [(ins[d][0][0, 0, slot, :, hl] - r[0:CHUNK]).astype(BF16) for (d, _), hl, r in zip(chains, hls, rs)]
        outs = [r[CHUNK:2 * CHUNK] + _dot(ins[d][3][0, 0, slot, :, h * CHUNK:(h + 1) * CHUNK], vb)
                for (d, h), r, vb in zip(chains, rs, vbs)]
        states = [s * ins[d][4][0, 0, slot, :, hl] + _dot_tn(ins[d][2][0, 0, slot, :, hl], vb)
                  for (d, _), hl, s, vb in zip(chains, hls, states, vbs)]
        for (d, _), hl, o in zip(chains, hls, outs):
            c = slot if d == 0 else CPT - 1 - slot
            (of_ref if d == 0 else ob_ref)[0, c * CHUNK:(c + 1) * CHUNK, hl] = o
    for (d, h), s in zip(chains, states):
        s_ref[d, h] = s


def _gdn_scan(prep):
    b, t = prep[0].shape[:2]
    w = GDN_VW
    imap = lambda i, j: (i, j, 0, 0, 0)
    specs = [pl.BlockSpec((1, 1, CPT, CHUNK, w), imap),
             pl.BlockSpec((1, 1, CPT, 2 * CHUNK, w), imap),
             pl.BlockSpec((1, 1, CPT, CHUNK, w), imap),
             pl.BlockSpec((1, 1, CPT, CHUNK, GDN_HEADS * CHUNK), imap),
             pl.BlockSpec((1, 1, CPT, 1, w), imap)]
    return pl.pallas_call(
        _gdn_scan_kernel,
        grid=(b, t),
        in_specs=specs + specs,
        out_specs=[pl.BlockSpec((1, TILE, w), lambda i, j: (i, j, 0)),
                   pl.BlockSpec((1, TILE, w), lambda i, j: (i, _bwd_block(j, t), 0))],
        out_shape=[jax.ShapeDtypeStruct((b, t * TILE, w), F32)] * 2,
        scratch_shapes=[pltpu.VMEM((2, GDN_HEADS, GDN_DK, GDN_DV), F32)],
        compiler_params=_cparams(("arbitrary", "arbitrary")),
        name="gdn_scan",
    )(*prep)


def _ssd_direction(d, xbc, sm, y_ref, st_ref, alog_ref, dtb_ref, dskip_ref):
    nh, hd, ng = SSM_HEADS, SSM_HEAD_DIM, SSM_GROUPS
    hpg = nh // ng
    gw = hpg * hd
    xs = xbc[:, :SSM_INNER]
    dt_all = _softplus(sm + dtb_ref[...])
    da_all = dt_all * (-jnp.exp(alog_ref[...]))
    cum = _dot_exact_lhs(_chunk_tri(TILE, upper=(d == 1)), da_all)
    cum_t = cum.T
    expand = _expander(LANES, SM_DT + d * nh, nh, hd)
    dt_x = _dot_exact_rhs(dt_all, expand)
    cum_x = _dot_exact_rhs(cum, expand)
    xdt = xs * dt_x
    ri = _iota((CHUNK, gw), 0)
    ci = _iota((CHUNK, gw), 1) % CHUNK
    incl = (ri >= ci) if d == 0 else (ri <= ci)
    head_diag = (_iota((gw, gw), 0) // hd) == (_iota((gw, gw), 1) // hd)
    zero_bf = jnp.zeros((gw, gw), BF16)
    for step in range(CPT):
        c = step if d == 0 else CPT - 1 - step
        rows = slice(c * CHUNK, (c + 1) * CHUNK)
        last = c * CHUNK + (CHUNK - 1 if d == 0 else 0)
        cum_c = cum_x[rows]
        cum_last = cum_x[last:last + 1]
        decay_states = jnp.exp(cum_last - cum_c)
        in_decay = jnp.exp(cum_c)
        chunk_decay = jnp.exp(cum_last)
        for g in range(ng):
            gl = slice(g * gw, (g + 1) * gw)
            bm = xbc[rows, SSM_INNER + g * SSM_STATE:SSM_INNER + (g + 1) * SSM_STATE].astype(BF16)
            cm = xbc[rows, SSM_INNER + SSM_BC + g * SSM_STATE:SSM_INNER + SSM_BC + (g + 1) * SSM_STATE].astype(BF16)
            state = st_ref[d, g]
            y_off = _dot(cm, state.astype(BF16)) * in_decay[:, gl]
            xdt_c = xdt[rows, gl]
            st_ref[d, g] = state * chunk_decay[:, gl] + _dot_tn(bm, (xdt_c * decay_states[:, gl]).astype(BF16))
            scores = _dot_nt(cm, jnp.concatenate([bm] * hpg, axis=0))
            col0 = SM_DT + d * nh + g * hpg
            row_terms = jnp.concatenate([cum_t[col0 + hh:col0 + hh + 1, rows] for hh in range(hpg)], axis=1)
            seg = cum_c[:, gl] - row_terms
            lmat = jnp.where(incl, jnp.exp(jnp.where(incl, seg, 0.0)), 0.0)
            x_bd = jnp.where(head_diag, jnp.concatenate([xdt_c.astype(BF16)] * hpg, axis=0), zero_bf)
            y = _dot((scores * lmat).astype(BF16), x_bd) + y_off
            if d == 0:
                y = y + dskip_ref[:, gl] * xs[rows, gl]
            y_ref[0, rows, gl] = y


def _ssd_kernel(m_ref, p_ref, n_ref, sm_ref, cw_ref, cb_ref, alog_ref, dtb_ref, dskip_ref,
                yf_ref, yb_ref, ext_ref, st_ref, cache_ref, *, n_tiles):
    s = pl.program_id(1)

    @pl.when(s == 0)
    def _():
        st_ref[...] = jnp.zeros_like(st_ref)

    @pl.when(s < n_tiles)
    def _():
        xbc = _silu(_conv_tile(m_ref, p_ref, n_ref, cw_ref, ext_ref, s, n_tiles) + cb_ref[...])
        cache_ref[s] = xbc
        _ssd_direction(0, xbc, sm_ref[0], yf_ref, st_ref, alog_ref, dtb_ref, dskip_ref)

    @pl.when(s >= n_tiles)
    def _():
        tile = _bwd_block(s - n_tiles, n_tiles)
        _ssd_direction(1, cache_ref[tile], sm_ref[0], yb_ref, st_ref, alog_ref, dtb_ref, dskip_ref)


def _ssd(u, conv_w, conv_b, alog_row, dtb_row, dskip_row):
    b, n_tok, _ = u.shape
    t = n_tok // TILE
    cw = SSM_INNER + 2 * SSM_BC
    main, prev, nxt = _halo_specs(cw, C_XBC // cw, t, lambda i, s: jnp.minimum(s, t - 1))
    tile_of = lambda s: jnp.where(s < t, s, _bwd_block(s - t, t))
    const = lambda r, c: pl.BlockSpec((r, c), lambda i, s: (0, 0))
    return pl.pallas_call(
        functools.partial(_ssd_kernel, n_tiles=t),
        grid=(b, 2 * t),
        in_specs=[main, prev, nxt,
                  pl.BlockSpec((1, TILE, LANES), lambda i, s: (i, tile_of(s), C_SMALL // LANES)),
                  const(CONV_K, cw), const(1, cw), const(1, LANES), const(1, LANES), const(1, SSM_INNER)],
        out_specs=[pl.BlockSpec((1, TILE, SSM_INNER), lambda i, s: (i, jnp.minimum(s, t - 1), 0)),
                   pl.BlockSpec((1, TILE, SSM_INNER), lambda i, s: (i, jnp.where(s < t, 0, _bwd_block(s - t, t)), 0))],
        out_shape=[jax.ShapeDtypeStruct((b, n_tok, SSM_INNER), F32)] * 2,
        scratch_shapes=[pltpu.VMEM((TILE + 2 * HALO, cw), F32),
                        pltpu.VMEM((2, SSM_GROUPS, SSM_STATE, SSM_INNER // SSM_GROUPS), F32),
                        pltpu.VMEM((t, TILE, cw), F32)],
        compiler_params=_cparams(("arbitrary", "arbitrary")),
        name="ssd",
    )(u, u, u, u, conv_w, conv_b, alog_row, dtb_row, dskip_row)


PAIR = 2 * LANES


def _group_ones(group):
    r = _iota((PAIR, PAIR), 0)
    c = _iota((PAIR, PAIR), 1)
    return jnp.where((r // group) == (c // group), 1.0, 0.0).astype(BF16)


def _rope_perm(period, lane_offset, rot_dim):
    r = _iota((PAIR, PAIR), 0)
    c = _iota((PAIR, PAIR), 1)
    q = rot_dim // 4
    cl = c % period - lane_offset
    src = jnp.where((cl % (2 * q)) < q, c + q, c - q)
    return jnp.where((cl >= 0) & (cl < rot_dim) & (r == src), 1.0, 0.0).astype(BF16)


def _norm_rope(x, gain, ones_bd, perm, tab_ref, n_real):
    ss = _dot((x * x).astype(BF16), ones_bd)
    xn = x * lax.rsqrt(ss / n_real + NORM_EPS) * gain
    hi = xn.astype(BF16)
    lo = (xn - hi.astype(F32)).astype(BF16)
    rot = _dot(hi, perm) + _dot(lo, perm)
    cos = jnp.concatenate([tab_ref[0]] * 2, axis=1)
    sin = jnp.concatenate([tab_ref[1] + tab_ref[2]] * 2, axis=1)
    return xn * cos + rot * sin


def _store_vt(vt_ref, vt, n_heads, dv):
    ones = jnp.ones((V_ONES, vt.shape[1]), BF16)
    for h in range(n_heads):
        base = h * (dv + V_ONES)
        vt_ref[0, base:base + dv, :] = vt[h * dv:(h + 1) * dv].astype(BF16)
        vt_ref[0, base + dv:base + dv + V_ONES, :] = ones


def _mla_prep_kernel(u_ref, tab_ref, qlg_ref, kvlg_ref, wq_ref, wk_ref, wvt_ref, qng_ref, kng_ref,
                     q_ref, k_ref, vt_ref):
    u = u_ref[0]
    cq = _rms(u[:, :MLA_Q_LORA], qlg_ref[...]).astype(BF16)
    ckv = _rms(u[:, MLA_Q_LORA:MLA_Q_LORA + MLA_KV_LORA], kvlg_ref[...]).astype(BF16)
    kpe = u[:, MLA_Q_LORA + MLA_KV_LORA:]
    q_all = _dot(cq, wq_ref[...])
    k_all = _dot(ckv, wk_ref[...])
    _store_vt(vt_ref, _dot_nt(wvt_ref[...], ckv), MLA_HEADS, MLA_V)
    scale = MLA_QK ** -0.5 * LOG2E
    ones_bd = _group_ones(LANES)
    perm = _rope_perm(LANES, MLA_NOPE, MLA_ROPE)
    qg = jnp.concatenate([qng_ref[...]] * 2, axis=1)
    kg = jnp.concatenate([kng_ref[...]] * 2, axis=1)
    kpe2 = jnp.concatenate([kpe, kpe], axis=1)
    for h in range(0, MLA_HEADS, 2):
        hl = slice(h * LANES, (h + 2) * LANES)
        q_ref[0, :, hl] = (_norm_rope(q_all[:, hl], qg, ones_bd, perm, tab_ref, MLA_QK) * scale).astype(BF16)
        k_ref[0, :, hl] = _norm_rope(k_all[:, hl] + kpe2, kg, ones_bd, perm, tab_ref, MLA_QK).astype(BF16)


def _mla_prep(u, tab, qlg, kvlg, wq, wk, wvt, qng, kng):
    b, n_tok, _ = u.shape
    t = n_tok // TILE
    w = MLA_HEADS * LANES
    cw = MLA_Q_LORA + MLA_KV_LORA + LANES
    const = lambda r, c: pl.BlockSpec((r, c), lambda i, j: (0, 0))
    return pl.pallas_call(
        _mla_prep_kernel,
        grid=(b, t),
        in_specs=[pl.BlockSpec((1, TILE, cw), lambda i, j: (i, j, C_MLA // cw)),
                  pl.BlockSpec((3, TILE, LANES), lambda i, j: (0, j, 0)),
                  const(1, MLA_Q_LORA), const(1, MLA_KV_LORA), const(MLA_Q_LORA, w), const(MLA_KV_LORA, w),
                  const(MLA_HEADS * MLA_V, MLA_KV_LORA), const(1, LANES), const(1, LANES)],
        out_specs=[pl.BlockSpec((1, TILE, w), lambda i, j: (i, j, 0)),
                   pl.BlockSpec((1, TILE, w), lambda i, j: (i, j, 0)),
                   pl.BlockSpec((1, MLA_HEADS * (MLA_V + V_ONES), TILE), lambda i, j: (i, 0, j))],
        out_shape=[jax.ShapeDtypeStruct((b, n_tok, w), BF16),
                   jax.ShapeDtypeStruct((b, n_tok, w), BF16),
                   jax.ShapeDtypeStruct((b, MLA_HEADS * (MLA_V + V_ONES), n_tok), BF16)],
        compiler_params=_cparams(("arbitrary", "arbitrary")),
        name="mla_prep",
    )(u, tab, qlg, kvlg, wq, wk, wvt, qng, kng)


def _diff_prep_kernel(u_ref, tab_ref, qng_ref, kng_ref, q_ref, k_ref, vt_ref):
    u = u_ref[0]
    ones_bd = _group_ones(DIFF_HD)
    perm = _rope_perm(DIFF_HD, 0, DIFF_HD)
    qg = jnp.concatenate([qng_ref[...]] * 2, axis=1)
    kg = jnp.concatenate([kng_ref[...]] * 2, axis=1)
    scale = DIFF_HD ** -0.5 * LOG2E
    for c in range(0, DIFF_QK, PAIR):
        q_ref[0, :, c:c + PAIR] = (_norm_rope(u[:, c:c + PAIR], qg, ones_bd, perm, tab_ref, DIFF_HD) * scale).astype(BF16)
        k_ref[0, :, c:c + PAIR] = _norm_rope(u[:, DIFF_QK + c:DIFF_QK + c + PAIR], kg, ones_bd, perm, tab_ref,
                                             DIFF_HD).astype(BF16)
    _store_vt(vt_ref, u[:, 2 * DIFF_QK:].T, DIFF_HEADS, 2 * DIFF_HD)


def _diff_prep(u, tab, qng, kng):
    b, n_tok, _ = u.shape
    t = n_tok // TILE
    const = lambda r, c: pl.BlockSpec((r, c), lambda i, j: (0, 0))
    return pl.pallas_call(
        _diff_prep_kernel,
        grid=(b, t),
        in_specs=[pl.BlockSpec((1, TILE, DIFF_COLS), lambda i, j: (i, j, C_DIFF // DIFF_COLS)),
                  pl.BlockSpec((3, TILE, LANES), lambda i, j: (0, j, 0)),
                  const(1, LANES), const(1, LANES)],
        out_specs=[pl.BlockSpec((1, TILE, DIFF_QK), lambda i, j: (i, j, 0)),
                   pl.BlockSpec((1, TILE, DIFF_QK), lambda i, j: (i, j, 0)),
                   pl.BlockSpec((1, DIFF_HEADS * (2 * DIFF_HD + V_ONES), TILE), lambda i, j: (i, 0, j))],
        out_shape=[jax.ShapeDtypeStruct((b, n_tok, DIFF_QK), BF16),
                   jax.ShapeDtypeStruct((b, n_tok, DIFF_QK), BF16),
                   jax.ShapeDtypeStruct((b, DIFF_HEADS * (2 * DIFF_HD + V_ONES), n_tok), BF16)],
        compiler_params=_cparams(("arbitrary", "arbitrary")),
        name="diff_prep",
    )(u, tab, qng, kng)


def _attn_kernel(q_ref, k_ref, vt_ref, *rest, n_heads, dv, diff, lam_init, n_q_tiles):
    rest = list(rest)
    if n_q_tiles is None:
        rest.pop(0)
    if diff:
        lamp_ref, subg_ref = rest[:2]
        rest = rest[2:]
        lp = lamp_ref[...]
        lam = (jnp.exp(jnp.sum(lp[0:1] * lp[1:2], axis=-1, keepdims=True))
               - jnp.exp(jnp.sum(lp[2:3] * lp[3:4], axis=-1, keepdims=True)) + lam_init)
        lane = _iota((TILE, LANES), 1)
    o_ref, ot_ref, s_ref = rest[:3]
    dva = dv + V_ONES
    n_iter = n_heads if diff else n_heads // 2
    n_keys = k_ref.shape[1]
    blocks = [(0, TILE)] + [(b0, KEY_BLOCK) for b0 in range(TILE, n_keys, KEY_BLOCK)]

    def streams(i):
        if diff:
            hl = slice(i * LANES, (i + 1) * LANES)
            q = q_ref[0, :, hl]
            zero = jnp.zeros_like(q)
            vr = slice(i * dva, (i + 1) * dva)
            return [(hl, vr, jnp.where(lane < DIFF_HD, q, zero)), (hl, vr, jnp.where(lane < DIFF_HD, zero, q))]
        out = []
        for h in (2 * i, 2 * i + 1):
            hl = slice(h * LANES, (h + 1) * LANES)
            out.append((hl, slice(h * dva, (h + 1) * dva), q_ref[0, :, hl]))
        return out

    def v_rows(i):
        if diff:
            return [slice(i * dva, (i + 1) * dva)] * 2
        return [slice(h * dva, (h + 1) * dva) for h in (2 * i, 2 * i + 1)]

    def finish(i, accs):
        a0, a1 = accs
        if diff:
            o = a0[:dv] / a0[dv:dv + 1] - lam * (a1[:dv] / a1[dv:dv + 1])
            ms = jnp.mean(o * o, axis=0, keepdims=True)
            o = o * lax.rsqrt(ms + NORM_EPS) * subg_ref[...] * (1.0 - lam_init)
            ot_ref[i * dv:(i + 1) * dv, :] = o
        else:
            o = jnp.concatenate([a0[:dv] / a0[dv:dv + 1], a1[:dv] / a1[dv:dv + 1]], axis=0)
            ot_ref[2 * i * dv:2 * (i + 1) * dv, :] = o

    def phase(cur, cur_par, prev_rows, prev_par, prev_max):
        cur_max = [None, None]
        accs = [None, None]
        for start, size in blocks:
            rows = slice(start, start + size)
            if cur is not None:
                for st, (hl, _, q) in enumerate(cur):
                    s = _dot_nt(k_ref[0, rows, hl], q)
                    s_ref[cur_par, st, rows, :] = s
                    part = jnp.max(s, axis=0, keepdims=True)
                    cur_max[st] = part if cur_max[st] is None else jnp.maximum(cur_max[st], part)
            if prev_rows is not None:
                for st, vr in enumerate(prev_rows):
                    p = jnp.exp2(s_ref[prev_par, st, rows, :] - prev_max[st]).astype(BF16)
                    pv = _dot(vt_ref[0, vr, rows], p)
                    accs[st] = pv if accs[st] is None else accs[st] + pv
        return cur_max, accs

    def middle(first_max):
        prev_max = first_max
        for i in range(1, n_iter):
            cur_max, accs = phase(streams(i), i % 2, v_rows(i - 1), (i - 1) % 2, prev_max)
            finish(i - 1, accs)
            prev_max = cur_max
        return prev_max

    last_par = (n_iter - 1) % 2
    if n_q_tiles is None:
        last_max = middle(phase(streams(0), 0, None, 0, None)[0])
        finish(n_iter - 1, phase(None, 0, v_rows(n_iter - 1), last_par, last_max)[1])
        o_ref[0] = ot_ref[...].T.astype(o_ref.dtype)
        return

    mx_ref = rest[3]
    j = pl.program_id(1)

    @pl.when(j == 0)
    def _():
        s_ref[last_par] = jnp.zeros(s_ref.shape[1:], F32)
        mx_ref[...] = jnp.zeros_like(mx_ref)
        ot_ref[...] = jnp.zeros_like(ot_ref)

    first_max, accs = phase(streams(0), 0, v_rows(n_iter - 1), last_par, [mx_ref[0], mx_ref[1]])
    finish(n_iter - 1, accs)
    o_ref[0] = ot_ref[...].T.astype(o_ref.dtype)

    @pl.when(j < n_q_tiles)
    def _():
        last_max = middle(first_max)
        mx_ref[0] = last_max[0]
        mx_ref[1] = last_max[1]


def _attention(q, k, vt, *, n_heads, dv, need_ctx, diff=False, lam_p=None, sub_g=None, lam_init=0.0):
    b, n_tok, w = q.shape
    nq = n_tok // TILE - 1
    ow = n_heads * dv
    vrows = n_heads * (dv + V_ONES)
    n_iter = n_heads if diff else n_heads // 2
    assert n_iter % 2 == 0 and (n_tok - TILE) % KEY_BLOCK == 0
    extra_specs, extra_args = [], []
    if diff:
        extra_specs = [pl.BlockSpec(lam_p.shape, lambda i, j: (0, 0)), pl.BlockSpec(sub_g.shape, lambda i, j: (0, 0))]
        extra_args = [lam_p, sub_g]
    kern = functools.partial(_attn_kernel, n_heads=n_heads, dv=dv, diff=diff, lam_init=lam_init)
    name = "diff_attn" if diff else "mla_attn"
    y = pl.pallas_call(
        functools.partial(kern, n_q_tiles=nq),
        grid=(b, nq + 1),
        in_specs=[pl.BlockSpec((1, TILE, w), lambda i, j: (i, jnp.minimum(j, nq - 1) + 1, 0)),
                  pl.BlockSpec((1, n_tok, w), lambda i, j: (i, 0, 0)),
                  pl.BlockSpec((1, vrows, n_tok), lambda i, j: (i, 0, 0))] + extra_specs,
        out_specs=pl.BlockSpec((1, TILE, ow), lambda i, j: (i, jnp.maximum(j - 1, 0) + 1, 0)),
        out_shape=jax.ShapeDtypeStruct((b, n_tok, ow), BF16),
        scratch_shapes=[pltpu.VMEM((ow, TILE), F32), pltpu.VMEM((2, 2, n_tok, TILE), F32),
                        pltpu.VMEM((2, 1, TILE), F32)],
        compiler_params=_cparams(("arbitrary", "arbitrary")),
        name=name,
    )(q, k, vt, *extra_args)
    if not need_ctx:
        return y
    return pl.pallas_call(
        functools.partial(kern, n_q_tiles=None),
        grid=(b,),
        in_specs=[pl.BlockSpec((1, TILE, w), lambda i: (i, 0, 0)),
                  pl.BlockSpec((1, TILE, w), lambda i: (i, 0, 0)),
                  pl.BlockSpec((1, vrows, TILE), lambda i: (i, 0, 0)),
                  pl.BlockSpec(memory_space=pl.ANY)]
                 + [pl.BlockSpec(s.block_shape, lambda i: (0, 0)) for s in extra_specs],
        out_specs=pl.BlockSpec((1, TILE, ow), lambda i: (i, 0, 0)),
        out_shape=jax.ShapeDtypeStruct((b, n_tok, ow), BF16),
        input_output_aliases={3: 0},
        scratch_shapes=[pltpu.VMEM((ow, TILE), F32), pltpu.VMEM((2, 2, TILE, TILE), F32)],
        compiler_params=_cparams(("arbitrary",)),
        name=name + "_ctx",
    )(q, k, vt, y, *extra_args)


def _merge_kernel(*refs, n_src, first):
    (mod_ref, gof_ref, gob_ref, gz_ref, mla_ref, dif_ref, syf_ref, syb_ref, sz_ref,
     gates01_ref, gates23_ref, gng_ref, sng_ref, wb_ref, wo_ref, o_ref) = refs[n_src:]
    o = gof_ref[0] + gob_ref[0]
    z = gz_ref[0]
    ya = []
    for h in range(GDN_HEADS):
        hl = slice(h * GDN_DV, (h + 1) * GDN_DV)
        ya.append(_rms(o[:, hl], gng_ref[...]) * _silu(z[:, hl]))
    ya = jnp.concatenate(ya, axis=-1)
    y = (syf_ref[0] + syb_ref[0]) * _silu(sz_ref[0])
    gsz = SSM_INNER // SSM_GROUPS
    yd = jnp.concatenate([_rms(y[:, g * gsz:(g + 1) * gsz], sng_ref[:, g * gsz:(g + 1) * gsz])
                          for g in range(SSM_GROUPS)], axis=-1)
    ys = (ya, mla_ref[0], dif_ref[0], yd)
    m = None
    for i in range(N_BRANCH):
        gates_ref = gates01_ref if i < 2 else gates23_ref
        gate = _sigmoid(gates_ref[0, :, (i % 2) * D_MODEL:(i % 2 + 1) * D_MODEL].astype(F32))
        term = gate * _dot(ys[i].astype(BF16), wb_ref[i])
        m = term if m is None else m + term
    o_ref[0] = _stream_tile(refs[:n_src], first) + mod_ref[2:3, :] * _dot(m.astype(BF16), wo_ref[...])


def _merge(xs, mods, gdn_o, u, gates, y_mla, y_diff, ssd_y, gng, sng, wb, wo, *, need_ctx):
    first = 0 if need_ctx else 1
    src_specs, src_args, n_tok = _stream_specs(xs, first)
    b, d = mods.shape[0], mods.shape[-1]
    t = n_tok // TILE
    row = lambda w, cb=0: pl.BlockSpec((1, TILE, w), lambda i, j: (i, j + first, cb))
    const = lambda shape: pl.BlockSpec(shape, lambda i, j: (0,) * len(shape))
    aliases = {0: 0} if len(src_args) == 1 else {}
    assert need_ctx or len(src_args) == 1
    return pl.pallas_call(
        functools.partial(_merge_kernel, n_src=len(src_args), first=first),
        grid=(b, t - first),
        in_specs=src_specs + [
                  pl.BlockSpec((None, None, 6, d), lambda i, j: (i, jnp.minimum(j + first, 1), 0, 0)),
                  row(BRANCH_W), row(BRANCH_W), row(BRANCH_W, (C_GDN + 3 * BRANCH_W) // BRANCH_W),
                  row(BRANCH_W), row(BRANCH_W), row(BRANCH_W), row(BRANCH_W), row(BRANCH_W, C_SSMZ // BRANCH_W),
                  row(GATE_COLS // 2, 0), row(GATE_COLS // 2, 1),
                  const((1, GDN_DV)), const((1, SSM_INNER)), const((N_BRANCH, BRANCH_W, d)), const((d, d))],
        out_specs=row(d),
        out_shape=jax.ShapeDtypeStruct((b, n_tok, d), F32),
        input_output_aliases=aliases,
        compiler_params=_cparams(("arbitrary", "arbitrary")),
        name="merge",
    )(*src_args, mods, gdn_o[0], gdn_o[1], u, y_mla, y_diff, ssd_y[0], ssd_y[1], u,
      gates, gates, gng, sng, wb, wo)


def _mlp_kernel(x_ref, mod_ref, g_ref, w1_ref, w2_ref, o_ref):
    x = x_ref[0]
    h = _rms(x, g_ref[...]) * (1.0 + mod_ref[4:5, :]) + mod_ref[3:4, :]
    a = jnp.maximum(_dot(h.astype(BF16), w1_ref[...]), 0.0)
    o_ref[0] = x + mod_ref[5:6, :] * _dot((a * a).astype(BF16), w2_ref[...])


def _mlp(xs, mods, g, w1, w2, *, need_ctx):
    b, n_tok, d = xs.shape
    t = n_tok // TILE
    first = 0 if need_ctx else 1
    const = lambda shape: pl.BlockSpec(shape, lambda i, j: (0,) * len(shape))
    n_out = n_tok - first * TILE
    return pl.pallas_call(
        _mlp_kernel,
        grid=(b, t - first),
        in_specs=[pl.BlockSpec((1, TILE, d), lambda i, j: (i, j + first, 0)),
                  pl.BlockSpec((None, None, 6, d), lambda i, j: (i, jnp.minimum(j + first, 1), 0, 0)),
                  const((1, d)), const((d, D_FF)), const((D_FF, d))],
        out_specs=pl.BlockSpec((1, TILE, d), lambda i, j: (i, j, 0)),
        out_shape=jax.ShapeDtypeStruct((b, n_out, d), F32),
        compiler_params=_cparams(("arbitrary", "arbitrary")),
        name="mlp",
    )(xs, mods, g, w1, w2)


def _repack_w_in(w):
    d = w.shape[0]
    z = lambda n: jnp.zeros((d, n), w.dtype)
    o_mla = GDN_COLS
    o_diff = o_mla + MLA_COLS
    o_ssm = o_diff + DIFF_COLS
    o_gate = MIX_COLS
    gdn_ab = w[:, 2 * GDN_QK + 2 * GDN_VW:GDN_COLS]
    ssm_dt = w[:, o_ssm + 2 * SSM_INNER + 2 * SSM_BC:o_ssm + SSM_COLS]
    kpe = w[:, o_mla + MLA_Q_LORA + MLA_KV_LORA:o_mla + MLA_COLS]
    parts = [
        w[:, 0:2 * GDN_QK + 2 * GDN_VW],
        w[:, o_ssm + SSM_INNER:o_ssm + 2 * SSM_INNER + 2 * SSM_BC],
        w[:, o_diff:o_diff + DIFF_COLS],
        w[:, o_mla:o_mla + MLA_Q_LORA + MLA_KV_LORA], z(MLA_NOPE), kpe, z(LANES - MLA_QK),
        gdn_ab, ssm_dt, z(LANES - 4 * GDN_HEADS - 2 * SSM_HEADS),
        z(C_SSMZ - C_SMALL - LANES),
        w[:, o_ssm:o_ssm + SSM_INNER],
        w[:, o_gate:o_gate + GATE_COLS],
    ]
    out = jnp.concatenate(parts, axis=1)
    assert out.shape[1] == N_IN
    return out.astype(BF16)


def _lane_row(vals, offset):
    row = jnp.zeros((1, LANES), F32)
    return row.at[0, offset:offset + vals.shape[0]].set(vals.astype(F32))


def _rope_tables(n_lat, rot_dim, lane_offset, period):
    rows = n_lat // GRID_W
    row = jnp.repeat(jnp.arange(rows, dtype=F32), GRID_W)
    col = jnp.tile(jnp.arange(GRID_W, dtype=F32), rows)
    quarter = rot_dim // 4
    inv = ROPE_THETA ** (-jnp.arange(quarter, dtype=F32) / quarter)
    ar = row[:, None] * inv
    ac = col[:, None] * inv
    ang = jnp.concatenate([ar, ar, ac, ac], axis=-1)
    cos, sin = jnp.cos(ang), jnp.sin(ang)
    first = (np.arange(rot_dim) % (2 * quarter)) < quarter
    sin_a = jnp.where(first, -sin, 0.0)
    sin_b = jnp.where(first, 0.0, sin)
    reps = LANES // period
    def place(t, fill):
        blk = jnp.full((n_lat, period), fill, F32).at[:, lane_offset:lane_offset + rot_dim].set(t)
        blk = jnp.tile(blk, (1, reps))
        ctx = jnp.full((CTX_LEN, LANES), fill, F32)
        return jnp.concatenate([ctx, blk], axis=0)
    return jnp.stack([place(cos, 1.0), place(sin_a, 0.0), place(sin_b, 0.0)])


def kernel(x, c, ctx, c_ctx, ada_w, ada_b, norm1_g, norm2_g, w_in, gdn_conv, gdn_a_log, gdn_dt_bias, gdn_norm_g,
           mla_q_lora_g, mla_kv_lora_g, mla_w_uq, mla_w_ukv, mla_qn_g, mla_kn_g, diff_qn_g, diff_kn_g,
           diff_lambda, diff_sub_g, ssm_conv, ssm_conv_b, ssm_a_log, ssm_dt_bias, ssm_d, ssm_norm_g, w_branch,
           w_out, mlp_w1, mlp_w2):
    b, n_lat, d = x.shape
    depth = w_in.shape[0]
    assert ctx.shape[1] == CTX_LEN == TILE and n_lat % TILE == 0 and d == D_MODEL
    xs = (ctx, x) if depth > 1 else jnp.concatenate([ctx, x], axis=1)
    tab_mla = _rope_tables(n_lat, MLA_ROPE, MLA_NOPE, LANES)
    tab_diff = _rope_tables(n_lat, DIFF_HD, 0, DIFF_HD)
    rows = b + 8
    cc = jnp.zeros((rows, d), F32).at[:b].set(c).at[b].set(c_ctx)
    row2 = lambda v: v.reshape(1, -1).astype(F32)

    for l in range(depth):
        need_ctx = l < depth - 1
        lam_init = LAMBDA_BASE - LAMBDA_AMP * math.exp(-LAMBDA_RATE * l)
        mod = _ada(cc, ada_w[l], ada_b[l].reshape(1, -1))
        mod_lat = mod[:b].reshape(b, 6, d)
        mod_ctx = jnp.broadcast_to(mod[b].reshape(1, 6, d), (b, 6, d))
        mods = jnp.stack([mod_ctx, mod_lat], axis=1)

        u, gates = _inproj(xs, mods, row2(norm1_g[l]), _repack_w_in(w_in[l]))

        gdn_o = _gdn_scan(_gdn_prep(u, gdn_conv[l].astype(F32),
                                    _lane_row(gdn_a_log[l].reshape(-1), SM_A),
                                    _lane_row(gdn_dt_bias[l].reshape(-1), SM_A)))

        wq = mla_w_uq[l].reshape(MLA_Q_LORA, MLA_HEADS, MLA_QK)
        wq = jnp.pad(wq, ((0, 0), (0, 0), (0, LANES - MLA_QK))).reshape(MLA_Q_LORA, MLA_HEADS * LANES)
        wkv = mla_w_ukv[l].reshape(MLA_KV_LORA, MLA_HEADS, MLA_NOPE + MLA_V)
        wk = jnp.pad(wkv[:, :, :MLA_NOPE], ((0, 0), (0, 0), (0, LANES - MLA_NOPE))).reshape(MLA_KV_LORA, MLA_HEADS * LANES)
        wvt = wkv[:, :, MLA_NOPE:].reshape(MLA_KV_LORA, MLA_HEADS * MLA_V).T
        pad_g = lambda g: jnp.pad(g.astype(F32), (0, LANES - MLA_QK)).reshape(1, LANES)
        q_m, k_m, vt_m = _mla_prep(u, tab_mla, row2(mla_q_lora_g[l]), row2(mla_kv_lora_g[l]),
                                   wq.astype(BF16), wk.astype(BF16), wvt.astype(BF16),
                                   pad_g(mla_qn_g[l]), pad_g(mla_kn_g[l]))
        y_mla = _attention(q_m, k_m, vt_m, n_heads=MLA_HEADS, dv=MLA_V, need_ctx=need_ctx)

        rep_g = lambda g: jnp.tile(g.astype(F32), LANES // DIFF_HD).reshape(1, LANES)
        q_d, k_d, vt_d = _diff_prep(u, tab_diff, rep_g(diff_qn_g[l]), rep_g(diff_kn_g[l]))
        y_diff = _attention(q_d, k_d, vt_d, n_heads=DIFF_HEADS, dv=2 * DIFF_HD, need_ctx=need_ctx, diff=True,
                            lam_p=diff_lambda[l].astype(F32), sub_g=diff_sub_g[l].astype(F32).reshape(-1, 1),
                            lam_init=lam_init)

        ssd_y = _ssd(u, ssm_conv[l].astype(F32), row2(ssm_conv_b[l]),
                     _lane_row(ssm_a_log[l].reshape(-1), SM_DT), _lane_row(ssm_dt_bias[l].reshape(-1), SM_DT),
                     jnp.repeat(ssm_d[l].astype(F32), SSM_HEAD_DIM).reshape(1, -1))

        xs = _merge(xs, mods, gdn_o, u, gates, y_mla, y_diff, ssd_y, row2(gdn_norm_g[l]), row2(ssm_norm_g[l]),
                    w_branch[l].astype(BF16), w_out[l].astype(BF16), need_ctx=need_ctx)
        xs = _mlp(xs, mods, row2(norm2_g[l]), mlp_w1[l].astype(BF16), mlp_w2[l].astype(BF16), need_ctx=need_ctx)
    return xs
```

```python
import functools
import math

import jax
import jax.numpy as jnp
import numpy as np
from jax import lax
from jax.experimental import pallas as pl
from jax.experimental.pallas import tpu as pltpu

F32 = jnp.float32
BF16 = jnp.bfloat16

D_MODEL = 1024
CTX_LEN = 256
GRID_W = 64
ROPE_THETA = 10000.0
NORM_EPS = 1e-6
CONV_K = 5
N_BRANCH = 4
D_FF = 4 * D_MODEL
GDN_HEADS = 4
GDN_DK = 128
GDN_DV = 128
MLA_HEADS = 8
MLA_NOPE = 64
MLA_ROPE = 32
MLA_V = 64
MLA_QK = MLA_NOPE + MLA_ROPE
MLA_Q_LORA = 384
MLA_KV_LORA = 256
DIFF_HEADS = 4
DIFF_HD = 64
LAMBDA_BASE = 0.8
LAMBDA_AMP = 0.6
LAMBDA_RATE = 0.3
SSM_HEADS = 8
SSM_HEAD_DIM = 64
SSM_GROUPS = 2
SSM_STATE = 128
GDN_QK = GDN_HEADS * GDN_DK
GDN_VW = GDN_HEADS * GDN_DV
DIFF_QK = DIFF_HEADS * 2 * DIFF_HD
DIFF_VW = DIFF_HEADS * 2 * DIFF_HD
SSM_INNER = SSM_HEADS * SSM_HEAD_DIM
SSM_BC = SSM_GROUPS * SSM_STATE
BRANCH_W = 512
GDN_COLS = 2 * GDN_QK + 2 * GDN_VW + 4 * GDN_HEADS
MLA_COLS = MLA_Q_LORA + MLA_KV_LORA + MLA_ROPE
DIFF_COLS = 2 * DIFF_QK + DIFF_VW
SSM_COLS = 2 * SSM_INNER + 2 * SSM_BC + 2 * SSM_HEADS
MIX_COLS = GDN_COLS + MLA_COLS + DIFF_COLS + SSM_COLS
GATE_COLS = N_BRANCH * D_MODEL

LANES = 128
TILE = 256
CHUNK = 64
CPT = TILE // CHUNK
HALO = 8
VMEM_LIMIT = 56 * 1024 * 1024
V_ONES = 16
KEY_BLOCK = 256
LOG2E = 1.4426950408889634

C_GDN = 0
C_XBC = 2048
C_DIFF = 3072
C_MLA = 4608
C_SMALL = 5376
C_SSMZ = 5632
C_GATE = 6144
N_IN = 10240
IN_TN = 2048

SM_A = 0
SM_B = 8
SM_DT = 16


def _cparams(sem):
    return pltpu.CompilerParams(dimension_semantics=sem, vmem_limit_bytes=VMEM_LIMIT)


def _dot(a, b):
    return jnp.dot(a, b, preferred_element_type=F32)


def _dot_nt(a, b):
    return lax.dot_general(a, b, (((1,), (1,)), ((), ())), preferred_element_type=F32)


def _dot_tn(a, b):
    return lax.dot_general(a, b, (((0,), (0,)), ((), ())), preferred_element_type=F32)


def _split3(x):
    x1 = x.astype(BF16)
    r1 = x - x1.astype(F32)
    x2 = r1.astype(BF16)
    r2 = r1 - x2.astype(F32)
    return x1, x2, r2.astype(BF16)


def _dot_exact_lhs(a_bf, x):
    x1, x2, x3 = _split3(x)
    return _dot(a_bf, x1) + _dot(a_bf, x2) + _dot(a_bf, x3)


def _dot_exact_rhs(x, b_bf):
    x1, x2, x3 = _split3(x)
    return _dot(x1, b_bf) + _dot(x2, b_bf) + _dot(x3, b_bf)


def _sigmoid(x):
    return 1.0 / (1.0 + jnp.exp(-x))


def _silu(x):
    return x * _sigmoid(x)


def _softplus(x):
    return jnp.maximum(x, 0.0) + jnp.log1p(jnp.exp(-jnp.abs(x)))


def _iota(shape, dim):
    return lax.broadcasted_iota(jnp.int32, shape, dim)


def _chunk_tri(n, upper):
    r = _iota((n, n), 0)
    c = _iota((n, n), 1)
    same = (r // CHUNK) == (c // CHUNK)
    tri = (r <= c) if upper else (r >= c)
    return jnp.where(same & tri, 1.0, 0.0).astype(BF16)


def _expander(n_rows, first_row, n_groups, width):
    r = _iota((n_rows, n_groups * width), 0)
    c = _iota((n_rows, n_groups * width), 1)
    return jnp.where(r == first_row + c // width, 1.0, 0.0).astype(BF16)


def _ada_kernel(c_ref, w_ref, b_ref, o_ref):
    a = _silu(c_ref[...])
    a1, a2, a3 = _split3(a)
    w1, w2, w3 = _split3(w_ref[...])
    acc = _dot(a1, w1) + (_dot(a1, w2) + _dot(a2, w1)) + (_dot(a1, w3) + _dot(a2, w2) + _dot(a3, w1))
    o_ref[...] = acc + b_ref[...]


def _ada(cc, w, b):
    rows, d = cc.shape
    n = w.shape[1]
    tn = 1536
    return pl.pallas_call(
        _ada_kernel,
        grid=(n // tn,),
        in_specs=[pl.BlockSpec((rows, d), lambda j: (0, 0)),
                  pl.BlockSpec((d, tn), lambda j: (0, j)),
                  pl.BlockSpec((1, tn), lambda j: (0, j))],
        out_specs=pl.BlockSpec((rows, tn), lambda j: (0, j)),
        out_shape=jax.ShapeDtypeStruct((rows, n), F32),
        compiler_params=_cparams(("arbitrary",)),
        name="ada",
    )(cc, w, b)


def _rms(x, g):
    ms = jnp.mean(x * x, axis=-1, keepdims=True)
    return x * lax.rsqrt(ms + NORM_EPS) * g


def _stream_specs(xs, first=0):
    if isinstance(xs, tuple):
        ctx, x = xs
        d = x.shape[-1]
        specs = [pl.BlockSpec((1, TILE, d), lambda i, j: (i, 0, 0)),
                 pl.BlockSpec((1, TILE, d), lambda i, j: (i, jnp.maximum(j + first - 1, 0), 0))]
        return specs, [ctx, x], ctx.shape[1] + x.shape[1]
    d = xs.shape[-1]
    return [pl.BlockSpec((1, TILE, d), lambda i, j: (i, j + first, 0))], [xs], xs.shape[1]


def _stream_tile(refs, first=0):
    if len(refs) == 2:
        return jnp.where(pl.program_id(1) + first == 0, refs[0][0], refs[1][0])
    return refs[0][0]


def _inproj_kernel(*refs, n_src):
    mod_ref, g_ref, w_ref, o_ref, gate_ref = refs[n_src:]
    h = _rms(_stream_tile(refs[:n_src]), g_ref[...])
    hb = (h * (1.0 + mod_ref[1:2, :]) + mod_ref[0:1, :]).astype(BF16)
    for c in range(0, C_GATE, IN_TN):
        o_ref[0, :, c:c + IN_TN] = _dot(hb, w_ref[:, c:c + IN_TN])
    for c in range(0, GATE_COLS, IN_TN):
        gate_ref[0, :, c:c + IN_TN] = _dot(hb, w_ref[:, C_GATE + c:C_GATE + c + IN_TN]).astype(BF16)


def _inproj(xs, mods, g, w):
    src_specs, src_args, n_tok = _stream_specs(xs)
    b, d = mods.shape[0], mods.shape[-1]
    t = n_tok // TILE
    return pl.pallas_call(
        functools.partial(_inproj_kernel, n_src=len(src_args)),
        grid=(b, t),
        in_specs=src_specs + [
                  pl.BlockSpec((None, None, 6, d), lambda i, j: (i, jnp.minimum(j, 1), 0, 0)),
                  pl.BlockSpec((1, d), lambda i, j: (0, 0)),
                  pl.BlockSpec((d, N_IN), lambda i, j: (0, 0), pipeline_mode=pl.Buffered(1))],
        out_specs=[pl.BlockSpec((1, TILE, C_GATE), lambda i, j: (i, j, 0)),
                   pl.BlockSpec((1, TILE, GATE_COLS), lambda i, j: (i, j, 0))],
        out_shape=[jax.ShapeDtypeStruct((b, n_tok, C_GATE), F32),
                   jax.ShapeDtypeStruct((b, n_tok, GATE_COLS), BF16)],
        compiler_params=_cparams(("arbitrary", "arbitrary")),
        name="inproj",
    )(*src_args, mods, g, w)


def _halo_specs(width, col_block, n_tiles, tile_of):
    rpt = TILE // HALO
    last = n_tiles * rpt - 1
    main = pl.BlockSpec((1, TILE, width), lambda *ids: (ids[0], tile_of(*ids), col_block))
    prev = pl.BlockSpec((1, HALO, width),
                        lambda *ids: (ids[0], jnp.maximum(tile_of(*ids) * rpt - 1, 0), col_block))
    nxt = pl.BlockSpec((1, HALO, width),
                       lambda *ids: (ids[0], jnp.minimum(tile_of(*ids) * rpt + rpt, last), col_block))
    return main, prev, nxt


def _conv_tile(main_ref, prev_ref, next_ref, w_ref, ext_ref, tile, n_tiles):
    prev_ok = (tile >= 2).astype(F32)
    next_ok = jnp.logical_and(tile >= 1, tile < n_tiles - 1).astype(F32)
    ext_ref[0:HALO, :] = prev_ref[0] * prev_ok
    ext_ref[HALO:HALO + TILE, :] = main_ref[0]
    ext_ref[HALO + TILE:2 * HALO + TILE, :] = next_ref[0] * next_ok
    half = CONV_K // 2
    acc = None
    for k in range(CONV_K):
        term = w_ref[k:k + 1, :] * ext_ref[HALO - half + k:HALO - half + k + TILE, :]
        acc = term if acc is None else acc + term
    return acc


def _block_inverse_many(mats):
    n = mats[0].shape[0]
    ri = _iota((n, n), 0)
    ci = _iota((n, n), 1)
    eye = jnp.where(ri == ci, 1.0, 0.0).astype(F32)
    pair = (ri // 2) == (ci // 2)
    ts = [eye - jnp.where(pair, a, 0.0) for a in mats]
    abs_ = [a.astype(BF16) for a in mats]
    zero = jnp.zeros((n, n), BF16)
    size = 2
    while size < CHUNK:
        off = ((ri // (2 * size)) == (ci // (2 * size))) & ((ri // size) != (ci // size))
        tbs = [t.astype(BF16) for t in ts]
        xs = [_dot(jnp.where(off, ab, zero), tb).astype(BF16) for ab, tb in zip(abs_, tbs)]
        ts = [t - _dot(tb, x) for t, tb, x in zip(ts, tbs, xs)]
        size *= 2
    return ts


def _gdn_prep_kernel(main_ref, prev_ref, next_ref, sm_ref, cw_ref, alog_ref, dtb_ref,
                     u0_ref, wq0_ref, kd0_ref, at0_ref, gl0_ref,
                     u1_ref, wq1_ref, kd1_ref, at1_ref, gl1_ref,
                     ext_ref, *, n_tiles):
    tile = pl.program_id(1)
    qkv = _silu(_conv_tile(main_ref, prev_ref, next_ref, cw_ref, ext_ref, tile, n_tiles))
    sm = sm_ref[0]
    g_all = -jnp.exp(alog_ref[...]) * _softplus(sm + dtb_ref[...])
    beta_all = _sigmoid(sm)
    nh = GDN_HEADS
    exp_b = _expander(LANES, SM_B, 2 * nh, LANES)
    beta_x = _dot_exact_rhs(beta_all, exp_b)
    outs = ((u0_ref, wq0_ref, kd0_ref, at0_ref, gl0_ref), (u1_ref, wq1_ref, kd1_ref, at1_ref, gl1_ref))
    ri = _iota((TILE, TILE), 0)
    ci = _iota((TILE, TILE), 1)
    same = (ri // CHUNK) == (ci // CHUNK)
    cums, masks = [], []
    for d in range(2):
        cum = _dot_exact_lhs(_chunk_tri(TILE, upper=(d == 1)), g_all)
        cums.append((_dot_exact_rhs(cum, _expander(LANES, SM_A + d * nh, nh, LANES)), cum.T))
        masks.append((same & ((ri >= ci) if d == 0 else (ri <= ci)), same & ((ri > ci) if d == 0 else (ri < ci))))

    chains, a_mats = [], []
    for h in range(nh):
        q = qkv[:, h * GDN_DK:(h + 1) * GDN_DK]
        k = qkv[:, GDN_QK + h * GDN_DK:GDN_QK + (h + 1) * GDN_DK]
        v = qkv[:, 2 * GDN_QK + h * GDN_DV:2 * GDN_QK + (h + 1) * GDN_DV]
        q = q * lax.rsqrt(jnp.sum(q * q, axis=-1, keepdims=True) + 1e-6) * (GDN_DK ** -0.5)
        k = k * lax.rsqrt(jnp.sum(k * k, axis=-1, keepdims=True) + 1e-6)
        kb = k.astype(BF16)
        kq = _dot_nt(jnp.concatenate([kb, q.astype(BF16)], axis=0), kb)
        for d in range(2):
            col = d * nh + h
            lanes = slice(col * LANES, (col + 1) * LANES)
            cum_x, cum_t = cums[d]
            incl, strict = masks[d]
            g_col = cum_x[:, h * LANES:(h + 1) * LANES]
            b_col = beta_x[:, lanes]
            diff = jnp.concatenate([g_col, g_col], axis=1) - cum_t[SM_A + col:SM_A + col + 1, :]
            decay = jnp.where(incl, jnp.exp(jnp.where(incl, diff, 0.0)), 0.0)
            a_mats.append(jnp.where(strict, kq[:TILE] * jnp.concatenate([b_col, b_col], axis=1) * decay, 0.0))
            e_g = jnp.exp(g_col)
            rhs = jnp.concatenate([v * b_col, k * b_col * e_g], axis=1).astype(BF16)
            chains.append((h, d, q, k, g_col, e_g, rhs, kq[TILE:] * decay))

    t_mats = _block_inverse_many(a_mats)
    for (h, d, q, k, g_col, e_g, rhs, attn), t_mat in zip(chains, t_mats):
        uw = _dot(t_mat.astype(BF16), rhs)
        qg = (q * e_g).astype(BF16)
        u_ref, wq_ref, kd_ref, at_ref, gl_ref = outs[d]
        hl = slice(h * LANES, (h + 1) * LANES)
        for c in range(CPT):
            rows = slice(c * CHUNK, (c + 1) * CHUNK)
            last = c * CHUNK + (CHUNK - 1 if d == 0 else 0)
            g_last = g_col[last:last + 1, :]
            slot = c if d == 0 else CPT - 1 - c
            u_ref[0, 0, slot, :, hl] = uw[rows, :GDN_DV]
            wq_ref[0, 0, slot, 0:CHUNK, hl] = uw[rows, GDN_DV:].astype(BF16)
            wq_ref[0, 0, slot, CHUNK:2 * CHUNK, hl] = qg[rows]
            kd_ref[0, 0, slot, :, hl] = (k[rows] * jnp.exp(g_last - g_col[rows])).astype(BF16)
            at_ref[0, 0, slot, :, h * CHUNK:(h + 1) * CHUNK] = attn[rows, rows].astype(BF16)
            gl_ref[0, 0, slot, :, hl] = jnp.exp(g_last)


def _bwd_block(j, n_tiles):
    return jnp.where(j == 0, 0, n_tiles - j)


def _gdn_prep(u, conv_w, alog_row, dtb_row):
    b, n_tok, _ = u.shape
    t = n_tok // TILE
    w = GDN_VW
    cw = 2 * GDN_QK + GDN_VW
    main, prev, nxt = _halo_specs(cw, 0, t, lambda i, j: j)
    fwd = lambda i, j: (i, j, 0, 0, 0)
    bwd = lambda i, j: (i, _bwd_block(j, t), 0, 0, 0)

    def outs(imap):
        return [pl.BlockSpec((1, 1, CPT, CHUNK, w), imap),
                pl.BlockSpec((1, 1, CPT, 2 * CHUNK, w), imap),
                pl.BlockSpec((1, 1, CPT, CHUNK, w), imap),
                pl.BlockSpec((1, 1, CPT, CHUNK, GDN_HEADS * CHUNK), imap),
                pl.BlockSpec((1, 1, CPT, 1, w), imap)]

    shapes = [jax.ShapeDtypeStruct((b, t, CPT, CHUNK, w), F32),
              jax.ShapeDtypeStruct((b, t, CPT, 2 * CHUNK, w), BF16),
              jax.ShapeDtypeStruct((b, t, CPT, CHUNK, w), BF16),
              jax.ShapeDtypeStruct((b, t, CPT, CHUNK, GDN_HEADS * CHUNK), BF16),
              jax.ShapeDtypeStruct((b, t, CPT, 1, w), F32)]
    return pl.pallas_call(
        functools.partial(_gdn_prep_kernel, n_tiles=t),
        grid=(b, t),
        in_specs=[main, prev, nxt,
                  pl.BlockSpec((1, TILE, LANES), lambda i, j: (i, j, C_SMALL // LANES)),
                  pl.BlockSpec((CONV_K, cw), lambda i, j: (0, 0)),
                  pl.BlockSpec((1, LANES), lambda i, j: (0, 0)),
                  pl.BlockSpec((1, LANES), lambda i, j: (0, 0))],
        out_specs=outs(fwd) + outs(bwd),
        out_shape=shapes + shapes,
        scratch_shapes=[pltpu.VMEM((TILE + 2 * HALO, cw), F32)],
        compiler_params=_cparams(("arbitrary", "arbitrary")),
        name="gdn_prep",
    )(u, u, u, u, conv_w, alog_row, dtb_row)


def _gdn_scan_kernel(u0_ref, wq0_ref, kd0_ref, at0_ref, gl0_ref,
                     u1_ref, wq1_ref, kd1_ref, at1_ref, gl1_ref,
                     of_ref, ob_ref, s_ref):
    @pl.when(pl.program_id(1) == 0)
    def _():
        s_ref[...] = jnp.zeros_like(s_ref)

    ins = ((u0_ref, wq0_ref, kd0_ref, at0_ref, gl0_ref), (u1_ref, wq1_ref, kd1_ref, at1_ref, gl1_ref))
    chains = [(d, h) for d in range(2) for h in range(GDN_HEADS)]
    hls = [slice(h * LANES, (h + 1) * LANES) for _, h in chains]
    states = [s_ref[d, h] for d, h in chains]
    for slot in range(CPT):
        rs = [_dot(ins[d][1][0, 0, slot, :, hl], s.astype(BF16)) for (d, _), hl, s in zip(chains, hls, states)]
        vbs = [(ins[d][0][0, 0, slot, :, hl] - r[0:CHUNK]).astype(BF16) for (d, _), hl, r in zip(chains, hls, rs)]
        outs = [r[CHUNK:2 * CHUNK] + _dot(ins[d][3][0, 0, slot, :, h * CHUNK:(h + 1) * CHUNK], vb)
                for (d, h), r, vb in zip(chains, rs, vbs)]
        states = [s * ins[d][4][0, 0, slot, :, hl] + _dot_tn(ins[d][2][0, 0, slot, :, hl], vb)
                  for (d, _), hl, s, vb in zip(chains, hls, states, vbs)]
        for (d, _), hl, o in zip(chains, hls, outs):
            c = slot if d == 0 else CPT - 1 - slot
            (of_ref if d == 0 else ob_ref)[0, c * CHUNK:(c + 1) * CHUNK, hl] = o
    for (d, h), s in zip(chains, states):
        s_ref[d, h] = s


def _gdn_scan(prep):
    b, t = prep[0].shape[:2]
    w = GDN_VW
    imap = lambda i, j: (i, j, 0, 0, 0)
    specs = [pl.BlockSpec((1, 1, CPT, CHUNK, w), imap),
             pl.BlockSpec((1, 1, CPT, 2 * CHUNK, w), imap),
             pl.BlockSpec((1, 1, CPT, CHUNK, w), imap),
             pl.BlockSpec((1, 1, CPT, CHUNK, GDN_HEADS * CHUNK), imap),
             pl.BlockSpec((1, 1, CPT, 1, w), imap)]
    return pl.pallas_call(
        _gdn_scan_kernel,
        grid=(b, t),
        in_specs=specs + specs,
        out_specs=[pl.BlockSpec((1, TILE, w), lambda i, j: (i, j, 0)),
                   pl.BlockSpec((1, TILE, w), lambda i, j: (i, _bwd_block(j, t), 0))],
        out_shape=[jax.ShapeDtypeStruct((b, t * TILE, w), F32)] * 2,
        scratch_shapes=[pltpu.VMEM((2, GDN_HEADS, GDN_DK, GDN_DV), F32)],
        compiler_params=_cparams(("arbitrary", "arbitrary")),
        name="gdn_scan",
    )(*prep)


def _ssd_direction(d, xbc, sm, y_ref, st_ref, alog_ref, dtb_ref, dskip_ref):
    nh, hd, ng = SSM_HEADS, SSM_HEAD_DIM, SSM_GROUPS
    hpg = nh // ng
    gw = hpg * hd
    xs = xbc[:, :SSM_INNER]
    dt_all = _softplus(sm + dtb_ref[...])
    da_all = dt_all * (-jnp.exp(alog_ref[...]))
    cum = _dot_exact_lhs(_chunk_tri(TILE, upper=(d == 1)), da_all)
    cum_t = cum.T
    expand = _expander(LANES, SM_DT + d * nh, nh, hd)
    dt_x = _dot_exact_rhs(dt_all, expand)
    cum_x = _dot_exact_rhs(cum, expand)
    xdt = xs * dt_x
    ri = _iota((CHUNK, gw), 0)
    ci = _iota((CHUNK, gw), 1) % CHUNK
    incl = (ri >= ci) if d == 0 else (ri <= ci)
    head_diag = (_iota((gw, gw), 0) // hd) == (_iota((gw, gw), 1) // hd)
    zero_bf = jnp.zeros((gw, gw), BF16)
    for step in range(CPT):
        c = step if d == 0 else CPT - 1 - step
        rows = slice(c * CHUNK, (c + 1) * CHUNK)
        last = c * CHUNK + (CHUNK - 1 if d == 0 else 0)
        cum_c = cum_x[rows]
        cum_last = cum_x[last:last + 1]
        decay_states = jnp.exp(cum_last - cum_c)
        in_decay = jnp.exp(cum_c)
        chunk_decay = jnp.exp(cum_last)
        for g in range(ng):
            gl = slice(g * gw, (g + 1) * gw)
            bm = xbc[rows, SSM_INNER + g * SSM_STATE:SSM_INNER + (g + 1) * SSM_STATE].astype(BF16)
            cm = xbc[rows, SSM_INNER + SSM_BC + g * SSM_STATE:SSM_INNER + SSM_BC + (g + 1) * SSM_STATE].astype(BF16)
            state = st_ref[d, g]
            y_off = _dot(cm, state.astype(BF16)) * in_decay[:, gl]
            xdt_c = xdt[rows, gl]
            st_ref[d, g] = state * chunk_decay[:, gl] + _dot_tn(bm, (xdt_c * decay_states[:, gl]).astype(BF16))
            scores = _dot_nt(cm, jnp.concatenate([bm] * hpg, axis=0))
            col0 = SM_DT + d * nh + g * hpg
            row_terms = jnp.concatenate([cum_t[col0 + hh:col0 + hh + 1, rows] for hh in range(hpg)], axis=1)
            seg = cum_c[:, gl] - row_terms
            lmat = jnp.where(incl, jnp.exp(jnp.where(incl, seg, 0.0)), 0.0)
            x_bd = jnp.where(head_diag, jnp.concatenate([xdt_c.astype(BF16)] * hpg, axis=0), zero_bf)
            y = _dot((scores * lmat).astype(BF16), x_bd) + y_off
            if d == 0:
                y = y + dskip_ref[:, gl] * xs[rows, gl]
            y_ref[0, rows, gl] = y


def _ssd_kernel(m_ref, p_ref, n_ref, sm_ref, cw_ref, cb_ref, alog_ref, dtb_ref, dskip_ref,
                yf_ref, yb_ref, ext_ref, st_ref, cache_ref, *, n_tiles):
    s = pl.program_id(1)

    @pl.when(s == 0)
    def _():
        st_ref[...] = jnp.zeros_like(st_ref)

    @pl.when(s < n_tiles)
    def _():
        xbc = _silu(_conv_tile(m_ref, p_ref, n_ref, cw_ref, ext_ref, s, n_tiles) + cb_ref[...])
        cache_ref[s] = xbc
        _ssd_direction(0, xbc, sm_ref[0], yf_ref, st_ref, alog_ref, dtb_ref, dskip_ref)

    @pl.when(s >= n_tiles)
    def _():
        tile = _bwd_block(s - n_tiles, n_tiles)
        _ssd_direction(1, cache_ref[tile], sm_ref[0], yb_ref, st_ref, alog_ref, dtb_ref, dskip_ref)


def _ssd(u, conv_w, conv_b, alog_row, dtb_row, dskip_row):
    b, n_tok, _ = u.shape
    t = n_tok // TILE
    cw = SSM_INNER + 2 * SSM_BC
    main, prev, nxt = _halo_specs(cw, C_XBC // cw, t, lambda i, s: jnp.minimum(s, t - 1))
    tile_of = lambda s: jnp.where(s < t, s, _bwd_block(s - t, t))
    const = lambda r, c: pl.BlockSpec((r, c), lambda i, s: (0, 0))
    return pl.pallas_call(
        functools.partial(_ssd_kernel, n_tiles=t),
        grid=(b, 2 * t),
        in_specs=[main, prev, nxt,
                  pl.BlockSpec((1, TILE, LANES), lambda i, s: (i, tile_of(s), C_SMALL // LANES)),
                  const(CONV_K, cw), const(1, cw), const(1, LANES), const(1, LANES), const(1, SSM_INNER)],
        out_specs=[pl.BlockSpec((1, TILE, SSM_INNER), lambda i, s: (i, jnp.minimum(s, t - 1), 0)),
                   pl.BlockSpec((1, TILE, SSM_INNER), lambda i, s: (i, jnp.where(s < t, 0, _bwd_block(s - t, t)), 0))],
        out_shape=[jax.ShapeDtypeStruct((b, n_tok, SSM_INNER), F32)] * 2,
        scratch_shapes=[pltpu.VMEM((TILE + 2 * HALO, cw), F32),
                        pltpu.VMEM((2, SSM_GROUPS, SSM_STATE, SSM_INNER // SSM_GROUPS), F32),
                        pltpu.VMEM((t, TILE, cw), F32)],
        compiler_params=_cparams(("arbitrary", "arbitrary")),
        name="ssd",
    )(u, u, u, u, conv_w, conv_b, alog_row, dtb_row, dskip_row)


PAIR = 2 * LANES


def _group_ones(group):
    r = _iota((PAIR, PAIR), 0)
    c = _iota((PAIR, PAIR), 1)
    return jnp.where((r // group) == (c // group), 1.0, 0.0).astype(BF16)


def _rope_perm(period, lane_offset, rot_dim):
    r = _iota((PAIR, PAIR), 0)
    c = _iota((PAIR, PAIR), 1)
    q = rot_dim // 4
    cl = c % period - lane_offset
    src = jnp.where((cl % (2 * q)) < q, c + q, c - q)
    return jnp.where((cl >= 0) & (cl < rot_dim) & (r == src), 1.0, 0.0).astype(BF16)


def _norm_rope(x, gain, ones_bd, perm, tab_ref, n_real):
    ss = _dot((x * x).astype(BF16), ones_bd)
    xn = x * lax.rsqrt(ss / n_real + NORM_EPS) * gain
    hi = xn.astype(BF16)
    lo = (xn - hi.astype(F32)).astype(BF16)
    rot = _dot(hi, perm) + _dot(lo, perm)
    cos = jnp.concatenate([tab_ref[0]] * 2, axis=1)
    sin = jnp.concatenate([tab_ref[1] + tab_ref[2]] * 2, axis=1)
    return xn * cos + rot * sin


def _store_vt(vt_ref, vt, n_heads, dv):
    ones = jnp.ones((V_ONES, vt.shape[1]), BF16)
    for h in range(n_heads):
        base = h * (dv + V_ONES)
        vt_ref[0, base:base + dv, :] = vt[h * dv:(h + 1) * dv].astype(BF16)
        vt_ref[0, base + dv:base + dv + V_ONES, :] = ones


def _mla_prep_kernel(u_ref, tab_ref, qlg_ref, kvlg_ref, wq_ref, wk_ref, wvt_ref, qng_ref, kng_ref,
                     q_ref, k_ref, vt_ref):
    u = u_ref[0]
    cq = _rms(u[:, :MLA_Q_LORA], qlg_ref[...]).astype(BF16)
    ckv = _rms(u[:, MLA_Q_LORA:MLA_Q_LORA + MLA_KV_LORA], kvlg_ref[...]).astype(BF16)
    kpe = u[:, MLA_Q_LORA + MLA_KV_LORA:]
    q_all = _dot(cq, wq_ref[...])
    k_all = _dot(ckv, wk_ref[...])
    _store_vt(vt_ref, _dot_nt(wvt_ref[...], ckv), MLA_HEADS, MLA_V)
    scale = MLA_QK ** -0.5 * LOG2E
    ones_bd = _group_ones(LANES)
    perm = _rope_perm(LANES, MLA_NOPE, MLA_ROPE)
    qg = jnp.concatenate([qng_ref[...]] * 2, axis=1)
    kg = jnp.concatenate([kng_ref[...]] * 2, axis=1)
    kpe2 = jnp.concatenate([kpe, kpe], axis=1)
    for h in range(0, MLA_HEADS, 2):
        hl = slice(h * LANES, (h + 2) * LANES)
        q_ref[0, :, hl] = (_norm_rope(q_all[:, hl], qg, ones_bd, perm, tab_ref, MLA_QK) * scale).astype(BF16)
        k_pair = _norm_rope(k_all[:, hl] + kpe2, kg, ones_bd, perm, tab_ref, MLA_QK).astype(BF16)
        k_ref[0, h] = k_pair[:, :LANES]
        k_ref[0, h + 1] = k_pair[:, LANES:]


def _mla_prep(u, tab, qlg, kvlg, wq, wk, wvt, qng, kng):
    b, n_tok, _ = u.shape
    t = n_tok // TILE
    w = MLA_HEADS * LANES
    cw = MLA_Q_LORA + MLA_KV_LORA + LANES
    const = lambda r, c: pl.BlockSpec((r, c), lambda i, j: (0, 0))
    return pl.pallas_call(
        _mla_prep_kernel,
        grid=(b, t),
        in_specs=[pl.BlockSpec((1, TILE, cw), lambda i, j: (i, j, C_MLA // cw)),
                  pl.BlockSpec((3, TILE, LANES), lambda i, j: (0, j, 0)),
                  const(1, MLA_Q_LORA), const(1, MLA_KV_LORA), const(MLA_Q_LORA, w), const(MLA_KV_LORA, w),
                  const(MLA_HEADS * MLA_V, MLA_KV_LORA), const(1, LANES), const(1, LANES)],
        out_specs=[pl.BlockSpec((1, TILE, w), lambda i, j: (i, j, 0)),
                   pl.BlockSpec((1, MLA_HEADS, TILE, LANES), lambda i, j: (i, 0, j, 0)),
                   pl.BlockSpec((1, MLA_HEADS * (MLA_V + V_ONES), TILE), lambda i, j: (i, 0, j))],
        out_shape=[jax.ShapeDtypeStruct((b, n_tok, w), BF16),
                   jax.ShapeDtypeStruct((b, MLA_HEADS, n_tok, LANES), BF16),
                   jax.ShapeDtypeStruct((b, MLA_HEADS * (MLA_V + V_ONES), n_tok), BF16)],
        compiler_params=_cparams(("arbitrary", "arbitrary")),
        name="mla_prep",
    )(u, tab, qlg, kvlg, wq, wk, wvt, qng, kng)


def _diff_prep_kernel(u_ref, tab_ref, qng_ref, kng_ref, q_ref, k_ref, vt_ref):
    u = u_ref[0]
    ones_bd = _group_ones(DIFF_HD)
    perm = _rope_perm(DIFF_HD, 0, DIFF_HD)
    qg = jnp.concatenate([qng_ref[...]] * 2, axis=1)
    kg = jnp.concatenate([kng_ref[...]] * 2, axis=1)
    scale = DIFF_HD ** -0.5 * LOG2E
    for c in range(0, DIFF_QK, PAIR):
        q_ref[0, :, c:c + PAIR] = (_norm_rope(u[:, c:c + PAIR], qg, ones_bd, perm, tab_ref, DIFF_HD) * scale).astype(BF16)
        k_pair = _norm_rope(u[:, DIFF_QK + c:DIFF_QK + c + PAIR], kg, ones_bd, perm, tab_ref, DIFF_HD).astype(BF16)
        k_ref[0, c // LANES] = k_pair[:, :LANES]
        k_ref[0, c // LANES + 1] = k_pair[:, LANES:]
    _store_vt(vt_ref, u[:, 2 * DIFF_QK:].T, DIFF_HEADS, 2 * DIFF_HD)


def _diff_prep(u, tab, qng, kng):
    b, n_tok, _ = u.shape
    t = n_tok // TILE
    const = lambda r, c: pl.BlockSpec((r, c), lambda i, j: (0, 0))
    return pl.pallas_call(
        _diff_prep_kernel,
        grid=(b, t),
        in_specs=[pl.BlockSpec((1, TILE, DIFF_COLS), lambda i, j: (i, j, C_DIFF // DIFF_COLS)),
                  pl.BlockSpec((3, TILE, LANES), lambda i, j: (0, j, 0)),
                  const(1, LANES), const(1, LANES)],
        out_specs=[pl.BlockSpec((1, TILE, DIFF_QK), lambda i, j: (i, j, 0)),
                   pl.BlockSpec((1, DIFF_HEADS, TILE, LANES), lambda i, j: (i, 0, j, 0)),
                   pl.BlockSpec((1, DIFF_HEADS * (2 * DIFF_HD + V_ONES), TILE), lambda i, j: (i, 0, j))],
        out_shape=[jax.ShapeDtypeStruct((b, n_tok, DIFF_QK), BF16),
                   jax.ShapeDtypeStruct((b, DIFF_HEADS, n_tok, LANES), BF16),
                   jax.ShapeDtypeStruct((b, DIFF_HEADS * (2 * DIFF_HD + V_ONES), n_tok), BF16)],
        compiler_params=_cparams(("arbitrary", "arbitrary")),
        name="diff_prep",
    )(u, tab, qng, kng)


def _attn_kernel(q_ref, k_ref, vt_ref, *rest, n_heads, dv, diff, lam_init, n_q_tiles):
    rest = list(rest)
    if n_q_tiles is None:
        rest.pop(0)
    if diff:
        lamp_ref, subg_ref = rest[:2]
        rest = rest[2:]
        lp = lamp_ref[...]
        lam = (jnp.exp(jnp.sum(lp[0:1] * lp[1:2], axis=-1, keepdims=True))
               - jnp.exp(jnp.sum(lp[2:3] * lp[3:4], axis=-1, keepdims=True)) + lam_init)
        lane = _iota((TILE, LANES), 1)
    o_ref, ot_ref, s_ref = rest[:3]
    dva = dv + V_ONES
    n_iter = n_heads if diff else n_heads // 2
    n_keys = k_ref.shape[2]
    blocks = [(0, TILE)] + [(b0, KEY_BLOCK) for b0 in range(TILE, n_keys, KEY_BLOCK)]

    def streams(i):
        if diff:
            q = q_ref[0, :, i * LANES:(i + 1) * LANES]
            zero = jnp.zeros_like(q)
            vr = slice(i * dva, (i + 1) * dva)
            return [(i, vr, jnp.where(lane < DIFF_HD, q, zero)), (i, vr, jnp.where(lane < DIFF_HD, zero, q))]
        return [(h, slice(h * dva, (h + 1) * dva), q_ref[0, :, h * LANES:(h + 1) * LANES]) for h in (2 * i, 2 * i + 1)]

    def v_rows(i):
        if diff:
            return [slice(i * dva, (i + 1) * dva)] * 2
        return [slice(h * dva, (h + 1) * dva) for h in (2 * i, 2 * i + 1)]

    def finish(i, accs):
        a0, a1 = accs
        if diff:
            o = a0[:dv] / a0[dv:dv + 1] - lam * (a1[:dv] / a1[dv:dv + 1])
            ms = jnp.mean(o * o, axis=0, keepdims=True)
            o = o * lax.rsqrt(ms + NORM_EPS) * subg_ref[...] * (1.0 - lam_init)
            ot_ref[i * dv:(i + 1) * dv, :] = o
        else:
            o = jnp.concatenate([a0[:dv] / a0[dv:dv + 1], a1[:dv] / a1[dv:dv + 1]], axis=0)
            ot_ref[2 * i * dv:2 * (i + 1) * dv, :] = o

    def phase(cur, cur_par, prev_rows, prev_par, prev_max):
        cur_max = [None, None]
        accs = [None, None]
        for start, size in blocks:
            rows = slice(start, start + size)
            if cur is not None:
                for st, (head, _, q) in enumerate(cur):
                    s = _dot_nt(k_ref[0, head, rows, :], q)
                    s_ref[cur_par, st, rows, :] = s
                    part = jnp.max(s, axis=0, keepdims=True)
                    cur_max[st] = part if cur_max[st] is None else jnp.maximum(cur_max[st], part)
            if prev_rows is not None:
                for st, vr in enumerate(prev_rows):
                    p = jnp.exp2(s_ref[prev_par, st, rows, :] - prev_max[st]).astype(BF16)
                    pv = _dot(vt_ref[0, vr, rows], p)
                    accs[st] = pv if accs[st] is None else accs[st] + pv
        return cur_max, accs

    def middle(first_max):
        prev_max = first_max
        for i in range(1, n_iter):
            cur_max, accs = phase(streams(i), i % 2, v_rows(i - 1), (i - 1) % 2, prev_max)
            finish(i - 1, accs)
            prev_max = cur_max
        return prev_max

    last_par = (n_iter - 1) % 2
    if n_q_tiles is None:
        last_max = middle(phase(streams(0), 0, None, 0, None)[0])
        finish(n_iter - 1, phase(None, 0, v_rows(n_iter - 1), last_par, last_max)[1])
        o_ref[0] = ot_ref[...].T.astype(o_ref.dtype)
        return

    mx_ref = rest[3]
    j = pl.program_id(1)

    @pl.when(j == 0)
    def _():
        s_ref[last_par] = jnp.zeros(s_ref.shape[1:], F32)
        mx_ref[...] = jnp.zeros_like(mx_ref)
        ot_ref[...] = jnp.zeros_like(ot_ref)

    first_max, accs = phase(streams(0), 0, v_rows(n_iter - 1), last_par, [mx_ref[0], mx_ref[1]])
    finish(n_iter - 1, accs)
    o_ref[0] = ot_ref[...].T.astype(o_ref.dtype)

    @pl.when(j < n_q_tiles)
    def _():
        last_max = middle(first_max)
        mx_ref[0] = last_max[0]
        mx_ref[1] = last_max[1]


def _attention(q, k, vt, *, n_heads, dv, need_ctx, diff=False, lam_p=None, sub_g=None, lam_init=0.0):
    b, n_tok, w = q.shape
    nq = n_tok // TILE - 1
    ow = n_heads * dv
    vrows = n_heads * (dv + V_ONES)
    n_iter = n_heads if diff else n_heads // 2
    assert n_iter % 2 == 0 and (n_tok - TILE) % KEY_BLOCK == 0
    extra_specs, extra_args = [], []
    if diff:
        extra_specs = [pl.BlockSpec(lam_p.shape, lambda i, j: (0, 0)), pl.BlockSpec(sub_g.shape, lambda i, j: (0, 0))]
        extra_args = [lam_p, sub_g]
    kern = functools.partial(_attn_kernel, n_heads=n_heads, dv=dv, diff=diff, lam_init=lam_init)
    name = "diff_attn" if diff else "mla_attn"
    y = pl.pallas_call(
        functools.partial(kern, n_q_tiles=nq),
        grid=(b, nq + 1),
        in_specs=[pl.BlockSpec((1, TILE, w), lambda i, j: (i, jnp.minimum(j, nq - 1) + 1, 0)),
                  pl.BlockSpec((1, n_heads, n_tok, LANES), lambda i, j: (i, 0, 0, 0)),
                  pl.BlockSpec((1, vrows, n_tok), lambda i, j: (i, 0, 0))] + extra_specs,
        out_specs=pl.BlockSpec((1, TILE, ow), lambda i, j: (i, jnp.maximum(j - 1, 0) + 1, 0)),
        out_shape=jax.ShapeDtypeStruct((b, n_tok, ow), BF16),
        scratch_shapes=[pltpu.VMEM((ow, TILE), F32), pltpu.VMEM((2, 2, n_tok, TILE), F32),
                        pltpu.VMEM((2, 1, TILE), F32)],
        compiler_params=_cparams(("arbitrary", "arbitrary")),
        name=name,
    )(q, k, vt, *extra_args)
    if not need_ctx:
        return y
    return pl.pallas_call(
        functools.partial(kern, n_q_tiles=None),
        grid=(b,),
        in_specs=[pl.BlockSpec((1, TILE, w), lambda i: (i, 0, 0)),
                  pl.BlockSpec((1, n_heads, TILE, LANES), lambda i: (i, 0, 0, 0)),
                  pl.BlockSpec((1, vrows, TILE), lambda i: (i, 0, 0)),
                  pl.BlockSpec(memory_space=pl.ANY)]
                 + [pl.BlockSpec(s.block_shape, lambda i: (0, 0)) for s in extra_specs],
        out_specs=pl.BlockSpec((1, TILE, ow), lambda i: (i, 0, 0)),
        out_shape=jax.ShapeDtypeStruct((b, n_tok, ow), BF16),
        input_output_aliases={3: 0},
        scratch_shapes=[pltpu.VMEM((ow, TILE), F32), pltpu.VMEM((2, 2, TILE, TILE), F32)],
        compiler_params=_cparams(("arbitrary",)),
        name=name + "_ctx",
    )(q, k, vt, y, *extra_args)


def _merge_kernel(*refs, n_src, first):
    (mod_ref, gof_ref, gob_ref, gz_ref, mla_ref, dif_ref, syf_ref, syb_ref, sz_ref,
     gates01_ref, gates23_ref, gng_ref, sng_ref, wb_ref, wo_ref, o_ref) = refs[n_src:]
    o = gof_ref[0] + gob_ref[0]
    z = gz_ref[0]
    ya = []
    for h in range(GDN_HEADS):
        hl = slice(h * GDN_DV, (h + 1) * GDN_DV)
        ya.append(_rms(o[:, hl], gng_ref[...]) * _silu(z[:, hl]))
    ya = jnp.concatenate(ya, axis=-1)
    y = (syf_ref[0] + syb_ref[0]) * _silu(sz_ref[0])
    gsz = SSM_INNER // SSM_GROUPS
    yd = jnp.concatenate([_rms(y[:, g * gsz:(g + 1) * gsz], sng_ref[:, g * gsz:(g + 1) * gsz])
                          for g in range(SSM_GROUPS)], axis=-1)
    ys = (ya, mla_ref[0], dif_ref[0], yd)
    m = None
    for i in range(N_BRANCH):
        gates_ref = gates01_ref if i < 2 else gates23_ref
        gate = _sigmoid(gates_ref[0, :, (i % 2) * D_MODEL:(i % 2 + 1) * D_MODEL].astype(F32))
        term = gate * _dot(ys[i].astype(BF16), wb_ref[i])
        m = term if m is None else m + term
    o_ref[0] = _stream_tile(refs[:n_src], first) + mod_ref[2:3, :] * _dot(m.astype(BF16), wo_ref[...])


def _merge(xs, mods, gdn_o, u, gates, y_mla, y_diff, ssd_y, gng, sng, wb, wo, *, need_ctx):
    first = 0 if need_ctx else 1
    src_specs, src_args, n_tok = _stream_specs(xs, first)
    b, d = mods.shape[0], mods.shape[-1]
    t = n_tok // TILE
    row = lambda w, cb=0: pl.BlockSpec((1, TILE, w), lambda i, j: (i, j + first, cb))
    const = lambda shape: pl.BlockSpec(shape, lambda i, j: (0,) * len(shape))
    aliases = {0: 0} if len(src_args) == 1 else {}
    assert need_ctx or len(src_args) == 1
    return pl.pallas_call(
        functools.partial(_merge_kernel, n_src=len(src_args), first=first),
        grid=(b, t - first),
        in_specs=src_specs + [
                  pl.BlockSpec((None, None, 6, d), lambda i, j: (i, jnp.minimum(j + first, 1), 0, 0)),
                  row(BRANCH_W), row(BRANCH_W), row(BRANCH_W, (C_GDN + 3 * BRANCH_W) // BRANCH_W),
                  row(BRANCH_W), row(BRANCH_W), row(BRANCH_W), row(BRANCH_W), row(BRANCH_W, C_SSMZ // BRANCH_W),
                  row(GATE_COLS // 2, 0), row(GATE_COLS // 2, 1),
                  const((1, GDN_DV)), const((1, SSM_INNER)), const((N_BRANCH, BRANCH_W, d)), const((d, d))],
        out_specs=row(d),
        out_shape=jax.ShapeDtypeStruct((b, n_tok, d), F32),
        input_output_aliases=aliases,
        compiler_params=_cparams(("arbitrary", "arbitrary")),
        name="merge",
    )(*src_args, mods, gdn_o[0], gdn_o[1], u, y_mla, y_diff, ssd_y[0], ssd_y[1], u,
      gates, gates, gng, sng, wb, wo)


def _mlp_kernel(x_ref, mod_ref, g_ref, w1_ref, w2_ref, o_ref):
    x = x_ref[0]
    h = _rms(x, g_ref[...]) * (1.0 + mod_ref[4:5, :]) + mod_ref[3:4, :]
    a = jnp.maximum(_dot(h.astype(BF16), w1_ref[...]), 0.0)
    o_ref[0] = x + mod_ref[5:6, :] * _dot((a * a).astype(BF16), w2_ref[...])


def _mlp(xs, mods, g, w1, w2, *, need_ctx):
    b, n_tok, d = xs.shape
    t = n_tok // TILE
    first = 0 if need_ctx else 1
    const = lambda shape: pl.BlockSpec(shape, lambda i, j: (0,) * len(shape))
    n_out = n_tok - first * TILE
    return pl.pallas_call(
        _mlp_kernel,
        grid=(b, t - first),
        in_specs=[pl.BlockSpec((1, TILE, d), lambda i, j: (i, j + first, 0)),
                  pl.BlockSpec((None, None, 6, d), lambda i, j: (i, jnp.minimum(j + first, 1), 0, 0)),
                  const((1, d)), const((d, D_FF)), const((D_FF, d))],
        out_specs=pl.BlockSpec((1, TILE, d), lambda i, j: (i, j, 0)),
        out_shape=jax.ShapeDtypeStruct((b, n_out, d), F32),
        compiler_params=_cparams(("arbitrary", "arbitrary")),
        name="mlp",
    )(xs, mods, g, w1, w2)


def _repack_w_in(w):
    d = w.shape[0]
    z = lambda n: jnp.zeros((d, n), w.dtype)
    o_mla = GDN_COLS
    o_diff = o_mla + MLA_COLS
    o_ssm = o_diff + DIFF_COLS
    o_gate = MIX_COLS
    gdn_ab = w[:, 2 * GDN_QK + 2 * GDN_VW:GDN_COLS]
    ssm_dt = w[:, o_ssm + 2 * SSM_INNER + 2 * SSM_BC:o_ssm + SSM_COLS]
    kpe = w[:, o_mla + MLA_Q_LORA + MLA_KV_LORA:o_mla + MLA_COLS]
    parts = [
        w[:, 0:2 * GDN_QK + 2 * GDN_VW],
        w[:, o_ssm + SSM_INNER:o_ssm + 2 * SSM_INNER + 2 * SSM_BC],
        w[:, o_diff:o_diff + DIFF_COLS],
        w[:, o_mla:o_mla + MLA_Q_LORA + MLA_KV_LORA], z(MLA_NOPE), kpe, z(LANES - MLA_QK),
        gdn_ab, ssm_dt, z(LANES - 4 * GDN_HEADS - 2 * SSM_HEADS),
        z(C_SSMZ - C_SMALL - LANES),
        w[:, o_ssm:o_ssm + SSM_INNER],
        w[:, o_gate:o_gate + GATE_COLS],
    ]
    out = jnp.concatenate(parts, axis=1)
    assert out.shape[1] == N_IN
    return out.astype(BF16)


def _lane_row(vals, offset):
    row = jnp.zeros((1, LANES), F32)
    return row.at[0, offset:offset + vals.shape[0]].set(vals.astype(F32))


def _rope_tables(n_lat, rot_dim, lane_offset, period):
    rows = n_lat // GRID_W
    row = jnp.repeat(jnp.arange(rows, dtype=F32), GRID_W)
    col = jnp.tile(jnp.arange(GRID_W, dtype=F32), rows)
    quarter = rot_dim // 4
    inv = ROPE_THETA ** (-jnp.arange(quarter, dtype=F32) / quarter)
    ar = row[:, None] * inv
    ac = col[:, None] * inv
    ang = jnp.concatenate([ar, ar, ac, ac], axis=-1)
    cos, sin = jnp.cos(ang), jnp.sin(ang)
    first = (np.arange(rot_dim) % (2 * quarter)) < quarter
    sin_a = jnp.where(first, -sin, 0.0)
    sin_b = jnp.where(first, 0.0, sin)
    reps = LANES // period
    def place(t, fill):
        blk = jnp.full((n_lat, period), fill, F32).at[:, lane_offset:lane_offset + rot_dim].set(t)
        blk = jnp.tile(blk, (1, reps))
        ctx = jnp.full((CTX_LEN, LANES), fill, F32)
        return jnp.concatenate([ctx, blk], axis=0)
    return jnp.stack([place(cos, 1.0), place(sin_a, 0.0), place(sin_b, 0.0)])


def kernel(x, c, ctx, c_ctx, ada_w, ada_b, norm1_g, norm2_g, w_in, gdn_conv, gdn_a_log, gdn_dt_bias, gdn_norm_g,
           mla_q_lora_g, mla_kv_lora_g, mla_w_uq, mla_w_ukv, mla_qn_g, mla_kn_g, diff_qn_g, diff_kn_g,
           diff_lambda, diff_sub_g, ssm_conv, ssm_conv_b, ssm_a_log, ssm_dt_bias, ssm_d, ssm_norm_g, w_branch,
           w_out, mlp_w1, mlp_w2):
    b, n_lat, d = x.shape
    depth = w_in.shape[0]
    assert ctx.shape[1] == CTX_LEN == TILE and n_lat % TILE == 0 and d == D_MODEL
    xs = (ctx, x) if depth > 1 else jnp.concatenate([ctx, x], axis=1)
    tab_mla = _rope_tables(n_lat, MLA_ROPE, MLA_NOPE, LANES)
    tab_diff = _rope_tables(n_lat, DIFF_HD, 0, DIFF_HD)
    rows = b + 8
    cc = jnp.zeros((rows, d), F32).at[:b].set(c).at[b].set(c_ctx)
    row2 = lambda v: v.reshape(1, -1).astype(F32)

    for l in range(depth):
        need_ctx = l < depth - 1
        lam_init = LAMBDA_BASE - LAMBDA_AMP * math.exp(-LAMBDA_RATE * l)
        mod = _ada(cc, ada_w[l], ada_b[l].reshape(1, -1))
        mod_lat = mod[:b].reshape(b, 6, d)
        mod_ctx = jnp.broadcast_to(mod[b].reshape(1, 6, d), (b, 6, d))
        mods = jnp.stack([mod_ctx, mod_lat], axis=1)

        u, gates = _inproj(xs, mods, row2(norm1_g[l]), _repack_w_in(w_in[l]))

        gdn_o = _gdn_scan(_gdn_prep(u, gdn_conv[l].astype(F32),
                                    _lane_row(gdn_a_log[l].reshape(-1), SM_A),
                                    _lane_row(gdn_dt_bias[l].reshape(-1), SM_A)))

        wq = mla_w_uq[l].reshape(MLA_Q_LORA, MLA_HEADS, MLA_QK)
        wq = jnp.pad(wq, ((0, 0), (0, 0), (0, LANES - MLA_QK))).reshape(MLA_Q_LORA, MLA_HEADS * LANES)
        wkv = mla_w_ukv[l].reshape(MLA_KV_LORA, MLA_HEADS, MLA_NOPE + MLA_V)
        wk = jnp.pad(wkv[:, :, :MLA_NOPE], ((0, 0), (0, 0), (0, LANES - MLA_NOPE))).reshape(MLA_KV_LORA, MLA_HEADS * LANES)
        wvt = wkv[:, :, MLA_NOPE:].reshape(MLA_KV_LORA, MLA_HEADS * MLA_V).T
        pad_g = lambda g: jnp.pad(g.astype(F32), (0, LANES - MLA_QK)).reshape(1, LANES)
        q_m, k_m, vt_m = _mla_prep(u, tab_mla, row2(mla_q_lora_g[l]), row2(mla_kv_lora_g[l]),
                                   wq.astype(BF16), wk.astype(BF16), wvt.astype(BF16),
                                   pad_g(mla_qn_g[l]), pad_g(mla_kn_g[l]))
        y_mla = _attention(q_m, k_m, vt_m, n_heads=MLA_HEADS, dv=MLA_V, need_ctx=need_ctx)

        rep_g = lambda g: jnp.tile(g.astype(F32), LANES // DIFF_HD).reshape(1, LANES)
        q_d, k_d, vt_d = _diff_prep(u, tab_diff, rep_g(diff_qn_g[l]), rep_g(diff_kn_g[l]))
        y_diff = _attention(q_d, k_d, vt_d, n_heads=DIFF_HEADS, dv=2 * DIFF_HD, need_ctx=need_ctx, diff=True,
                            lam_p=diff_lambda[l].astype(F32), sub_g=diff_sub_g[l].astype(F32).reshape(-1, 1),
                            lam_init=lam_init)

        ssd_y = _ssd(u, ssm_conv[l].astype(F32), row2(ssm_conv_b[l]),
                     _lane_row(ssm_a_log[l].reshape(-1), SM_DT), _lane_row(ssm_dt_bias[l].reshape(-1), SM_DT),
                     jnp.repeat(ssm_d[l].astype(F32), SSM_HEAD_DIM).reshape(1, -1))

        xs = _merge(xs, mods, gdn_o, u, gates, y_mla, y_diff, ssd_y, row2(gdn_norm_g[l]), row2(ssm_norm_g[l]),
                    w_branch[l].astype(BF16), w_out[l].astype(BF16), need_ctx=need_ctx)
        xs = _mlp(xs, mods, row2(norm2_g[l]), mlp_w1[l].astype(BF16), mlp_w2[l].astype(BF16), need_ctx=need_ctx)
    return xs
```

```python
import functools
import math

import jax
import jax.numpy as jnp
import numpy as np
from jax import lax
from jax.experimental import pallas as pl
from jax.experimental.pallas import tpu as pltpu

F32 = jnp.float32
BF16 = jnp.bfloat16

D_MODEL = 1024
CTX_LEN = 256
GRID_W = 64
ROPE_THETA = 10000.0
NORM_EPS = 1e-6
CONV_K = 5
N_BRANCH = 4
D_FF = 4 * D_MODEL
GDN_HEADS = 4
GDN_DK = 128
GDN_DV = 128
MLA_HEADS = 8
MLA_NOPE = 64
MLA_ROPE = 32
MLA_V = 64
MLA_QK = MLA_NOPE + MLA_ROPE
MLA_Q_LORA = 384
MLA_KV_LORA = 256
DIFF_HEADS = 4
DIFF_HD = 64
LAMBDA_BASE = 0.8
LAMBDA_AMP = 0.6
LAMBDA_RATE = 0.3
SSM_HEADS = 8
SSM_HEAD_DIM = 64
SSM_GROUPS = 2
SSM_STATE = 128
GDN_QK = GDN_HEADS * GDN_DK
GDN_VW = GDN_HEADS * GDN_DV
DIFF_QK = DIFF_HEADS * 2 * DIFF_HD
DIFF_VW = DIFF_HEADS * 2 * DIFF_HD
SSM_INNER = SSM_HEADS * SSM_HEAD_DIM
SSM_BC = SSM_GROUPS * SSM_STATE
BRANCH_W = 512
GDN_COLS = 2 * GDN_QK + 2 * GDN_VW + 4 * GDN_HEADS
MLA_COLS = MLA_Q_LORA + MLA_KV_LORA + MLA_ROPE
DIFF_COLS = 2 * DIFF_QK + DIFF_VW
SSM_COLS = 2 * SSM_INNER + 2 * SSM_BC + 2 * SSM_HEADS
MIX_COLS = GDN_COLS + MLA_COLS + DIFF_COLS + SSM_COLS
GATE_COLS = N_BRANCH * D_MODEL

LANES = 128
TILE = 256
CHUNK = 64
CPT = TILE // CHUNK
HALO = 8
VMEM_LIMIT = 56 * 1024 * 1024
V_ONES = 16
KEY_BLOCK = 256
LOG2E = 1.4426950408889634

C_GDN = 0
C_XBC = 2048
C_DIFF = 3072
C_MLA = 4608
C_SMALL = 5376
C_SSMZ = 5632
C_GATE = 6144
N_IN = 10240
IN_TN = 2048

SM_A = 0
SM_B = 8
SM_DT = 16


def _cparams(sem):
    return pltpu.CompilerParams(dimension_semantics=sem, vmem_limit_bytes=VMEM_LIMIT)


def _dot(a, b):
    return jnp.dot(a, b, preferred_element_type=F32)


def _dot_nt(a, b):
    return lax.dot_general(a, b, (((1,), (1,)), ((), ())), preferred_element_type=F32)


def _dot_tn(a, b):
    return lax.dot_general(a, b, (((0,), (0,)), ((), ())), preferred_element_type=F32)


def _split3(x):
    x1 = x.astype(BF16)
    r1 = x - x1.astype(F32)
    x2 = r1.astype(BF16)
    r2 = r1 - x2.astype(F32)
    return x1, x2, r2.astype(BF16)


def _dot_exact_lhs(a_bf, x):
    x1, x2, x3 = _split3(x)
    return _dot(a_bf, x1) + _dot(a_bf, x2) + _dot(a_bf, x3)


def _dot_exact_rhs(x, b_bf):
    x1, x2, x3 = _split3(x)
    return _dot(x1, b_bf) + _dot(x2, b_bf) + _dot(x3, b_bf)


def _sigmoid(x):
    return 1.0 / (1.0 + jnp.exp(-x))


def _silu(x):
    return x * _sigmoid(x)


def _softplus(x):
    return jnp.maximum(x, 0.0) + jnp.log1p(jnp.exp(-jnp.abs(x)))


def _iota(shape, dim):
    return lax.broadcasted_iota(jnp.int32, shape, dim)


def _chunk_tri(n, upper):
    r = _iota((n, n), 0)
    c = _iota((n, n), 1)
    same = (r // CHUNK) == (c // CHUNK)
    tri = (r <= c) if upper else (r >= c)
    return jnp.where(same & tri, 1.0, 0.0).astype(BF16)


def _expander(n_rows, first_row, n_groups, width):
    r = _iota((n_rows, n_groups * width), 0)
    c = _iota((n_rows, n_groups * width), 1)
    return jnp.where(r == first_row + c // width, 1.0, 0.0).astype(BF16)


def _ada_kernel(c_ref, w_ref, b_ref, o_ref):
    a = _silu(c_ref[...])
    a1, a2, a3 = _split3(a)
    w1, w2, w3 = _split3(w_ref[...])
    acc = _dot(a1, w1) + (_dot(a1, w2) + _dot(a2, w1)) + (_dot(a1, w3) + _dot(a2, w2) + _dot(a3, w1))
    o_ref[...] = acc + b_ref[...]


def _ada(cc, w, b):
    rows, d = cc.shape
    n = w.shape[1]
    tn = 1536
    return pl.pallas_call(
        _ada_kernel,
        grid=(n // tn,),
        in_specs=[pl.BlockSpec((rows, d), lambda j: (0, 0)),
                  pl.BlockSpec((d, tn), lambda j: (0, j)),
                  pl.BlockSpec((1, tn), lambda j: (0, j))],
        out_specs=pl.BlockSpec((rows, tn), lambda j: (0, j)),
        out_shape=jax.ShapeDtypeStruct((rows, n), F32),
        compiler_params=_cparams(("arbitrary",)),
        name="ada",
    )(cc, w, b)


def _rms(x, g):
    ms = jnp.mean(x * x, axis=-1, keepdims=True)
    return x * lax.rsqrt(ms + NORM_EPS) * g


def _stream_specs(xs, first=0):
    if isinstance(xs, tuple):
        ctx, x = xs
        d = x.shape[-1]
        specs = [pl.BlockSpec((1, TILE, d), lambda i, j: (i, 0, 0)),
                 pl.BlockSpec((1, TILE, d), lambda i, j: (i, jnp.maximum(j + first - 1, 0), 0))]
        return specs, [ctx, x], ctx.shape[1] + x.shape[1]
    d = xs.shape[-1]
    return [pl.BlockSpec((1, TILE, d), lambda i, j: (i, j + first, 0))], [xs], xs.shape[1]


def _stream_tile(refs, first=0):
    if len(refs) == 2:
        return jnp.where(pl.program_id(1) + first == 0, refs[0][0], refs[1][0])
    return refs[0][0]


def _inproj_kernel(*refs, n_src):
    mod_ref, g_ref, w_ref, o_ref, gate_ref = refs[n_src:]
    h = _rms(_stream_tile(refs[:n_src]), g_ref[...])
    hb = (h * (1.0 + mod_ref[1:2, :]) + mod_ref[0:1, :]).astype(BF16)
    for c in range(0, C_GATE, IN_TN):
        o_ref[0, :, c:c + IN_TN] = _dot(hb, w_ref[:, c:c + IN_TN])
    for c in range(0, GATE_COLS, IN_TN):
        gate_ref[0, :, c:c + IN_TN] = _dot(hb, w_ref[:, C_GATE + c:C_GATE + c + IN_TN]).astype(BF16)


def _inproj(xs, mods, g, w):
    src_specs, src_args, n_tok = _stream_specs(xs)
    b, d = mods.shape[0], mods.shape[-1]
    t = n_tok // TILE
    return pl.pallas_call(
        functools.partial(_inproj_kernel, n_src=len(src_args)),
        grid=(b, t),
        in_specs=src_specs + [
                  pl.BlockSpec((None, None, 6, d), lambda i, j: (i, jnp.minimum(j, 1), 0, 0)),
                  pl.BlockSpec((1, d), lambda i, j: (0, 0)),
                  pl.BlockSpec((d, N_IN), lambda i, j: (0, 0), pipeline_mode=pl.Buffered(1))],
        out_specs=[pl.BlockSpec((1, TILE, C_GATE), lambda i, j: (i, j, 0)),
                   pl.BlockSpec((1, TILE, GATE_COLS), lambda i, j: (i, j, 0))],
        out_shape=[jax.ShapeDtypeStruct((b, n_tok, C_GATE), F32),
                   jax.ShapeDtypeStruct((b, n_tok, GATE_COLS), BF16)],
        compiler_params=_cparams(("arbitrary", "arbitrary")),
        name="inproj",
    )(*src_args, mods, g, w)


def _halo_specs(width, col_block, n_tiles, tile_of):
    rpt = TILE // HALO
    last = n_tiles * rpt - 1
    main = pl.BlockSpec((1, TILE, width), lambda *ids: (ids[0], tile_of(*ids), col_block))
    prev = pl.BlockSpec((1, HALO, width),
                        lambda *ids: (ids[0], jnp.maximum(tile_of(*ids) * rpt - 1, 0), col_block))
    nxt = pl.BlockSpec((1, HALO, width),
                       lambda *ids: (ids[0], jnp.minimum(tile_of(*ids) * rpt + rpt, last), col_block))
    return main, prev, nxt


def _conv_tile(main_ref, prev_ref, next_ref, w_ref, ext_ref, tile, n_tiles):
    prev_ok = (tile >= 2).astype(F32)
    next_ok = jnp.logical_and(tile >= 1, tile < n_tiles - 1).astype(F32)
    ext_ref[0:HALO, :] = prev_ref[0] * prev_ok
    ext_ref[HALO:HALO + TILE, :] = main_ref[0]
    ext_ref[HALO + TILE:2 * HALO + TILE, :] = next_ref[0] * next_ok
    half = CONV_K // 2
    acc = None
    for k in range(CONV_K):
        term = w_ref[k:k + 1, :] * ext_ref[HALO - half + k:HALO - half + k + TILE, :]
        acc = term if acc is None else acc + term
    return acc


def _block_inverse_many(mats):
    n = mats[0].shape[0]
    ri = _iota((n, n), 0)
    ci = _iota((n, n), 1)
    eye = jnp.where(ri == ci, 1.0, 0.0).astype(F32)
    pair = (ri // 2) == (ci // 2)
    ts = [eye - jnp.where(pair, a, 0.0) for a in mats]
    abs_ = [a.astype(BF16) for a in mats]
    zero = jnp.zeros((n, n), BF16)
    size = 2
    while size < CHUNK:
        off = ((ri // (2 * size)) == (ci // (2 * size))) & ((ri // size) != (ci // size))
        tbs = [t.astype(BF16) for t in ts]
        xs = [_dot(jnp.where(off, ab, zero), tb).astype(BF16) for ab, tb in zip(abs_, tbs)]
        ts = [t - _dot(tb, x) for t, tb, x in zip(ts, tbs, xs)]
        size *= 2
    return ts


def _gdn_prep_kernel(main_ref, prev_ref, next_ref, sm_ref, cw_ref, alog_ref, dtb_ref,
                     u0_ref, wq0_ref, kd0_ref, at0_ref, gl0_ref,
                     u1_ref, wq1_ref, kd1_ref, at1_ref, gl1_ref,
                     ext_ref, *, n_tiles):
    tile = pl.program_id(1)
    qkv = _silu(_conv_tile(main_ref, prev_ref, next_ref, cw_ref, ext_ref, tile, n_tiles))
    sm = sm_ref[0]
    g_all = -jnp.exp(alog_ref[...]) * _softplus(sm + dtb_ref[...])
    beta_all = _sigmoid(sm)
    nh = GDN_HEADS
    exp_b = _expander(LANES, SM_B, 2 * nh, LANES)
    beta_x = _dot_exact_rhs(beta_all, exp_b)
    outs = ((u0_ref, wq0_ref, kd0_ref, at0_ref, gl0_ref), (u1_ref, wq1_ref, kd1_ref, at1_ref, gl1_ref))
    ri = _iota((TILE, TILE), 0)
    ci = _iota((TILE, TILE), 1)
    same = (ri // CHUNK) == (ci // CHUNK)
    cums, masks = [], []
    for d in range(2):
        cum = _dot_exact_lhs(_chunk_tri(TILE, upper=(d == 1)), g_all)
        cums.append((_dot_exact_rhs(cum, _expander(LANES, SM_A + d * nh, nh, LANES)), cum.T))
        masks.append((same & ((ri >= ci) if d == 0 else (ri <= ci)), same & ((ri > ci) if d == 0 else (ri < ci))))

    chains, a_mats = [], []
    for h in range(nh):
        q = qkv[:, h * GDN_DK:(h + 1) * GDN_DK]
        k = qkv[:, GDN_QK + h * GDN_DK:GDN_QK + (h + 1) * GDN_DK]
        v = qkv[:, 2 * GDN_QK + h * GDN_DV:2 * GDN_QK + (h + 1) * GDN_DV]
        q = q * lax.rsqrt(jnp.sum(q * q, axis=-1, keepdims=True) + 1e-6) * (GDN_DK ** -0.5)
        k = k * lax.rsqrt(jnp.sum(k * k, axis=-1, keepdims=True) + 1e-6)
        kb = k.astype(BF16)
        kq = _dot_nt(jnp.concatenate([kb, q.astype(BF16)], axis=0), kb)
        for d in range(2):
            col = d * nh + h
            lanes = slice(col * LANES, (col + 1) * LANES)
            cum_x, cum_t = cums[d]
            incl, strict = masks[d]
            g_col = cum_x[:, h * LANES:(h + 1) * LANES]
            b_col = beta_x[:, lanes]
            diff = jnp.concatenate([g_col, g_col], axis=1) - cum_t[SM_A + col:SM_A + col + 1, :]
            decay = jnp.where(incl, jnp.exp(jnp.where(incl, diff, 0.0)), 0.0)
            a_mats.append(jnp.where(strict, kq[:TILE] * jnp.concatenate([b_col, b_col], axis=1) * decay, 0.0))
            e_g = jnp.exp(g_col)
            rhs = jnp.concatenate([v * b_col, k * b_col * e_g], axis=1).astype(BF16)
            chains.append((h, d, q, k, g_col, e_g, rhs, kq[TILE:] * decay))

    t_mats = _block_inverse_many(a_mats)
    for (h, d, q, k, g_col, e_g, rhs, attn), t_mat in zip(chains, t_mats):
        uw = _dot(t_mat.astype(BF16), rhs)
        qg = (q * e_g).astype(BF16)
        u_ref, wq_ref, kd_ref, at_ref, gl_ref = outs[d]
        hl = slice(h * LANES, (h + 1) * LANES)
        for c in range(CPT):
            rows = slice(c * CHUNK, (c + 1) * CHUNK)
            last = c * CHUNK + (CHUNK - 1 if d == 0 else 0)
            g_last = g_col[last:last + 1, :]
            slot = c if d == 0 else CPT - 1 - c
            u_ref[0, 0, slot, :, hl] = uw[rows, :GDN_DV]
            wq_ref[0, 0, slot, 0:CHUNK, hl] = uw[rows, GDN_DV:].astype(BF16)
            wq_ref[0, 0, slot, CHUNK:2 * CHUNK, hl] = qg[rows]
            kd_ref[0, 0, slot, :, hl] = (k[rows] * jnp.exp(g_last - g_col[rows])).astype(BF16)
            at_ref[0, 0, slot, :, h * CHUNK:(h + 1) * CHUNK] = attn[rows, rows].astype(BF16)
            gl_ref[0, 0, slot, :, hl] = jnp.exp(g_last)


def _bwd_block(j, n_tiles):
    return jnp.where(j == 0, 0, n_tiles - j)


def _gdn_prep(u, conv_w, alog_row, dtb_row):
    b, n_tok, _ = u.shape
    t = n_tok // TILE
    w = GDN_VW
    cw = 2 * GDN_QK + GDN_VW
    main, prev, nxt = _halo_specs(cw, 0, t, lambda i, j: j)
    fwd = lambda i, j: (i, j, 0, 0, 0)
    bwd = lambda i, j: (i, _bwd_block(j, t), 0, 0, 0)

    def outs(imap):
        return [pl.BlockSpec((1, 1, CPT, CHUNK, w), imap),
                pl.BlockSpec((1, 1, CPT, 2 * CHUNK, w), imap),
                pl.BlockSpec((1, 1, CPT, CHUNK, w), imap),
                pl.BlockSpec((1, 1, CPT, CHUNK, GDN_HEADS * CHUNK), imap),
                pl.BlockSpec((1, 1, CPT, 1, w), imap)]

    shapes = [jax.ShapeDtypeStruct((b, t, CPT, CHUNK, w), F32),
              jax.ShapeDtypeStruct((b, t, CPT, 2 * CHUNK, w), BF16),
              jax.ShapeDtypeStruct((b, t, CPT, CHUNK, w), BF16),
              jax.ShapeDtypeStruct((b, t, CPT, CHUNK, GDN_HEADS * CHUNK), BF16),
              jax.ShapeDtypeStruct((b, t, CPT, 1, w), F32)]
    return pl.pallas_call(
        functools.partial(_gdn_prep_kernel, n_tiles=t),
        grid=(b, t),
        in_specs=[main, prev, nxt,
                  pl.BlockSpec((1, TILE, LANES), lambda i, j: (i, j, C_SMALL // LANES)),
                  pl.BlockSpec((CONV_K, cw), lambda i, j: (0, 0)),
                  pl.BlockSpec((1, LANES), lambda i, j: (0, 0)),
                  pl.BlockSpec((1, LANES), lambda i, j: (0, 0))],
        out_specs=outs(fwd) + outs(bwd),
        out_shape=shapes + shapes,
        scratch_shapes=[pltpu.VMEM((TILE + 2 * HALO, cw), F32)],
        compiler_params=_cparams(("arbitrary", "arbitrary")),
        name="gdn_prep",
    )(u, u, u, u, conv_w, alog_row, dtb_row)


def _gdn_scan_kernel(u0_ref, wq0_ref, kd0_ref, at0_ref, gl0_ref,
                     u1_ref, wq1_ref, kd1_ref, at1_ref, gl1_ref,
                     of_ref, ob_ref, s_ref):
    @pl.when(pl.program_id(1) == 0)
    def _():
        s_ref[...] = jnp.zeros_like(s_ref)

    ins = ((u0_ref, wq0_ref, kd0_ref, at0_ref, gl0_ref), (u1_ref, wq1_ref, kd1_ref, at1_ref, gl1_ref))
    chains = [(d, h) for d in range(2) for h in range(GDN_HEADS)]
    hls = [slice(h * LANES, (h + 1) * LANES) for _, h in chains]
    states = [s_ref[d, h] for d, h in chains]
    for slot in range(CPT):
        rs = [_dot(ins[d][1][0, 0, slot, :, hl], s.astype(BF16)) for (d, _), hl, s in zip(chains, hls, states)]
        vbs = [(ins[d][0][0, 0, slot, :, hl] - r[0:CHUNK]).astype(BF16) for (d, _), hl, r in zip(chains, hls, rs)]
        outs = [r[CHUNK:2 * CHUNK] + _dot(ins[d][3][0, 0, slot, :, h * CHUNK:(h + 1) * CHUNK], vb)
                for (d, h), r, vb in zip(chains, rs, vbs)]
        states = [s * ins[d][4][0, 0, slot, :, hl] + _dot_tn(ins[d][2][0, 0, slot, :, hl], vb)
                  for (d, _), hl, s, vb in zip(chains, hls, states, vbs)]
        for (d, _), hl, o in zip(chains, hls, outs):
            c = slot if d == 0 else CPT - 1 - slot
            (of_ref if d == 0 else ob_ref)[0, c * CHUNK:(c + 1) * CHUNK, hl] = o
    for (d, h), s in zip(chains, states):
        s_ref[d, h] = s


def _gdn_scan(prep):
    b, t = prep[0].shape[:2]
    w = GDN_VW
    imap = lambda i, j: (i, j, 0, 0, 0)
    specs = [pl.BlockSpec((1, 1, CPT, CHUNK, w), imap),
             pl.BlockSpec((1, 1, CPT, 2 * CHUNK, w), imap),
             pl.BlockSpec((1, 1, CPT, CHUNK, w), imap),
             pl.BlockSpec((1, 1, CPT, CHUNK, GDN_HEADS * CHUNK), imap),
             pl.BlockSpec((1, 1, CPT, 1, w), imap)]
    return pl.pallas_call(
        _gdn_scan_kernel,
        grid=(b, t),
        in_specs=specs + specs,
        out_specs=[pl.BlockSpec((1, TILE, w), lambda i, j: (i, j, 0)),
                   pl.BlockSpec((1, TILE, w), lambda i, j: (i, _bwd_block(j, t), 0))],
        out_shape=[jax.ShapeDtypeStruct((b, t * TILE, w), F32)] * 2,
        scratch_shapes=[pltpu.VMEM((2, GDN_HEADS, GDN_DK, GDN_DV), F32)],
        compiler_params=_cparams(("arbitrary", "arbitrary")),
        name="gdn_scan",
    )(*prep)


def _ssd_direction(d, xbc, sm, y_ref, st_ref, alog_ref, dtb_ref, dskip_ref):
    nh, hd, ng = SSM_HEADS, SSM_HEAD_DIM, SSM_GROUPS
    hpg = nh // ng
    gw = hpg * hd
    xs = xbc[:, :SSM_INNER]
    dt_all = _softplus(sm + dtb_ref[...])
    da_all = dt_all * (-jnp.exp(alog_ref[...]))
    cum = _dot_exact_lhs(_chunk_tri(TILE, upper=(d == 1)), da_all)
    cum_t = cum.T
    expand = _expander(LANES, SM_DT + d * nh, nh, hd)
    dt_x = _dot_exact_rhs(dt_all, expand)
    cum_x = _dot_exact_rhs(cum, expand)
    xdt = xs * dt_x
    ri = _iota((CHUNK, gw), 0)
    ci = _iota((CHUNK, gw), 1) % CHUNK
    incl = (ri >= ci) if d == 0 else (ri <= ci)
    head_diag = (_iota((gw, gw), 0) // hd) == (_iota((gw, gw), 1) // hd)
    zero_bf = jnp.zeros((gw, gw), BF16)
    for step in range(CPT):
        c = step if d == 0 else CPT - 1 - step
        rows = slice(c * CHUNK, (c + 1) * CHUNK)
        last = c * CHUNK + (CHUNK - 1 if d == 0 else 0)
        cum_c = cum_x[rows]
        cum_last = cum_x[last:last + 1]
        decay_states = jnp.exp(cum_last - cum_c)
        in_decay = jnp.exp(cum_c)
        chunk_decay = jnp.exp(cum_last)
        for g in range(ng):
            gl = slice(g * gw, (g + 1) * gw)
            bm = xbc[rows, SSM_INNER + g * SSM_STATE:SSM_INNER + (g + 1) * SSM_STATE].astype(BF16)
            cm = xbc[rows, SSM_INNER + SSM_BC + g * SSM_STATE:SSM_INNER + SSM_BC + (g + 1) * SSM_STATE].astype(BF16)
            state = st_ref[d, g]
            y_off = _dot(cm, state.astype(BF16)) * in_decay[:, gl]
            xdt_c = xdt[rows, gl]
            st_ref[d, g] = state * chunk_decay[:, gl] + _dot_tn(bm, (xdt_c * decay_states[:, gl]).astype(BF16))
            scores = _dot_nt(cm, jnp.concatenate([bm] * hpg, axis=0))
            col0 = SM_DT + d * nh + g * hpg
            row_terms = jnp.concatenate([cum_t[col0 + hh:col0 + hh + 1, rows] for hh in range(hpg)], axis=1)
            seg = cum_c[:, gl] - row_terms
            lmat = jnp.where(incl, jnp.exp(jnp.where(incl, seg, 0.0)), 0.0)
            x_bd = jnp.where(head_diag, jnp.concatenate([xdt_c.astype(BF16)] * hpg, axis=0), zero_bf)
            y = _dot((scores * lmat).astype(BF16), x_bd) + y_off
            if d == 0:
                y = y + dskip_ref[:, gl] * xs[rows, gl]
            y_ref[0, rows, gl] = y


def _ssd_kernel(m_ref, p_ref, n_ref, sm_ref, cw_ref, cb_ref, alog_ref, dtb_ref, dskip_ref,
                yf_ref, yb_ref, ext_ref, st_ref, cache_ref, *, n_tiles):
    s = pl.program_id(1)

    @pl.when(s == 0)
    def _():
        st_ref[...] = jnp.zeros_like(st_ref)

    @pl.when(s < n_tiles)
    def _():
        xbc = _silu(_conv_tile(m_ref, p_ref, n_ref, cw_ref, ext_ref, s, n_tiles) + cb_ref[...])
        cache_ref[s] = xbc
        _ssd_direction(0, xbc, sm_ref[0], yf_ref, st_ref, alog_ref, dtb_ref, dskip_ref)

    @pl.when(s >= n_tiles)
    def _():
        tile = _bwd_block(s - n_tiles, n_tiles)
        _ssd_direction(1, cache_ref[tile], sm_ref[0], yb_ref, st_ref, alog_ref, dtb_ref, dskip_ref)


def _ssd(u, conv_w, conv_b, alog_row, dtb_row, dskip_row):
    b, n_tok, _ = u.shape
    t = n_tok // TILE
    cw = SSM_INNER + 2 * SSM_BC
    main, prev, nxt = _halo_specs(cw, C_XBC // cw, t, lambda i, s: jnp.minimum(s, t - 1))
    tile_of = lambda s: jnp.where(s < t, s, _bwd_block(s - t, t))
    const = lambda r, c: pl.BlockSpec((r, c), lambda i, s: (0, 0))
    return pl.pallas_call(
        functools.partial(_ssd_kernel, n_tiles=t),
        grid=(b, 2 * t),
        in_specs=[main, prev, nxt,
                  pl.BlockSpec((1, TILE, LANES), lambda i, s: (i, tile_of(s), C_SMALL // LANES)),
                  const(CONV_K, cw), const(1, cw), const(1, LANES), const(1, LANES), const(1, SSM_INNER)],
        out_specs=[pl.BlockSpec((1, TILE, SSM_INNER), lambda i, s: (i, jnp.minimum(s, t - 1), 0)),
                   pl.BlockSpec((1, TILE, SSM_INNER), lambda i, s: (i, jnp.where(s < t, 0, _bwd_block(s - t, t)), 0))],
        out_shape=[jax.ShapeDtypeStruct((b, n_tok, SSM_INNER), F32)] * 2,
        scratch_shapes=[pltpu.VMEM((TILE + 2 * HALO, cw), F32),
                        pltpu.VMEM((2, SSM_GROUPS, SSM_STATE, SSM_INNER // SSM_GROUPS), F32),
                        pltpu.VMEM((t, TILE, cw), F32)],
        compiler_params=_cparams(("arbitrary", "arbitrary")),
        name="ssd",
    )(u, u, u, u, conv_w, conv_b, alog_row, dtb_row, dskip_row)


PAIR = 2 * LANES


def _group_ones(group):
    r = _iota((PAIR, PAIR), 0)
    c = _iota((PAIR, PAIR), 1)
    return jnp.where((r // group) == (c // group), 1.0, 0.0).astype(BF16)


def _rope_perm(period, lane_offset, rot_dim):
    r = _iota((PAIR, PAIR), 0)
    c = _iota((PAIR, PAIR), 1)
    q = rot_dim // 4
    cl = c % period - lane_offset
    src = jnp.where((cl % (2 * q)) < q, c + q, c - q)
    return jnp.where((cl >= 0) & (cl < rot_dim) & (r == src), 1.0, 0.0).astype(BF16)


def _norm_rope(x, gain, ones_bd, perm, tab_ref, n_real):
    ss = _dot((x * x).astype(BF16), ones_bd)
    xn = x * lax.rsqrt(ss / n_real + NORM_EPS) * gain
    hi = xn.astype(BF16)
    lo = (xn - hi.astype(F32)).astype(BF16)
    rot = _dot(hi, perm) + _dot(lo, perm)
    cos = jnp.concatenate([tab_ref[0]] * 2, axis=1)
    sin = jnp.concatenate([tab_ref[1] + tab_ref[2]] * 2, axis=1)
    return xn * cos + rot * sin


def _store_vt(vt_ref, vt, n_heads, dv, n_ones):
    for h in range(n_heads):
        base = h * (dv + n_ones)
        vt_ref[0, base:base + dv, :] = vt[h * dv:(h + 1) * dv].astype(BF16)
        if n_ones:
            vt_ref[0, base + dv:base + dv + n_ones, :] = jnp.ones((n_ones, vt.shape[1]), BF16)


def _mla_prep_kernel(u_ref, tab_ref, qlg_ref, kvlg_ref, wq_ref, wk_ref, wvt_ref, qng_ref, kng_ref,
                     q_ref, k_ref, vt_ref):
    u = u_ref[0]
    cq = _rms(u[:, :MLA_Q_LORA], qlg_ref[...]).astype(BF16)
    ckv = _rms(u[:, MLA_Q_LORA:MLA_Q_LORA + MLA_KV_LORA], kvlg_ref[...]).astype(BF16)
    kpe = u[:, MLA_Q_LORA + MLA_KV_LORA:]
    q_all = _dot(cq, wq_ref[...])
    k_all = _dot(ckv, wk_ref[...])
    _store_vt(vt_ref, _dot_nt(wvt_ref[...], ckv), MLA_HEADS, MLA_V, V_ONES)
    scale = MLA_QK ** -0.5 * LOG2E
    ones_bd = _group_ones(LANES)
    perm = _rope_perm(LANES, MLA_NOPE, MLA_ROPE)
    qg = jnp.concatenate([qng_ref[...]] * 2, axis=1)
    kg = jnp.concatenate([kng_ref[...]] * 2, axis=1)
    kpe2 = jnp.concatenate([kpe, kpe], axis=1)
    for h in range(0, MLA_HEADS, 2):
        hl = slice(h * LANES, (h + 2) * LANES)
        q_ref[0, :, hl] = (_norm_rope(q_all[:, hl], qg, ones_bd, perm, tab_ref, MLA_QK) * scale).astype(BF16)
        k_pair = _norm_rope(k_all[:, hl] + kpe2, kg, ones_bd, perm, tab_ref, MLA_QK).astype(BF16)
        k_ref[0, h] = k_pair[:, :LANES]
        k_ref[0, h + 1] = k_pair[:, LANES:]


def _mla_prep(u, tab, qlg, kvlg, wq, wk, wvt, qng, kng):
    b, n_tok, _ = u.shape
    t = n_tok // TILE
    w = MLA_HEADS * LANES
    cw = MLA_Q_LORA + MLA_KV_LORA + LANES
    const = lambda r, c: pl.BlockSpec((r, c), lambda i, j: (0, 0))
    return pl.pallas_call(
        _mla_prep_kernel,
        grid=(b, t),
        in_specs=[pl.BlockSpec((1, TILE, cw), lambda i, j: (i, j, C_MLA // cw)),
                  pl.BlockSpec((3, TILE, LANES), lambda i, j: (0, j, 0)),
                  const(1, MLA_Q_LORA), const(1, MLA_KV_LORA), const(MLA_Q_LORA, w), const(MLA_KV_LORA, w),
                  const(MLA_HEADS * MLA_V, MLA_KV_LORA), const(1, LANES), const(1, LANES)],
        out_specs=[pl.BlockSpec((1, TILE, w), lambda i, j: (i, j, 0)),
                   pl.BlockSpec((1, MLA_HEADS, TILE, LANES), lambda i, j: (i, 0, j, 0)),
                   pl.BlockSpec((1, MLA_HEADS * (MLA_V + V_ONES), TILE), lambda i, j: (i, 0, j))],
        out_shape=[jax.ShapeDtypeStruct((b, n_tok, w), BF16),
                   jax.ShapeDtypeStruct((b, MLA_HEADS, n_tok, LANES), BF16),
                   jax.ShapeDtypeStruct((b, MLA_HEADS * (MLA_V + V_ONES), n_tok), BF16)],
        compiler_params=_cparams(("arbitrary", "arbitrary")),
        name="mla_prep",
    )(u, tab, qlg, kvlg, wq, wk, wvt, qng, kng)


def _diff_prep_kernel(u_ref, tab_ref, qng_ref, kng_ref, q_ref, k_ref, vt_ref):
    u = u_ref[0]
    ones_bd = _group_ones(DIFF_HD)
    perm = _rope_perm(DIFF_HD, 0, DIFF_HD)
    qg = jnp.concatenate([qng_ref[...]] * 2, axis=1)
    kg = jnp.concatenate([kng_ref[...]] * 2, axis=1)
    scale = DIFF_HD ** -0.5 * LOG2E
    for c in range(0, DIFF_QK, PAIR):
        q_ref[0, :, c:c + PAIR] = (_norm_rope(u[:, c:c + PAIR], qg, ones_bd, perm, tab_ref, DIFF_HD) * scale).astype(BF16)
        k_pair = _norm_rope(u[:, DIFF_QK + c:DIFF_QK + c + PAIR], kg, ones_bd, perm, tab_ref, DIFF_HD).astype(BF16)
        k_ref[0, c // LANES] = k_pair[:, :LANES]
        k_ref[0, c // LANES + 1] = k_pair[:, LANES:]
    _store_vt(vt_ref, u[:, 2 * DIFF_QK:].T, DIFF_HEADS, 2 * DIFF_HD, 0)


def _diff_prep(u, tab, qng, kng):
    b, n_tok, _ = u.shape
    t = n_tok // TILE
    const = lambda r, c: pl.BlockSpec((r, c), lambda i, j: (0, 0))
    return pl.pallas_call(
        _diff_prep_kernel,
        grid=(b, t),
        in_specs=[pl.BlockSpec((1, TILE, DIFF_COLS), lambda i, j: (i, j, C_DIFF // DIFF_COLS)),
                  pl.BlockSpec((3, TILE, LANES), lambda i, j: (0, j, 0)),
                  const(1, LANES), const(1, LANES)],
        out_specs=[pl.BlockSpec((1, TILE, DIFF_QK), lambda i, j: (i, j, 0)),
                   pl.BlockSpec((1, DIFF_HEADS, TILE, LANES), lambda i, j: (i, 0, j, 0)),
                   pl.BlockSpec((1, DIFF_VW, TILE), lambda i, j: (i, 0, j))],
        out_shape=[jax.ShapeDtypeStruct((b, n_tok, DIFF_QK), BF16),
                   jax.ShapeDtypeStruct((b, DIFF_HEADS, n_tok, LANES), BF16),
                   jax.ShapeDtypeStruct((b, DIFF_VW, n_tok), BF16)],
        compiler_params=_cparams(("arbitrary", "arbitrary")),
        name="diff_prep",
    )(u, tab, qng, kng)


def _attn_kernel(q_ref, k_ref, vt_ref, *rest, n_heads, dv, v_ones, diff, lam_init, n_q_tiles):
    rest = list(rest)
    if n_q_tiles is None:
        rest.pop(0)
    if diff:
        lamp_ref, subg_ref = rest[:2]
        rest = rest[2:]
        lp = lamp_ref[...]
        lam = (jnp.exp(jnp.sum(lp[0:1] * lp[1:2], axis=-1, keepdims=True))
               - jnp.exp(jnp.sum(lp[2:3] * lp[3:4], axis=-1, keepdims=True)) + lam_init)
        lane = _iota((TILE, LANES), 1)
    o_ref, ot_ref, s_ref = rest[:3]
    dva = dv + v_ones
    n_iter = n_heads if diff else n_heads // 2
    n_keys = k_ref.shape[2]
    blocks = [(0, TILE)] + [(b0, KEY_BLOCK) for b0 in range(TILE, n_keys, KEY_BLOCK)]

    def streams(i):
        if diff:
            q = q_ref[0, :, i * LANES:(i + 1) * LANES]
            zero = jnp.zeros_like(q)
            vr = slice(i * dva, (i + 1) * dva)
            return [(i, vr, jnp.where(lane < DIFF_HD, q, zero)), (i, vr, jnp.where(lane < DIFF_HD, zero, q))]
        return [(h, slice(h * dva, (h + 1) * dva), q_ref[0, :, h * LANES:(h + 1) * LANES]) for h in (2 * i, 2 * i + 1)]

    def v_rows(i):
        if diff:
            return [slice(i * dva, (i + 1) * dva)] * 2
        return [slice(h * dva, (h + 1) * dva) for h in (2 * i, 2 * i + 1)]

    def finish(i, drained):
        (a0, a1), (l0, l1) = drained
        if diff:
            o = a0[:dv] / l0 - lam * (a1[:dv] / l1)
            ms = jnp.mean(o * o, axis=0, keepdims=True)
            o = o * lax.rsqrt(ms + NORM_EPS) * subg_ref[...] * (1.0 - lam_init)
            ot_ref[i * dv:(i + 1) * dv, :] = o
        else:
            o = jnp.concatenate([a0[:dv] / l0, a1[:dv] / l1], axis=0)
            ot_ref[2 * i * dv:2 * (i + 1) * dv, :] = o

    def phase(cur, cur_par, prev_rows, prev_par, prev_max):
        cur_max = [None, None]
        accs = [None, None]
        sums = [None, None]
        for start, size in blocks:
            rows = slice(start, start + size)
            if cur is not None:
                for st, (head, _, q) in enumerate(cur):
                    s = _dot_nt(k_ref[0, head, rows, :], q)
                    s_ref[cur_par, st, rows, :] = s
                    part = jnp.max(s, axis=0, keepdims=True)
                    cur_max[st] = part if cur_max[st] is None else jnp.maximum(cur_max[st], part)
            if prev_rows is not None:
                for st, vr in enumerate(prev_rows):
                    p = jnp.exp2(s_ref[prev_par, st, rows, :] - prev_max[st])
                    pv = _dot(vt_ref[0, vr, rows], p.astype(BF16))
                    accs[st] = pv if accs[st] is None else accs[st] + pv
                    if v_ones == 0:
                        ps = jnp.sum(p, axis=0, keepdims=True)
                        sums[st] = ps if sums[st] is None else sums[st] + ps
        if v_ones:
            sums = [a if a is None else a[dv:dv + 1] for a in accs]
        return cur_max, (accs, sums)

    def middle(first_max):
        prev_max = first_max
        for i in range(1, n_iter):
            cur_max, accs = phase(streams(i), i % 2, v_rows(i - 1), (i - 1) % 2, prev_max)
            finish(i - 1, accs)
            prev_max = cur_max
        return prev_max

    last_par = (n_iter - 1) % 2
    if n_q_tiles is None:
        last_max = middle(phase(streams(0), 0, None, 0, None)[0])
        finish(n_iter - 1, phase(None, 0, v_rows(n_iter - 1), last_par, last_max)[1])
        o_ref[0] = ot_ref[...].T.astype(o_ref.dtype)
        return

    mx_ref = rest[3]
    j = pl.program_id(1)

    @pl.when(j == 0)
    def _():
        s_ref[last_par] = jnp.zeros(s_ref.shape[1:], F32)
        mx_ref[...] = jnp.zeros_like(mx_ref)
        ot_ref[...] = jnp.zeros_like(ot_ref)

    first_max, accs = phase(streams(0), 0, v_rows(n_iter - 1), last_par, [mx_ref[0], mx_ref[1]])
    finish(n_iter - 1, accs)
    o_ref[0] = ot_ref[...].T.astype(o_ref.dtype)

    @pl.when(j < n_q_tiles)
    def _():
        last_max = middle(first_max)
        mx_ref[0] = last_max[0]
        mx_ref[1] = last_max[1]


def _attention(q, k, vt, *, n_heads, dv, v_ones, need_ctx, diff=False, lam_p=None, sub_g=None, lam_init=0.0):
    b, n_tok, w = q.shape
    nq = n_tok // TILE - 1
    ow = n_heads * dv
    vrows = n_heads * (dv + v_ones)
    n_iter = n_heads if diff else n_heads // 2
    assert n_iter % 2 == 0 and (n_tok - TILE) % KEY_BLOCK == 0
    extra_specs, extra_args = [], []
    if diff:
        extra_specs = [pl.BlockSpec(lam_p.shape, lambda i, j: (0, 0)), pl.BlockSpec(sub_g.shape, lambda i, j: (0, 0))]
        extra_args = [lam_p, sub_g]
    kern = functools.partial(_attn_kernel, n_heads=n_heads, dv=dv, v_ones=v_ones, diff=diff, lam_init=lam_init)
    name = "diff_attn" if diff else "mla_attn"
    y = pl.pallas_call(
        functools.partial(kern, n_q_tiles=nq),
        grid=(b, nq + 1),
        in_specs=[pl.BlockSpec((1, TILE, w), lambda i, j: (i, jnp.minimum(j, nq - 1) + 1, 0)),
                  pl.BlockSpec((1, n_heads, n_tok, LANES), lambda i, j: (i, 0, 0, 0)),
                  pl.BlockSpec((1, vrows, n_tok), lambda i, j: (i, 0, 0))] + extra_specs,
        out_specs=pl.BlockSpec((1, TILE, ow), lambda i, j: (i, jnp.maximum(j - 1, 0) + 1, 0)),
        out_shape=jax.ShapeDtypeStruct((b, n_tok, ow), BF16),
        scratch_shapes=[pltpu.VMEM((ow, TILE), F32), pltpu.VMEM((2, 2, n_tok, TILE), F32),
                        pltpu.VMEM((2, 1, TILE), F32)],
        compiler_params=_cparams(("arbitrary", "arbitrary")),
        name=name,
    )(q, k, vt, *extra_args)
    if not need_ctx:
        return y
    return pl.pallas_call(
        functools.partial(kern, n_q_tiles=None),
        grid=(b,),
        in_specs=[pl.BlockSpec((1, TILE, w), lambda i: (i, 0, 0)),
                  pl.BlockSpec((1, n_heads, TILE, LANES), lambda i: (i, 0, 0, 0)),
                  pl.BlockSpec((1, vrows, TILE), lambda i: (i, 0, 0)),
                  pl.BlockSpec(memory_space=pl.ANY)]
                 + [pl.BlockSpec(s.block_shape, lambda i: (0, 0)) for s in extra_specs],
        out_specs=pl.BlockSpec((1, TILE, ow), lambda i: (i, 0, 0)),
        out_shape=jax.ShapeDtypeStruct((b, n_tok, ow), BF16),
        input_output_aliases={3: 0},
        scratch_shapes=[pltpu.VMEM((ow, TILE), F32), pltpu.VMEM((2, 2, TILE, TILE), F32)],
        compiler_params=_cparams(("arbitrary",)),
        name=name + "_ctx",
    )(q, k, vt, y, *extra_args)


def _merge_kernel(*refs, n_src, first):
    (mod_ref, gof_ref, gob_ref, gz_ref, mla_ref, dif_ref, syf_ref, syb_ref, sz_ref,
     gates01_ref, gates23_ref, gng_ref, sng_ref, wb_ref, wo_ref, o_ref) = refs[n_src:]
    o = gof_ref[0] + gob_ref[0]
    z = gz_ref[0]
    ya = []
    for h in range(GDN_HEADS):
        hl = slice(h * GDN_DV, (h + 1) * GDN_DV)
        ya.append(_rms(o[:, hl], gng_ref[...]) * _silu(z[:, hl]))
    ya = jnp.concatenate(ya, axis=-1)
    y = (syf_ref[0] + syb_ref[0]) * _silu(sz_ref[0])
    gsz = SSM_INNER // SSM_GROUPS
    yd = jnp.concatenate([_rms(y[:, g * gsz:(g + 1) * gsz], sng_ref[:, g * gsz:(g + 1) * gsz])
                          for g in range(SSM_GROUPS)], axis=-1)
    ys = (ya, mla_ref[0], dif_ref[0], yd)
    m = None
    for i in range(N_BRANCH):
        gates_ref = gates01_ref if i < 2 else gates23_ref
        gate = _sigmoid(gates_ref[0, :, (i % 2) * D_MODEL:(i % 2 + 1) * D_MODEL].astype(F32))
        term = gate * _dot(ys[i].astype(BF16), wb_ref[i])
        m = term if m is None else m + term
    o_ref[0] = _stream_tile(refs[:n_src], first) + mod_ref[2:3, :] * _dot(m.astype(BF16), wo_ref[...])


def _merge(xs, mods, gdn_o, u, gates, y_mla, y_diff, ssd_y, gng, sng, wb, wo, *, need_ctx):
    first = 0 if need_ctx else 1
    src_specs, src_args, n_tok = _stream_specs(xs, first)
    b, d = mods.shape[0], mods.shape[-1]
    t = n_tok // TILE
    row = lambda w, cb=0: pl.BlockSpec((1, TILE, w), lambda i, j: (i, j + first, cb))
    const = lambda shape: pl.BlockSpec(shape, lambda i, j: (0,) * len(shape))
    aliases = {0: 0} if len(src_args) == 1 else {}
    assert need_ctx or len(src_args) == 1
    return pl.pallas_call(
        functools.partial(_merge_kernel, n_src=len(src_args), first=first),
        grid=(b, t - first),
        in_specs=src_specs + [
                  pl.BlockSpec((None, None, 6, d), lambda i, j: (i, jnp.minimum(j + first, 1), 0, 0)),
                  row(BRANCH_W), row(BRANCH_W), row(BRANCH_W, (C_GDN + 3 * BRANCH_W) // BRANCH_W),
                  row(BRANCH_W), row(BRANCH_W), row(BRANCH_W), row(BRANCH_W), row(BRANCH_W, C_SSMZ // BRANCH_W),
                  row(GATE_COLS // 2, 0), row(GATE_COLS // 2, 1),
                  const((1, GDN_DV)), const((1, SSM_INNER)), const((N_BRANCH, BRANCH_W, d)), const((d, d))],
        out_specs=row(d),
        out_shape=jax.ShapeDtypeStruct((b, n_tok, d), F32),
        input_output_aliases=aliases,
        compiler_params=_cparams(("arbitrary", "arbitrary")),
        name="merge",
    )(*src_args, mods, gdn_o[0], gdn_o[1], u, y_mla, y_diff, ssd_y[0], ssd_y[1], u,
      gates, gates, gng, sng, wb, wo)


def _mlp_kernel(x_ref, mod_ref, g_ref, w1_ref, w2_ref, o_ref):
    x = x_ref[0]
    h = _rms(x, g_ref[...]) * (1.0 + mod_ref[4:5, :]) + mod_ref[3:4, :]
    a = jnp.maximum(_dot(h.astype(BF16), w1_ref[...]), 0.0)
    o_ref[0] = x + mod_ref[5:6, :] * _dot((a * a).astype(BF16), w2_ref[...])


def _mlp(xs, mods, g, w1, w2, *, need_ctx):
    b, n_tok, d = xs.shape
    t = n_tok // TILE
    first = 0 if need_ctx else 1
    const = lambda shape: pl.BlockSpec(shape, lambda i, j: (0,) * len(shape))
    n_out = n_tok - first * TILE
    return pl.pallas_call(
        _mlp_kernel,
        grid=(b, t - first),
        in_specs=[pl.BlockSpec((1, TILE, d), lambda i, j: (i, j + first, 0)),
                  pl.BlockSpec((None, None, 6, d), lambda i, j: (i, jnp.minimum(j + first, 1), 0, 0)),
                  const((1, d)), const((d, D_FF)), const((D_FF, d))],
        out_specs=pl.BlockSpec((1, TILE, d), lambda i, j: (i, j, 0)),
        out_shape=jax.ShapeDtypeStruct((b, n_out, d), F32),
        compiler_params=_cparams(("arbitrary", "arbitrary")),
        name="mlp",
    )(xs, mods, g, w1, w2)


def _repack_w_in(w):
    d = w.shape[0]
    z = lambda n: jnp.zeros((d, n), w.dtype)
    o_mla = GDN_COLS
    o_diff = o_mla + MLA_COLS
    o_ssm = o_diff + DIFF_COLS
    o_gate = MIX_COLS
    gdn_ab = w[:, 2 * GDN_QK + 2 * GDN_VW:GDN_COLS]
    ssm_dt = w[:, o_ssm + 2 * SSM_INNER + 2 * SSM_BC:o_ssm + SSM_COLS]
    kpe = w[:, o_mla + MLA_Q_LORA + MLA_KV_LORA:o_mla + MLA_COLS]
    parts = [
        w[:, 0:2 * GDN_QK + 2 * GDN_VW],
        w[:, o_ssm + SSM_INNER:o_ssm + 2 * SSM_INNER + 2 * SSM_BC],
        w[:, o_diff:o_diff + DIFF_COLS],
        w[:, o_mla:o_mla + MLA_Q_LORA + MLA_KV_LORA], z(MLA_NOPE), kpe, z(LANES - MLA_QK),
        gdn_ab, ssm_dt, z(LANES - 4 * GDN_HEADS - 2 * SSM_HEADS),
        z(C_SSMZ - C_SMALL - LANES),
        w[:, o_ssm:o_ssm + SSM_INNER],
        w[:, o_gate:o_gate + GATE_COLS],
    ]
    out = jnp.concatenate(parts, axis=1)
    assert out.shape[1] == N_IN
    return out.astype(BF16)


def _lane_row(vals, offset):
    row = jnp.zeros((1, LANES), F32)
    return row.at[0, offset:offset + vals.shape[0]].set(vals.astype(F32))


def _rope_tables(n_lat, rot_dim, lane_offset, period):
    rows = n_lat // GRID_W
    row = jnp.repeat(jnp.arange(rows, dtype=F32), GRID_W)
    col = jnp.tile(jnp.arange(GRID_W, dtype=F32), rows)
    quarter = rot_dim // 4
    inv = ROPE_THETA ** (-jnp.arange(quarter, dtype=F32) / quarter)
    ar = row[:, None] * inv
    ac = col[:, None] * inv
    ang = jnp.concatenate([ar, ar, ac, ac], axis=-1)
    cos, sin = jnp.cos(ang), jnp.sin(ang)
    first = (np.arange(rot_dim) % (2 * quarter)) < quarter
    sin_a = jnp.where(first, -sin, 0.0)
    sin_b = jnp.where(first, 0.0, sin)
    reps = LANES // period
    def place(t, fill):
        blk = jnp.full((n_lat, period), fill, F32).at[:, lane_offset:lane_offset + rot_dim].set(t)
        blk = jnp.tile(blk, (1, reps))
        ctx = jnp.full((CTX_LEN, LANES), fill, F32)
        return jnp.concatenate([ctx, blk], axis=0)
    return jnp.stack([place(cos, 1.0), place(sin_a, 0.0), place(sin_b, 0.0)])


def kernel(x, c, ctx, c_ctx, ada_w, ada_b, norm1_g, norm2_g, w_in, gdn_conv, gdn_a_log, gdn_dt_bias, gdn_norm_g,
           mla_q_lora_g, mla_kv_lora_g, mla_w_uq, mla_w_ukv, mla_qn_g, mla_kn_g, diff_qn_g, diff_kn_g,
           diff_lambda, diff_sub_g, ssm_conv, ssm_conv_b, ssm_a_log, ssm_dt_bias, ssm_d, ssm_norm_g, w_branch,
           w_out, mlp_w1, mlp_w2):
    b, n_lat, d = x.shape
    depth = w_in.shape[0]
    assert ctx.shape[1] == CTX_LEN == TILE and n_lat % TILE == 0 and d == D_MODEL
    xs = (ctx, x) if depth > 1 else jnp.concatenate([ctx, x], axis=1)
    tab_mla = _rope_tables(n_lat, MLA_ROPE, MLA_NOPE, LANES)
    tab_diff = _rope_tables(n_lat, DIFF_HD, 0, DIFF_HD)
    rows = b + 8
    cc = jnp.zeros((rows, d), F32).at[:b].set(c).at[b].set(c_ctx)
    row2 = lambda v: v.reshape(1, -1).astype(F32)

    for l in range(depth):
        need_ctx = l < depth - 1
        lam_init = LAMBDA_BASE - LAMBDA_AMP * math.exp(-LAMBDA_RATE * l)
        mod = _ada(cc, ada_w[l], ada_b[l].reshape(1, -1))
        mod_lat = mod[:b].reshape(b, 6, d)
        mod_ctx = jnp.broadcast_to(mod[b].reshape(1, 6, d), (b, 6, d))
        mods = jnp.stack([mod_ctx, mod_lat], axis=1)

        u, gates = _inproj(xs, mods, row2(norm1_g[l]), _repack_w_in(w_in[l]))

        gdn_o = _gdn_scan(_gdn_prep(u, gdn_conv[l].astype(F32),
                                    _lane_row(gdn_a_log[l].reshape(-1), SM_A),
                                    _lane_row(gdn_dt_bias[l].reshape(-1), SM_A)))

        wq = mla_w_uq[l].reshape(MLA_Q_LORA, MLA_HEADS, MLA_QK)
        wq = jnp.pad(wq, ((0, 0), (0, 0), (0, LANES - MLA_QK))).reshape(MLA_Q_LORA, MLA_HEADS * LANES)
        wkv = mla_w_ukv[l].reshape(MLA_KV_LORA, MLA_HEADS, MLA_NOPE + MLA_V)
        wk = jnp.pad(wkv[:, :, :MLA_NOPE], ((0, 0), (0, 0), (0, LANES - MLA_NOPE))).reshape(MLA_KV_LORA, MLA_HEADS * LANES)
        wvt = wkv[:, :, MLA_NOPE:].reshape(MLA_KV_LORA, MLA_HEADS * MLA_V).T
        pad_g = lambda g: jnp.pad(g.astype(F32), (0, LANES - MLA_QK)).reshape(1, LANES)
        q_m, k_m, vt_m = _mla_prep(u, tab_mla, row2(mla_q_lora_g[l]), row2(mla_kv_lora_g[l]),
                                   wq.astype(BF16), wk.astype(BF16), wvt.astype(BF16),
                                   pad_g(mla_qn_g[l]), pad_g(mla_kn_g[l]))
        y_mla = _attention(q_m, k_m, vt_m, n_heads=MLA_HEADS, dv=MLA_V, v_ones=V_ONES, need_ctx=need_ctx)

        rep_g = lambda g: jnp.tile(g.astype(F32), LANES // DIFF_HD).reshape(1, LANES)
        q_d, k_d, vt_d = _diff_prep(u, tab_diff, rep_g(diff_qn_g[l]), rep_g(diff_kn_g[l]))
        y_diff = _attention(q_d, k_d, vt_d, n_heads=DIFF_HEADS, dv=2 * DIFF_HD, v_ones=0, need_ctx=need_ctx, diff=True,
                            lam_p=diff_lambda[l].astype(F32), sub_g=diff_sub_g[l].astype(F32).reshape(-1, 1),
                            lam_init=lam_init)

        ssd_y = _ssd(u, ssm_conv[l].astype(F32), row2(ssm_conv_b[l]),
                     _lane_row(ssm_a_log[l].reshape(-1), SM_DT), _lane_row(ssm_dt_bias[l].reshape(-1), SM_DT),
                     jnp.repeat(ssm_d[l].astype(F32), SSM_HEAD_DIM).reshape(1, -1))

        xs = _merge(xs, mods, gdn_o, u, gates, y_mla, y_diff, ssd_y, row2(gdn_norm_g[l]), row2(ssm_norm_g[l]),
                    w_branch[l].astype(BF16), w_out[l].astype(BF16), need_ctx=need_ctx)
        xs = _mlp(xs, mods, row2(norm2_g[l]), mlp_w1[l].astype(BF16), mlp_w2[l].astype(BF16), need_ctx=need_ctx)
    return xs
```

```python
import functools
import math

import jax
import jax.numpy as jnp
import numpy as np
from jax import lax
from jax.experimental import pallas as pl
from jax.experimental.pallas import tpu as pltpu

F32 = jnp.float32
BF16 = jnp.bfloat16

D_MODEL = 1024
CTX_LEN = 256
GRID_W = 64
ROPE_THETA = 10000.0
NORM_EPS = 1e-6
CONV_K = 5
N_BRANCH = 4
D_FF = 4 * D_MODEL
GDN_HEADS = 4
GDN_DK = 128
GDN_DV = 128
MLA_HEADS = 8
MLA_NOPE = 64
MLA_ROPE = 32
MLA_V = 64
MLA_QK = MLA_NOPE + MLA_ROPE
MLA_Q_LORA = 384
MLA_KV_LORA = 256
DIFF_HEADS = 4
DIFF_HD = 64
LAMBDA_BASE = 0.8
LAMBDA_AMP = 0.6
LAMBDA_RATE = 0.3
SSM_HEADS = 8
SSM_HEAD_DIM = 64
SSM_GROUPS = 2
SSM_STATE = 128
GDN_QK = GDN_HEADS * GDN_DK
GDN_VW = GDN_HEADS * GDN_DV
DIFF_QK = DIFF_HEADS * 2 * DIFF_HD
DIFF_VW = DIFF_HEADS * 2 * DIFF_HD
SSM_INNER = SSM_HEADS * SSM_HEAD_DIM
SSM_BC = SSM_GROUPS * SSM_STATE
BRANCH_W = 512
GDN_COLS = 2 * GDN_QK + 2 * GDN_VW + 4 * GDN_HEADS
MLA_COLS = MLA_Q_LORA + MLA_KV_LORA + MLA_ROPE
DIFF_COLS = 2 * DIFF_QK + DIFF_VW
SSM_COLS = 2 * SSM_INNER + 2 * SSM_BC + 2 * SSM_HEADS
MIX_COLS = GDN_COLS + MLA_COLS + DIFF_COLS + SSM_COLS
GATE_COLS = N_BRANCH * D_MODEL

LANES = 128
TILE = 256
CHUNK = 64
CPT = TILE // CHUNK
HALO = 8
VMEM_LIMIT = 56 * 1024 * 1024
V_ONES = 16
KEY_BLOCK = 256
LOG2E = 1.4426950408889634

C_GDN = 0
C_XBC = 2048
C_DIFF = 3072
C_MLA = 4608
C_SMALL = 5376
C_SSMZ = 5632
C_GATE = 6144
N_IN = 10240
IN_TN = 2048

SM_A = 0
SM_B = 8
SM_DT = 16


def _cparams(sem):
    return pltpu.CompilerParams(dimension_semantics=sem, vmem_limit_bytes=VMEM_LIMIT)


def _dot(a, b):
    return jnp.dot(a, b, preferred_element_type=F32)


def _dot_nt(a, b):
    return lax.dot_general(a, b, (((1,), (1,)), ((), ())), preferred_element_type=F32)


def _dot_tn(a, b):
    return lax.dot_general(a, b, (((0,), (0,)), ((), ())), preferred_element_type=F32)


def _split3(x):
    x1 = x.astype(BF16)
    r1 = x - x1.astype(F32)
    x2 = r1.astype(BF16)
    r2 = r1 - x2.astype(F32)
    return x1, x2, r2.astype(BF16)


def _dot_exact_lhs(a_bf, x):
    x1, x2, x3 = _split3(x)
    return _dot(a_bf, x1) + _dot(a_bf, x2) + _dot(a_bf, x3)


def _dot_exact_rhs(x, b_bf):
    x1, x2, x3 = _split3(x)
    return _dot(x1, b_bf) + _dot(x2, b_bf) + _dot(x3, b_bf)


def _sigmoid(x):
    return 0.5 * jnp.tanh(0.5 * x) + 0.5


def _silu(x):
    return x * _sigmoid(x)


def _softplus(x):
    return jnp.maximum(x, 0.0) + jnp.log1p(jnp.exp(-jnp.abs(x)))


def _iota(shape, dim):
    return lax.broadcasted_iota(jnp.int32, shape, dim)


def _chunk_tri(n, upper):
    r = _iota((n, n), 0)
    c = _iota((n, n), 1)
    same = (r // CHUNK) == (c // CHUNK)
    tri = (r <= c) if upper else (r >= c)
    return jnp.where(same & tri, 1.0, 0.0).astype(BF16)


def _expander(n_rows, first_row, n_groups, width):
    r = _iota((n_rows, n_groups * width), 0)
    c = _iota((n_rows, n_groups * width), 1)
    return jnp.where(r == first_row + c // width, 1.0, 0.0).astype(BF16)


def _ada_kernel(c_ref, w_ref, b_ref, o_ref):
    a = _silu(c_ref[...])
    a1, a2, a3 = _split3(a)
    w1, w2, w3 = _split3(w_ref[...])
    acc = _dot(a1, w1) + (_dot(a1, w2) + _dot(a2, w1)) + (_dot(a1, w3) + _dot(a2, w2) + _dot(a3, w1))
    o_ref[...] = acc + b_ref[...]


def _ada(cc, w, b, layer):
    rows, d = cc.shape
    n = w.shape[2]
    tn = 1536
    return pl.pallas_call(
        _ada_kernel,
        grid=(n // tn,),
        in_specs=[pl.BlockSpec((rows, d), lambda j: (0, 0)),
                  pl.BlockSpec((None, d, tn), lambda j: (layer, 0, j)),
                  pl.BlockSpec((None, 1, tn), lambda j: (layer, 0, j))],
        out_specs=pl.BlockSpec((rows, tn), lambda j: (0, j)),
        out_shape=jax.ShapeDtypeStruct((rows, n), F32),
        compiler_params=_cparams(("arbitrary",)),
        name="ada",
    )(cc, w, b)


def _rms(x, g):
    ms = jnp.mean(x * x, axis=-1, keepdims=True)
    return x * lax.rsqrt(ms + NORM_EPS) * g


def _stream_specs(xs, first=0):
    if isinstance(xs, tuple):
        ctx, x = xs
        d = x.shape[-1]
        specs = [pl.BlockSpec((1, TILE, d), lambda i, j: (i, 0, 0)),
                 pl.BlockSpec((1, TILE, d), lambda i, j: (i, jnp.maximum(j + first - 1, 0), 0))]
        return specs, [ctx, x], ctx.shape[1] + x.shape[1]
    d = xs.shape[-1]
    return [pl.BlockSpec((1, TILE, d), lambda i, j: (i, j + first, 0))], [xs], xs.shape[1]


def _stream_tile(refs, first=0):
    if len(refs) == 2:
        return jnp.where(pl.program_id(1) + first == 0, refs[0][0], refs[1][0])
    return refs[0][0]


def _inproj_kernel(*refs, n_src):
    mod_ref, g_ref, w_ref, o_ref, gate_ref = refs[n_src:]
    h = _rms(_stream_tile(refs[:n_src]), g_ref[...])
    hb = (h * (1.0 + mod_ref[1:2, :]) + mod_ref[0:1, :]).astype(BF16)
    for c in range(0, C_GATE, IN_TN):
        o_ref[0, :, c:c + IN_TN] = _dot(hb, w_ref[:, c:c + IN_TN])
    for c in range(0, GATE_COLS, IN_TN):
        gate_ref[0, :, c:c + IN_TN] = _dot(hb, w_ref[:, C_GATE + c:C_GATE + c + IN_TN]).astype(BF16)


def _inproj(xs, mods, g, w):
    src_specs, src_args, n_tok = _stream_specs(xs)
    b, d = mods.shape[0], mods.shape[-1]
    t = n_tok // TILE
    return pl.pallas_call(
        functools.partial(_inproj_kernel, n_src=len(src_args)),
        grid=(b, t),
        in_specs=src_specs + [
                  pl.BlockSpec((None, None, 6, d), lambda i, j: (i, jnp.minimum(j, 1), 0, 0)),
                  pl.BlockSpec((1, d), lambda i, j: (0, 0)),
                  pl.BlockSpec((d, N_IN), lambda i, j: (0, 0), pipeline_mode=pl.Buffered(1))],
        out_specs=[pl.BlockSpec((1, TILE, C_GATE), lambda i, j: (i, j, 0)),
                   pl.BlockSpec((1, TILE, GATE_COLS), lambda i, j: (i, j, 0))],
        out_shape=[jax.ShapeDtypeStruct((b, n_tok, C_GATE), F32),
                   jax.ShapeDtypeStruct((b, n_tok, GATE_COLS), BF16)],
        compiler_params=_cparams(("arbitrary", "arbitrary")),
        name="inproj",
    )(*src_args, mods, g, w)


def _halo_specs(width, col_block, n_tiles, tile_of):
    rpt = TILE // HALO
    last = n_tiles * rpt - 1
    main = pl.BlockSpec((1, TILE, width), lambda *ids: (ids[0], tile_of(*ids), col_block))
    prev = pl.BlockSpec((1, HALO, width),
                        lambda *ids: (ids[0], jnp.maximum(tile_of(*ids) * rpt - 1, 0), col_block))
    nxt = pl.BlockSpec((1, HALO, width),
                       lambda *ids: (ids[0], jnp.minimum(tile_of(*ids) * rpt + rpt, last), col_block))
    return main, prev, nxt


def _conv_tile(main_ref, prev_ref, next_ref, w_ref, ext_ref, tile, n_tiles):
    prev_ok = (tile >= 2).astype(F32)
    next_ok = jnp.logical_and(tile >= 1, tile < n_tiles - 1).astype(F32)
    ext_ref[0:HALO, :] = prev_ref[0] * prev_ok
    ext_ref[HALO:HALO + TILE, :] = main_ref[0]
    ext_ref[HALO + TILE:2 * HALO + TILE, :] = next_ref[0] * next_ok
    half = CONV_K // 2
    acc = None
    for k in range(CONV_K):
        term = w_ref[k:k + 1, :] * ext_ref[HALO - half + k:HALO - half + k + TILE, :]
        acc = term if acc is None else acc + term
    return acc


def _block_inverse_many(mats):
    n = mats[0].shape[0]
    ri = _iota((n, n), 0)
    ci = _iota((n, n), 1)
    eye = jnp.where(ri == ci, 1.0, 0.0).astype(F32)
    pair = (ri // 2) == (ci // 2)
    ts = [eye - jnp.where(pair, a, 0.0) for a in mats]
    abs_ = [a.astype(BF16) for a in mats]
    zero = jnp.zeros((n, n), BF16)
    size = 2
    while size < CHUNK:
        off = ((ri // (2 * size)) == (ci // (2 * size))) & ((ri // size) != (ci // size))
        tbs = [t.astype(BF16) for t in ts]
        xs = [_dot(jnp.where(off, ab, zero), tb).astype(BF16) for ab, tb in zip(abs_, tbs)]
        ts = [t - _dot(tb, x) for t, tb, x in zip(ts, tbs, xs)]
        size *= 2
    return ts


def _gdn_prep_kernel(main_ref, prev_ref, next_ref, sm_ref, cw_ref, alog_ref, dtb_ref,
                     u0_ref, wq0_ref, kd0_ref, at0_ref, gl0_ref,
                     u1_ref, wq1_ref, kd1_ref, at1_ref, gl1_ref,
                     ext_ref, *, n_tiles):
    tile = pl.program_id(1)
    qkv = _silu(_conv_tile(main_ref, prev_ref, next_ref, cw_ref, ext_ref, tile, n_tiles))
    sm = sm_ref[0]
    g_all = -jnp.exp(alog_ref[...]) * _softplus(sm + dtb_ref[...])
    beta_all = _sigmoid(sm)
    nh = GDN_HEADS
    exp_b = _expander(LANES, SM_B, 2 * nh, LANES)
    beta_x = _dot_exact_rhs(beta_all, exp_b)
    outs = ((u0_ref, wq0_ref, kd0_ref, at0_ref, gl0_ref), (u1_ref, wq1_ref, kd1_ref, at1_ref, gl1_ref))
    ri = _iota((TILE, TILE), 0)
    ci = _iota((TILE, TILE), 1)
    same = (ri // CHUNK) == (ci // CHUNK)
    cums, masks = [], []
    for d in range(2):
        cum = _dot_exact_lhs(_chunk_tri(TILE, upper=(d == 1)), g_all)
        cums.append((_dot_exact_rhs(cum, _expander(LANES, SM_A + d * nh, nh, LANES)), cum.T))
        masks.append((same & ((ri >= ci) if d == 0 else (ri <= ci)), same & ((ri > ci) if d == 0 else (ri < ci))))

    chains, a_mats = [], []
    for h in range(nh):
        q = qkv[:, h * GDN_DK:(h + 1) * GDN_DK]
        k = qkv[:, GDN_QK + h * GDN_DK:GDN_QK + (h + 1) * GDN_DK]
        v = qkv[:, 2 * GDN_QK + h * GDN_DV:2 * GDN_QK + (h + 1) * GDN_DV]
        q = q * lax.rsqrt(jnp.sum(q * q, axis=-1, keepdims=True) + 1e-6) * (GDN_DK ** -0.5)
        k = k * lax.rsqrt(jnp.sum(k * k, axis=-1, keepdims=True) + 1e-6)
        kb = k.astype(BF16)
        kq = _dot_nt(jnp.concatenate([kb, q.astype(BF16)], axis=0), kb)
        for d in range(2):
            col = d * nh + h
            lanes = slice(col * LANES, (col + 1) * LANES)
            cum_x, cum_t = cums[d]
            incl, strict = masks[d]
            g_col = cum_x[:, h * LANES:(h + 1) * LANES]
            b_col = beta_x[:, lanes]
            diff = jnp.concatenate([g_col, g_col], axis=1) - cum_t[SM_A + col:SM_A + col + 1, :]
            decay = jnp.where(incl, jnp.exp(jnp.where(incl, diff, 0.0)), 0.0)
            a_mats.append(jnp.where(strict, kq[:TILE] * jnp.concatenate([b_col, b_col], axis=1) * decay, 0.0))
            e_g = jnp.exp(g_col)
            rhs = jnp.concatenate([v * b_col, k * b_col * e_g], axis=1).astype(BF16)
            chains.append((h, d, q, k, g_col, e_g, rhs, kq[TILE:] * decay))

    t_mats = _block_inverse_many(a_mats)
    for (h, d, q, k, g_col, e_g, rhs, attn), t_mat in zip(chains, t_mats):
        uw = _dot(t_mat.astype(BF16), rhs)
        qg = (q * e_g).astype(BF16)
        u_ref, wq_ref, kd_ref, at_ref, gl_ref = outs[d]
        hl = slice(h * LANES, (h + 1) * LANES)
        for c in range(CPT):
            rows = slice(c * CHUNK, (c + 1) * CHUNK)
            last = c * CHUNK + (CHUNK - 1 if d == 0 else 0)
            g_last = g_col[last:last + 1, :]
            slot = c if d == 0 else CPT - 1 - c
            u_ref[0, 0, slot, :, hl] = uw[rows, :GDN_DV]
            wq_ref[0, 0, slot, 0:CHUNK, hl] = uw[rows, GDN_DV:].astype(BF16)
            wq_ref[0, 0, slot, CHUNK:2 * CHUNK, hl] = qg[rows]
            kd_ref[0, 0, slot, :, hl] = (k[rows] * jnp.exp(g_last - g_col[rows])).astype(BF16)
            at_ref[0, 0, slot, :, h * CHUNK:(h + 1) * CHUNK] = attn[rows, rows].astype(BF16)
            gl_ref[0, 0, slot, :, hl] = jnp.exp(g_last)


def _bwd_block(j, n_tiles):
    return jnp.where(j == 0, 0, n_tiles - j)


def _gdn_prep(u, conv_w, alog_row, dtb_row):
    b, n_tok, _ = u.shape
    t = n_tok // TILE
    w = GDN_VW
    cw = 2 * GDN_QK + GDN_VW
    main, prev, nxt = _halo_specs(cw, 0, t, lambda i, j: j)
    fwd = lambda i, j: (i, j, 0, 0, 0)
    bwd = lambda i, j: (i, _bwd_block(j, t), 0, 0, 0)

    def outs(imap):
        return [pl.BlockSpec((1, 1, CPT, CHUNK, w), imap),
                pl.BlockSpec((1, 1, CPT, 2 * CHUNK, w), imap),
                pl.BlockSpec((1, 1, CPT, CHUNK, w), imap),
                pl.BlockSpec((1, 1, CPT, CHUNK, GDN_HEADS * CHUNK), imap),
                pl.BlockSpec((1, 1, CPT, 1, w), imap)]

    shapes = [jax.ShapeDtypeStruct((b, t, CPT, CHUNK, w), F32),
              jax.ShapeDtypeStruct((b, t, CPT, 2 * CHUNK, w), BF16),
              jax.ShapeDtypeStruct((b, t, CPT, CHUNK, w), BF16),
              jax.ShapeDtypeStruct((b, t, CPT, CHUNK, GDN_HEADS * CHUNK), BF16),
              jax.ShapeDtypeStruct((b, t, CPT, 1, w), F32)]
    return pl.pallas_call(
        functools.partial(_gdn_prep_kernel, n_tiles=t),
        grid=(b, t),
        in_specs=[main, prev, nxt,
                  pl.BlockSpec((1, TILE, LANES), lambda i, j: (i, j, C_SMALL // LANES)),
                  pl.BlockSpec((CONV_K, cw), lambda i, j: (0, 0)),
                  pl.BlockSpec((1, LANES), lambda i, j: (0, 0)),
                  pl.BlockSpec((1, LANES), lambda i, j: (0, 0))],
        out_specs=outs(fwd) + outs(bwd),
        out_shape=shapes + shapes,
        scratch_shapes=[pltpu.VMEM((TILE + 2 * HALO, cw), F32)],
        compiler_params=_cparams(("arbitrary", "arbitrary")),
        name="gdn_prep",
    )(u, u, u, u, conv_w, alog_row, dtb_row)


def _gdn_scan_kernel(u0_ref, wq0_ref, kd0_ref, at0_ref, gl0_ref,
                     u1_ref, wq1_ref, kd1_ref, at1_ref, gl1_ref,
                     of_ref, ob_ref, s_ref):
    @pl.when(pl.program_id(1) == 0)
    def _():
        s_ref[...] = jnp.zeros_like(s_ref)

    ins = ((u0_ref, wq0_ref, kd0_ref, at0_ref, gl0_ref), (u1_ref, wq1_ref, kd1_ref, at1_ref, gl1_ref))
    chains = [(d, h) for d in range(2) for h in range(GDN_HEADS)]
    hls = [slice(h * LANES, (h + 1) * LANES) for _, h in chains]
    states = [s_ref[d, h] for d, h in chains]
    for slot in range(CPT):
        rs = [_dot(ins[d][1][0, 0, slot, :, hl], s.astype(BF16)) for (d, _), hl, s in zip(chains, hls, states)]
        vbs = [(ins[d][0][0, 0, slot, :, hl] - r[0:CHUNK]).astype(BF16) for (d, _), hl, r in zip(chains, hls, rs)]
        outs = [r[CHUNK:2 * CHUNK] + _dot(ins[d][3][0, 0, slot, :, h * CHUNK:(h + 1) * CHUNK], vb)
                for (d, h), r, vb in zip(chains, rs, vbs)]
        states = [s * ins[d][4][0, 0, slot, :, hl] + _dot_tn(ins[d][2][0, 0, slot, :, hl], vb)
                  for (d, _), hl, s, vb in zip(chains, hls, states, vbs)]
        for (d, _), hl, o in zip(chains, hls, outs):
            c = slot if d == 0 else CPT - 1 - slot
            (of_ref if d == 0 else ob_ref)[0, c * CHUNK:(c + 1) * CHUNK, hl] = o
    for (d, h), s in zip(chains, states):
        s_ref[d, h] = s


def _gdn_scan(prep):
    b, t = prep[0].shape[:2]
    w = GDN_VW
    imap = lambda i, j: (i, j, 0, 0, 0)
    specs = [pl.BlockSpec((1, 1, CPT, CHUNK, w), imap),
             pl.BlockSpec((1, 1, CPT, 2 * CHUNK, w), imap),
             pl.BlockSpec((1, 1, CPT, CHUNK, w), imap),
             pl.BlockSpec((1, 1, CPT, CHUNK, GDN_HEADS * CHUNK), imap),
             pl.BlockSpec((1, 1, CPT, 1, w), imap)]
    return pl.pallas_call(
        _gdn_scan_kernel,
        grid=(b, t),
        in_specs=specs + specs,
        out_specs=[pl.BlockSpec((1, TILE, w), lambda i, j: (i, j, 0)),
                   pl.BlockSpec((1, TILE, w), lambda i, j: (i, _bwd_block(j, t), 0))],
        out_shape=[jax.ShapeDtypeStruct((b, t * TILE, w), F32)] * 2,
        scratch_shapes=[pltpu.VMEM((2, GDN_HEADS, GDN_DK, GDN_DV), F32)],
        compiler_params=_cparams(("arbitrary", "arbitrary")),
        name="gdn_scan",
    )(*prep)


def _ssd_direction(d, xbc, sm, y_ref, st_ref, alog_ref, dtb_ref, dskip_ref):
    nh, hd, ng = SSM_HEADS, SSM_HEAD_DIM, SSM_GROUPS
    hpg = nh // ng
    gw = hpg * hd
    xs = xbc[:, :SSM_INNER]
    dt_all = _softplus(sm + dtb_ref[...])
    da_all = dt_all * (-jnp.exp(alog_ref[...]))
    cum = _dot_exact_lhs(_chunk_tri(TILE, upper=(d == 1)), da_all)
    cum_t = cum.T
    expand = _expander(LANES, SM_DT + d * nh, nh, hd)
    dt_x = _dot_exact_rhs(dt_all, expand)
    cum_x = _dot_exact_rhs(cum, expand)
    xdt = xs * dt_x
    ri = _iota((CHUNK, gw), 0)
    ci = _iota((CHUNK, gw), 1) % CHUNK
    incl = (ri >= ci) if d == 0 else (ri <= ci)
    head_diag = (_iota((gw, gw), 0) // hd) == (_iota((gw, gw), 1) // hd)
    zero_bf = jnp.zeros((gw, gw), BF16)
    for step in range(CPT):
        c = step if d == 0 else CPT - 1 - step
        rows = slice(c * CHUNK, (c + 1) * CHUNK)
        last = c * CHUNK + (CHUNK - 1 if d == 0 else 0)
        cum_c = cum_x[rows]
        cum_last = cum_x[last:last + 1]
        decay_states = jnp.exp(cum_last - cum_c)
        in_decay = jnp.exp(cum_c)
        chunk_decay = jnp.exp(cum_last)
        for g in range(ng):
            gl = slice(g * gw, (g + 1) * gw)
            bm = xbc[rows, SSM_INNER + g * SSM_STATE:SSM_INNER + (g + 1) * SSM_STATE].astype(BF16)
            cm = xbc[rows, SSM_INNER + SSM_BC + g * SSM_STATE:SSM_INNER + SSM_BC + (g + 1) * SSM_STATE].astype(BF16)
            state = st_ref[d, g]
            y_off = _dot(cm, state.astype(BF16)) * in_decay[:, gl]
            xdt_c = xdt[rows, gl]
            st_ref[d, g] = state * chunk_decay[:, gl] + _dot_tn(bm, (xdt_c * decay_states[:, gl]).astype(BF16))
            scores = _dot_nt(cm, jnp.concatenate([bm] * hpg, axis=0))
            col0 = SM_DT + d * nh + g * hpg
            row_terms = jnp.concatenate([cum_t[col0 + hh:col0 + hh + 1, rows] for hh in range(hpg)], axis=1)
            seg = cum_c[:, gl] - row_terms
            lmat = jnp.where(incl, jnp.exp(jnp.where(incl, seg, 0.0)), 0.0)
            x_bd = jnp.where(head_diag, jnp.concatenate([xdt_c.astype(BF16)] * hpg, axis=0), zero_bf)
            y = _dot((scores * lmat).astype(BF16), x_bd) + y_off
            if d == 0:
                y = y + dskip_ref[:, gl] * xs[rows, gl]
            y_ref[0, rows, gl] = y


def _ssd_kernel(m_ref, p_ref, n_ref, sm_ref, cw_ref, cb_ref, alog_ref, dtb_ref, dskip_ref,
                yf_ref, yb_ref, ext_ref, st_ref, cache_ref, *, n_tiles):
    s = pl.program_id(1)

    @pl.when(s == 0)
    def _():
        st_ref[...] = jnp.zeros_like(st_ref)

    @pl.when(s < n_tiles)
    def _():
        xbc = _silu(_conv_tile(m_ref, p_ref, n_ref, cw_ref, ext_ref, s, n_tiles) + cb_ref[...])
        cache_ref[s] = xbc
        _ssd_direction(0, xbc, sm_ref[0], yf_ref, st_ref, alog_ref, dtb_ref, dskip_ref)

    @pl.when(s >= n_tiles)
    def _():
        tile = _bwd_block(s - n_tiles, n_tiles)
        _ssd_direction(1, cache_ref[tile], sm_ref[0], yb_ref, st_ref, alog_ref, dtb_ref, dskip_ref)


def _ssd(u, conv_w, conv_b, alog_row, dtb_row, dskip_row):
    b, n_tok, _ = u.shape
    t = n_tok // TILE
    cw = SSM_INNER + 2 * SSM_BC
    main, prev, nxt = _halo_specs(cw, C_XBC // cw, t, lambda i, s: jnp.minimum(s, t - 1))
    tile_of = lambda s: jnp.where(s < t, s, _bwd_block(s - t, t))
    const = lambda r, c: pl.BlockSpec((r, c), lambda i, s: (0, 0))
    return pl.pallas_call(
        functools.partial(_ssd_kernel, n_tiles=t),
        grid=(b, 2 * t),
        in_specs=[main, prev, nxt,
                  pl.BlockSpec((1, TILE, LANES), lambda i, s: (i, tile_of(s), C_SMALL // LANES)),
                  const(CONV_K, cw), const(1, cw), const(1, LANES), const(1, LANES), const(1, SSM_INNER)],
        out_specs=[pl.BlockSpec((1, TILE, SSM_INNER), lambda i, s: (i, jnp.minimum(s, t - 1), 0)),
                   pl.BlockSpec((1, TILE, SSM_INNER), lambda i, s: (i, jnp.where(s < t, 0, _bwd_block(s - t, t)), 0))],
        out_shape=[jax.ShapeDtypeStruct((b, n_tok, SSM_INNER), F32)] * 2,
        scratch_shapes=[pltpu.VMEM((TILE + 2 * HALO, cw), F32),
                        pltpu.VMEM((2, SSM_GROUPS, SSM_STATE, SSM_INNER // SSM_GROUPS), F32),
                        pltpu.VMEM((t, TILE, cw), F32)],
        compiler_params=_cparams(("arbitrary", "arbitrary")),
        name="ssd",
    )(u, u, u, u, conv_w, conv_b, alog_row, dtb_row, dskip_row)


PAIR = 2 * LANES


def _group_ones(group):
    r = _iota((PAIR, PAIR), 0)
    c = _iota((PAIR, PAIR), 1)
    return jnp.where((r // group) == (c // group), 1.0, 0.0).astype(BF16)


def _rope_perm(period, lane_offset, rot_dim):
    r = _iota((PAIR, PAIR), 0)
    c = _iota((PAIR, PAIR), 1)
    q = rot_dim // 4
    cl = c % period - lane_offset
    src = jnp.where((cl % (2 * q)) < q, c + q, c - q)
    return jnp.where((cl >= 0) & (cl < rot_dim) & (r == src), 1.0, 0.0).astype(BF16)


def _norm_rope(x, gain, ones_bd, perm, tab_ref, n_real):
    ss = _dot((x * x).astype(BF16), ones_bd)
    xn = x * lax.rsqrt(ss / n_real + NORM_EPS) * gain
    hi = xn.astype(BF16)
    lo = (xn - hi.astype(F32)).astype(BF16)
    rot = _dot(hi, perm) + _dot(lo, perm)
    cos = jnp.concatenate([tab_ref[0]] * 2, axis=1)
    sin = jnp.concatenate([tab_ref[1] + tab_ref[2]] * 2, axis=1)
    return xn * cos + rot * sin


def _store_vt(vt_ref, vt, n_heads, dv, n_ones):
    for h in range(n_heads):
        base = h * (dv + n_ones)
        vt_ref[0, base:base + dv, :] = vt[h * dv:(h + 1) * dv].astype(BF16)
        if n_ones:
            vt_ref[0, base + dv:base + dv + n_ones, :] = jnp.ones((n_ones, vt.shape[1]), BF16)


def _mla_prep_kernel(u_ref, tab_ref, qlg_ref, kvlg_ref, wq_ref, wk_ref, wvt_ref, qng_ref, kng_ref,
                     q_ref, k_ref, vt_ref):
    u = u_ref[0]
    cq = _rms(u[:, :MLA_Q_LORA], qlg_ref[...]).astype(BF16)
    ckv = _rms(u[:, MLA_Q_LORA:MLA_Q_LORA + MLA_KV_LORA], kvlg_ref[...]).astype(BF16)
    kpe = u[:, MLA_Q_LORA + MLA_KV_LORA:]
    q_all = _dot(cq, wq_ref[...])
    k_all = _dot(ckv, wk_ref[...])
    _store_vt(vt_ref, _dot_nt(wvt_ref[...], ckv), MLA_HEADS, MLA_V, V_ONES)
    scale = MLA_QK ** -0.5 * LOG2E
    ones_bd = _group_ones(LANES)
    perm = _rope_perm(LANES, MLA_NOPE, MLA_ROPE)
    qg = jnp.concatenate([qng_ref[...]] * 2, axis=1)
    kg = jnp.concatenate([kng_ref[...]] * 2, axis=1)
    kpe2 = jnp.concatenate([kpe, kpe], axis=1)
    for h in range(0, MLA_HEADS, 2):
        hl = slice(h * LANES, (h + 2) * LANES)
        q_ref[0, :, hl] = (_norm_rope(q_all[:, hl], qg, ones_bd, perm, tab_ref, MLA_QK) * scale).astype(BF16)
        k_pair = _norm_rope(k_all[:, hl] + kpe2, kg, ones_bd, perm, tab_ref, MLA_QK).astype(BF16)
        k_ref[0, h] = k_pair[:, :LANES]
        k_ref[0, h + 1] = k_pair[:, LANES:]


def _mla_prep(u, tab, qlg, kvlg, wq, wk, wvt, qng, kng):
    b, n_tok, _ = u.shape
    t = n_tok // TILE
    w = MLA_HEADS * LANES
    cw = MLA_Q_LORA + MLA_KV_LORA + LANES
    const = lambda r, c: pl.BlockSpec((r, c), lambda i, j: (0, 0))
    return pl.pallas_call(
        _mla_prep_kernel,
        grid=(b, t),
        in_specs=[pl.BlockSpec((1, TILE, cw), lambda i, j: (i, j, C_MLA // cw)),
                  pl.BlockSpec((3, TILE, LANES), lambda i, j: (0, j, 0)),
                  const(1, MLA_Q_LORA), const(1, MLA_KV_LORA), const(MLA_Q_LORA, w), const(MLA_KV_LORA, w),
                  const(MLA_HEADS * MLA_V, MLA_KV_LORA), const(1, LANES), const(1, LANES)],
        out_specs=[pl.BlockSpec((1, TILE, w), lambda i, j: (i, j, 0)),
                   pl.BlockSpec((1, MLA_HEADS, TILE, LANES), lambda i, j: (i, 0, j, 0)),
                   pl.BlockSpec((1, MLA_HEADS * (MLA_V + V_ONES), TILE), lambda i, j: (i, 0, j))],
        out_shape=[jax.ShapeDtypeStruct((b, n_tok, w), BF16),
                   jax.ShapeDtypeStruct((b, MLA_HEADS, n_tok, LANES), BF16),
                   jax.ShapeDtypeStruct((b, MLA_HEADS * (MLA_V + V_ONES), n_tok), BF16)],
        compiler_params=_cparams(("arbitrary", "arbitrary")),
        name="mla_prep",
    )(u, tab, qlg, kvlg, wq, wk, wvt, qng, kng)


def _diff_prep_kernel(u_ref, tab_ref, qng_ref, kng_ref, q_ref, k_ref, vt_ref):
    u = u_ref[0]
    ones_bd = _group_ones(DIFF_HD)
    perm = _rope_perm(DIFF_HD, 0, DIFF_HD)
    qg = jnp.concatenate([qng_ref[...]] * 2, axis=1)
    kg = jnp.concatenate([kng_ref[...]] * 2, axis=1)
    scale = DIFF_HD ** -0.5 * LOG2E
    for c in range(0, DIFF_QK, PAIR):
        q_ref[0, :, c:c + PAIR] = (_norm_rope(u[:, c:c + PAIR], qg, ones_bd, perm, tab_ref, DIFF_HD) * scale).astype(BF16)
        k_pair = _norm_rope(u[:, DIFF_QK + c:DIFF_QK + c + PAIR], kg, ones_bd, perm, tab_ref, DIFF_HD).astype(BF16)
        k_ref[0, c // LANES] = k_pair[:, :LANES]
        k_ref[0, c // LANES + 1] = k_pair[:, LANES:]
    _store_vt(vt_ref, u[:, 2 * DIFF_QK:].T, DIFF_HEADS, 2 * DIFF_HD, 0)


def _diff_prep(u, tab, qng, kng):
    b, n_tok, _ = u.shape
    t = n_tok // TILE
    const = lambda r, c: pl.BlockSpec((r, c), lambda i, j: (0, 0))
    return pl.pallas_call(
        _diff_prep_kernel,
        grid=(b, t),
        in_specs=[pl.BlockSpec((1, TILE, DIFF_COLS), lambda i, j: (i, j, C_DIFF // DIFF_COLS)),
                  pl.BlockSpec((3, TILE, LANES), lambda i, j: (0, j, 0)),
                  const(1, LANES), const(1, LANES)],
        out_specs=[pl.BlockSpec((1, TILE, DIFF_QK), lambda i, j: (i, j, 0)),
                   pl.BlockSpec((1, DIFF_HEADS, TILE, LANES), lambda i, j: (i, 0, j, 0)),
                   pl.BlockSpec((1, DIFF_VW, TILE), lambda i, j: (i, 0, j))],
        out_shape=[jax.ShapeDtypeStruct((b, n_tok, DIFF_QK), BF16),
                   jax.ShapeDtypeStruct((b, DIFF_HEADS, n_tok, LANES), BF16),
                   jax.ShapeDtypeStruct((b, DIFF_VW, n_tok), BF16)],
        compiler_params=_cparams(("arbitrary", "arbitrary")),
        name="diff_prep",
    )(u, tab, qng, kng)


def _attn_kernel(q_ref, k_ref, vt_ref, *rest, n_heads, dv, v_ones, diff, lam_init, n_q_tiles):
    rest = list(rest)
    if n_q_tiles is None:
        rest.pop(0)
    if diff:
        lamp_ref, subg_ref = rest[:2]
        rest = rest[2:]
        lp = lamp_ref[...]
        lam = (jnp.exp(jnp.sum(lp[0:1] * lp[1:2], axis=-1, keepdims=True))
               - jnp.exp(jnp.sum(lp[2:3] * lp[3:4], axis=-1, keepdims=True)) + lam_init)
        lane = _iota((TILE, LANES), 1)
    o_ref, ot_ref, s_ref = rest[:3]
    dva = dv + v_ones
    n_iter = n_heads if diff else n_heads // 2
    n_keys = k_ref.shape[2]
    blocks = [(0, TILE)] + [(b0, KEY_BLOCK) for b0 in range(TILE, n_keys, KEY_BLOCK)]

    def streams(i):
        if diff:
            q = q_ref[0, :, i * LANES:(i + 1) * LANES]
            zero = jnp.zeros_like(q)
            vr = slice(i * dva, (i + 1) * dva)
            return [(i, vr, jnp.where(lane < DIFF_HD, q, zero)), (i, vr, jnp.where(lane < DIFF_HD, zero, q))]
        return [(h, slice(h * dva, (h + 1) * dva), q_ref[0, :, h * LANES:(h + 1) * LANES]) for h in (2 * i, 2 * i + 1)]

    def v_rows(i):
        if diff:
            return [slice(i * dva, (i + 1) * dva)] * 2
        return [slice(h * dva, (h + 1) * dva) for h in (2 * i, 2 * i + 1)]

    def finish(i, drained):
        (a0, a1), (l0, l1) = drained
        if diff:
            o = a0[:dv] / l0 - lam * (a1[:dv] / l1)
            ms = jnp.mean(o * o, axis=0, keepdims=True)
            o = o * lax.rsqrt(ms + NORM_EPS) * subg_ref[...] * (1.0 - lam_init)
            ot_ref[i * dv:(i + 1) * dv, :] = o
        else:
            o = jnp.concatenate([a0[:dv] / l0, a1[:dv] / l1], axis=0)
            ot_ref[2 * i * dv:2 * (i + 1) * dv, :] = o

    def phase(cur, cur_par, prev_rows, prev_par, prev_max):
        cur_max = [None, None]
        accs = [None, None]
        sums = [None, None]
        for start, size in blocks:
            rows = slice(start, start + size)
            if cur is not None:
                for st, (head, _, q) in enumerate(cur):
                    s = _dot_nt(k_ref[0, head, rows, :], q)
                    s_ref[cur_par, st, rows, :] = s
                    part = jnp.max(s, axis=0, keepdims=True)
                    cur_max[st] = part if cur_max[st] is None else jnp.maximum(cur_max[st], part)
            if prev_rows is not None:
                for st, vr in enumerate(prev_rows):
                    p = jnp.exp2(s_ref[prev_par, st, rows, :] - prev_max[st])
                    pv = _dot(vt_ref[0, vr, rows], p.astype(BF16))
                    accs[st] = pv if accs[st] is None else accs[st] + pv
                    if v_ones == 0:
                        ps = jnp.sum(p, axis=0, keepdims=True)
                        sums[st] = ps if sums[st] is None else sums[st] + ps
        if v_ones:
            sums = [a if a is None else a[dv:dv + 1] for a in accs]
        return cur_max, (accs, sums)

    def middle(first_max):
        prev_max = first_max
        for i in range(1, n_iter):
            cur_max, accs = phase(streams(i), i % 2, v_rows(i - 1), (i - 1) % 2, prev_max)
            finish(i - 1, accs)
            prev_max = cur_max
        return prev_max

    last_par = (n_iter - 1) % 2
    if n_q_tiles is None:
        last_max = middle(phase(streams(0), 0, None, 0, None)[0])
        finish(n_iter - 1, phase(None, 0, v_rows(n_iter - 1), last_par, last_max)[1])
        o_ref[0] = ot_ref[...].T.astype(o_ref.dtype)
        return

    mx_ref = rest[3]
    j = pl.program_id(1)

    @pl.when(j == 0)
    def _():
        s_ref[last_par] = jnp.zeros(s_ref.shape[1:], F32)
        mx_ref[...] = jnp.zeros_like(mx_ref)
        ot_ref[...] = jnp.zeros_like(ot_ref)

    first_max, accs = phase(streams(0), 0, v_rows(n_iter - 1), last_par, [mx_ref[0], mx_ref[1]])
    finish(n_iter - 1, accs)
    o_ref[0] = ot_ref[...].T.astype(o_ref.dtype)

    @pl.when(j < n_q_tiles)
    def _():
        last_max = middle(first_max)
        mx_ref[0] = last_max[0]
        mx_ref[1] = last_max[1]


def _attention(q, k, vt, *, n_heads, dv, v_ones, need_ctx, diff=False, lam_p=None, sub_g=None, lam_init=0.0):
    b, n_tok, w = q.shape
    nq = n_tok // TILE - 1
    ow = n_heads * dv
    vrows = n_heads * (dv + v_ones)
    n_iter = n_heads if diff else n_heads // 2
    assert n_iter % 2 == 0 and (n_tok - TILE) % KEY_BLOCK == 0
    extra_specs, extra_args = [], []
    if diff:
        extra_specs = [pl.BlockSpec(lam_p.shape, lambda i, j: (0, 0)), pl.BlockSpec(sub_g.shape, lambda i, j: (0, 0))]
        extra_args = [lam_p, sub_g]
    kern = functools.partial(_attn_kernel, n_heads=n_heads, dv=dv, v_ones=v_ones, diff=diff, lam_init=lam_init)
    name = "diff_attn" if diff else "mla_attn"
    y = pl.pallas_call(
        functools.partial(kern, n_q_tiles=nq),
        grid=(b, nq + 1),
        in_specs=[pl.BlockSpec((1, TILE, w), lambda i, j: (i, jnp.minimum(j, nq - 1) + 1, 0)),
                  pl.BlockSpec((1, n_heads, n_tok, LANES), lambda i, j: (i, 0, 0, 0)),
                  pl.BlockSpec((1, vrows, n_tok), lambda i, j: (i, 0, 0))] + extra_specs,
        out_specs=pl.BlockSpec((1, TILE, ow), lambda i, j: (i, jnp.maximum(j - 1, 0) + 1, 0)),
        out_shape=jax.ShapeDtypeStruct((b, n_tok, ow), BF16),
        scratch_shapes=[pltpu.VMEM((ow, TILE), F32), pltpu.VMEM((2, 2, n_tok, TILE), F32),
                        pltpu.VMEM((2, 1, TILE), F32)],
        compiler_params=_cparams(("arbitrary", "arbitrary")),
        name=name,
    )(q, k, vt, *extra_args)
    if not need_ctx:
        return y
    return pl.pallas_call(
        functools.partial(kern, n_q_tiles=None),
        grid=(b,),
        in_specs=[pl.BlockSpec((1, TILE, w), lambda i: (i, 0, 0)),
                  pl.BlockSpec((1, n_heads, TILE, LANES), lambda i: (i, 0, 0, 0)),
                  pl.BlockSpec((1, vrows, TILE), lambda i: (i, 0, 0)),
                  pl.BlockSpec(memory_space=pl.ANY)]
                 + [pl.BlockSpec(s.block_shape, lambda i: (0, 0)) for s in extra_specs],
        out_specs=pl.BlockSpec((1, TILE, ow), lambda i: (i, 0, 0)),
        out_shape=jax.ShapeDtypeStruct((b, n_tok, ow), BF16),
        input_output_aliases={3: 0},
        scratch_shapes=[pltpu.VMEM((ow, TILE), F32), pltpu.VMEM((2, 2, TILE, TILE), F32)],
        compiler_params=_cparams(("arbitrary",)),
        name=name + "_ctx",
    )(q, k, vt, y, *extra_args)


def _merge_kernel(*refs, n_src, first):
    (mod_ref, gof_ref, gob_ref, gz_ref, mla_ref, dif_ref, syf_ref, syb_ref, sz_ref,
     gates01_ref, gates23_ref, gng_ref, sng_ref, wb_ref, wo_ref, o_ref) = refs[n_src:]
    o = gof_ref[0] + gob_ref[0]
    z = gz_ref[0]
    ya = []
    for h in range(GDN_HEADS):
        hl = slice(h * GDN_DV, (h + 1) * GDN_DV)
        ya.append(_rms(o[:, hl], gng_ref[...]) * _silu(z[:, hl]))
    ya = jnp.concatenate(ya, axis=-1)
    y = (syf_ref[0] + syb_ref[0]) * _silu(sz_ref[0])
    gsz = SSM_INNER // SSM_GROUPS
    yd = jnp.concatenate([_rms(y[:, g * gsz:(g + 1) * gsz], sng_ref[:, g * gsz:(g + 1) * gsz])
                          for g in range(SSM_GROUPS)], axis=-1)
    ys = (ya, mla_ref[0], dif_ref[0], yd)
    m = None
    for i in range(N_BRANCH):
        gates_ref = gates01_ref if i < 2 else gates23_ref
        gate = _sigmoid(gates_ref[0, :, (i % 2) * D_MODEL:(i % 2 + 1) * D_MODEL].astype(F32))
        term = gate * _dot(ys[i].astype(BF16), wb_ref[i])
        m = term if m is None else m + term
    o_ref[0] = _stream_tile(refs[:n_src], first) + mod_ref[2:3, :] * _dot(m.astype(BF16), wo_ref[...])


def _merge(xs, mods, gdn_o, u, gates, y_mla, y_diff, ssd_y, gng, sng, wb, wo, *, need_ctx):
    first = 0 if need_ctx else 1
    src_specs, src_args, n_tok = _stream_specs(xs, first)
    b, d = mods.shape[0], mods.shape[-1]
    t = n_tok // TILE
    row = lambda w, cb=0: pl.BlockSpec((1, TILE, w), lambda i, j: (i, j + first, cb))
    const = lambda shape: pl.BlockSpec(shape, lambda i, j: (0,) * len(shape))
    aliases = {0: 0} if len(src_args) == 1 else {}
    assert need_ctx or len(src_args) == 1
    return pl.pallas_call(
        functools.partial(_merge_kernel, n_src=len(src_args), first=first),
        grid=(b, t - first),
        in_specs=src_specs + [
                  pl.BlockSpec((None, None, 6, d), lambda i, j: (i, jnp.minimum(j + first, 1), 0, 0)),
                  row(BRANCH_W), row(BRANCH_W), row(BRANCH_W, (C_GDN + 3 * BRANCH_W) // BRANCH_W),
                  row(BRANCH_W), row(BRANCH_W), row(BRANCH_W), row(BRANCH_W), row(BRANCH_W, C_SSMZ // BRANCH_W),
                  row(GATE_COLS // 2, 0), row(GATE_COLS // 2, 1),
                  const((1, GDN_DV)), const((1, SSM_INNER)), const((N_BRANCH, BRANCH_W, d)), const((d, d))],
        out_specs=row(d),
        out_shape=jax.ShapeDtypeStruct((b, n_tok, d), F32),
        input_output_aliases=aliases,
        compiler_params=_cparams(("arbitrary", "arbitrary")),
        name="merge",
    )(*src_args, mods, gdn_o[0], gdn_o[1], u, y_mla, y_diff, ssd_y[0], ssd_y[1], u,
      gates, gates, gng, sng, wb, wo)


def _mlp_kernel(x_ref, mod_ref, g_ref, w1_ref, w2_ref, o_ref):
    x = x_ref[0]
    h = _rms(x, g_ref[...]) * (1.0 + mod_ref[4:5, :]) + mod_ref[3:4, :]
    a = jnp.maximum(_dot(h.astype(BF16), w1_ref[...]), 0.0)
    o_ref[0] = x + mod_ref[5:6, :] * _dot((a * a).astype(BF16), w2_ref[...])


def _mlp(xs, mods, g, w1, w2, *, need_ctx):
    b, n_tok, d = xs.shape
    t = n_tok // TILE
    first = 0 if need_ctx else 1
    const = lambda shape: pl.BlockSpec(shape, lambda i, j: (0,) * len(shape))
    n_out = n_tok - first * TILE
    return pl.pallas_call(
        _mlp_kernel,
        grid=(b, t - first),
        in_specs=[pl.BlockSpec((1, TILE, d), lambda i, j: (i, j + first, 0)),
                  pl.BlockSpec((None, None, 6, d), lambda i, j: (i, jnp.minimum(j + first, 1), 0, 0)),
                  const((1, d)), const((d, D_FF)), const((D_FF, d))],
        out_specs=pl.BlockSpec((1, TILE, d), lambda i, j: (i, j, 0)),
        out_shape=jax.ShapeDtypeStruct((b, n_out, d), F32),
        compiler_params=_cparams(("arbitrary", "arbitrary")),
        name="mlp",
    )(xs, mods, g, w1, w2)


def _repack_w_in(w):
    d = w.shape[0]
    z = lambda n: jnp.zeros((d, n), w.dtype)
    o_mla = GDN_COLS
    o_diff = o_mla + MLA_COLS
    o_ssm = o_diff + DIFF_COLS
    o_gate = MIX_COLS
    gdn_ab = w[:, 2 * GDN_QK + 2 * GDN_VW:GDN_COLS]
    ssm_dt = w[:, o_ssm + 2 * SSM_INNER + 2 * SSM_BC:o_ssm + SSM_COLS]
    kpe = w[:, o_mla + MLA_Q_LORA + MLA_KV_LORA:o_mla + MLA_COLS]
    parts = [
        w[:, 0:2 * GDN_QK + 2 * GDN_VW],
        w[:, o_ssm + SSM_INNER:o_ssm + 2 * SSM_INNER + 2 * SSM_BC],
        w[:, o_diff:o_diff + DIFF_COLS],
        w[:, o_mla:o_mla + MLA_Q_LORA + MLA_KV_LORA], z(MLA_NOPE), kpe, z(LANES - MLA_QK),
        gdn_ab, ssm_dt, z(LANES - 4 * GDN_HEADS - 2 * SSM_HEADS),
        z(C_SSMZ - C_SMALL - LANES),
        w[:, o_ssm:o_ssm + SSM_INNER],
        w[:, o_gate:o_gate + GATE_COLS],
    ]
    out = jnp.concatenate(parts, axis=1)
    assert out.shape[1] == N_IN
    return out.astype(BF16)


def _lane_row(vals, offset):
    row = jnp.zeros((1, LANES), F32)
    return row.at[0, offset:offset + vals.shape[0]].set(vals.astype(F32))


def _rope_tables(n_lat, rot_dim, lane_offset, period):
    rows = n_lat // GRID_W
    row = jnp.repeat(jnp.arange(rows, dtype=F32), GRID_W)
    col = jnp.tile(jnp.arange(GRID_W, dtype=F32), rows)
    quarter = rot_dim // 4
    inv = ROPE_THETA ** (-jnp.arange(quarter, dtype=F32) / quarter)
    ar = row[:, None] * inv
    ac = col[:, None] * inv
    ang = jnp.concatenate([ar, ar, ac, ac], axis=-1)
    cos, sin = jnp.cos(ang), jnp.sin(ang)
    first = (np.arange(rot_dim) % (2 * quarter)) < quarter
    sin_a = jnp.where(first, -sin, 0.0)
    sin_b = jnp.where(first, 0.0, sin)
    reps = LANES // period
    def place(t, fill):
        blk = jnp.full((n_lat, period), fill, F32).at[:, lane_offset:lane_offset + rot_dim].set(t)
        blk = jnp.tile(blk, (1, reps))
        ctx = jnp.full((CTX_LEN, LANES), fill, F32)
        return jnp.concatenate([ctx, blk], axis=0)
    return jnp.stack([place(cos, 1.0), place(sin_a, 0.0), place(sin_b, 0.0)])


def kernel(x, c, ctx, c_ctx, ada_w, ada_b, norm1_g, norm2_g, w_in, gdn_conv, gdn_a_log, gdn_dt_bias, gdn_norm_g,
           mla_q_lora_g, mla_kv_lora_g, mla_w_uq, mla_w_ukv, mla_qn_g, mla_kn_g, diff_qn_g, diff_kn_g,
           diff_lambda, diff_sub_g, ssm_conv, ssm_conv_b, ssm_a_log, ssm_dt_bias, ssm_d, ssm_norm_g, w_branch,
           w_out, mlp_w1, mlp_w2):
    b, n_lat, d = x.shape
    depth = w_in.shape[0]
    assert ctx.shape[1] == CTX_LEN == TILE and n_lat % TILE == 0 and d == D_MODEL
    xs = (ctx, x) if depth > 1 else jnp.concatenate([ctx, x], axis=1)
    tab_mla = _rope_tables(n_lat, MLA_ROPE, MLA_NOPE, LANES)
    tab_diff = _rope_tables(n_lat, DIFF_HD, 0, DIFF_HD)
    rows = b + 8
    cc = jnp.zeros((rows, d), F32).at[:b].set(c).at[b].set(c_ctx)
    row2 = lambda v: v.reshape(1, -1).astype(F32)

    for l in range(depth):
        need_ctx = l < depth - 1
        lam_init = LAMBDA_BASE - LAMBDA_AMP * math.exp(-LAMBDA_RATE * l)
        mod = _ada(cc, ada_w, ada_b.reshape(depth, 1, -1), l)
        mod_lat = mod[:b].reshape(b, 6, d)
        mod_ctx = jnp.broadcast_to(mod[b].reshape(1, 6, d), (b, 6, d))
        mods = jnp.stack([mod_ctx, mod_lat], axis=1)

        u, gates = _inproj(xs, mods, row2(norm1_g[l]), _repack_w_in(w_in[l]))

        gdn_o = _gdn_scan(_gdn_prep(u, gdn_conv[l].astype(F32),
                                    _lane_row(gdn_a_log[l].reshape(-1), SM_A),
                                    _lane_row(gdn_dt_bias[l].reshape(-1), SM_A)))

        wq = mla_w_uq[l].reshape(MLA_Q_LORA, MLA_HEADS, MLA_QK)
        wq = jnp.pad(wq, ((0, 0), (0, 0), (0, LANES - MLA_QK))).reshape(MLA_Q_LORA, MLA_HEADS * LANES)
        wkv = mla_w_ukv[l].reshape(MLA_KV_LORA, MLA_HEADS, MLA_NOPE + MLA_V)
        wk = jnp.pad(wkv[:, :, :MLA_NOPE], ((0, 0), (0, 0), (0, LANES - MLA_NOPE))).reshape(MLA_KV_LORA, MLA_HEADS * LANES)
        wvt = wkv[:, :, MLA_NOPE:].reshape(MLA_KV_LORA, MLA_HEADS * MLA_V).T
        pad_g = lambda g: jnp.pad(g.astype(F32), (0, LANES - MLA_QK)).reshape(1, LANES)
        q_m, k_m, vt_m = _mla_prep(u, tab_mla, row2(mla_q_lora_g[l]), row2(mla_kv_lora_g[l]),
                                   wq.astype(BF16), wk.astype(BF16), wvt.astype(BF16),
                                   pad_g(mla_qn_g[l]), pad_g(mla_kn_g[l]))
        y_mla = _attention(q_m, k_m, vt_m, n_heads=MLA_HEADS, dv=MLA_V, v_ones=V_ONES, need_ctx=need_ctx)

        rep_g = lambda g: jnp.tile(g.astype(F32), LANES // DIFF_HD).reshape(1, LANES)
        q_d, k_d, vt_d = _diff_prep(u, tab_diff, rep_g(diff_qn_g[l]), rep_g(diff_kn_g[l]))
        y_diff = _attention(q_d, k_d, vt_d, n_heads=DIFF_HEADS, dv=2 * DIFF_HD, v_ones=0, need_ctx=need_ctx, diff=True,
                            lam_p=diff_lambda[l].astype(F32), sub_g=diff_sub_g[l].astype(F32).reshape(-1, 1),
                            lam_init=lam_init)

        ssd_y = _ssd(u, ssm_conv[l].astype(F32), row2(ssm_conv_b[l]),
                     _lane_row(ssm_a_log[l].reshape(-1), SM_DT), _lane_row(ssm_dt_bias[l].reshape(-1), SM_DT),
                     jnp.repeat(ssm_d[l].astype(F32), SSM_HEAD_DIM).reshape(1, -1))

        xs = _merge(xs, mods, gdn_o, u, gates, y_mla, y_diff, ssd_y, row2(gdn_norm_g[l]), row2(ssm_norm_g[l]),
                    w_branch[l].astype(BF16), w_out[l].astype(BF16), need_ctx=need_ctx)
        xs = _mlp(xs, mods, row2(norm2_g[l]), mlp_w1[l].astype(BF16), mlp_w2[l].astype(BF16), need_ctx=need_ctx)
    return xs
```
